```python
import math
import jax
import jax.numpy as jnp
from jax import lax
import numpy as np

D_MODEL = 1024
BATCH = 8
SEQ = 2048
DEPTH = 2

CTX_LEN = 256
GRID_W = 64
HEAD_DIM = 64
Q_BLOCK = 128
MIX_HALF = D_MODEL // 2
A_HEADS = MIX_HALF // HEAD_DIM
A_KV_HEADS = max(1, A_HEADS // 4)
A_GROUP = A_HEADS // A_KV_HEADS
B_HEADS = MIX_HALF // (2 * HEAD_DIM)
B_V_DIM = 2 * HEAD_DIM
C_HEADS = MIX_HALF // HEAD_DIM
WIN_H = 8
WIN_W = 16
D_GROUPS = 4
D_GROUP_DIM = MIX_HALF // D_GROUPS
FFN_DIM = 4 * D_MODEL
ROPE_THETA = 10000.0
N_EVEN = (DEPTH + 1) // 2
N_ODD = DEPTH // 2
ALPHA = (2.0 * DEPTH) ** 0.25
BETA = (8.0 * DEPTH) ** -0.25
EPS = 1e-6
ATTN_SCALE = HEAD_DIM ** -0.5
QA_W = A_HEADS * HEAD_DIM
QB_W = B_HEADS * 2 * HEAD_DIM
KA_W = A_KV_HEADS * HEAD_DIM
VA_W = KA_W
KB_W = QB_W
VB_W = B_HEADS * B_V_DIM
AB_Q_W = QA_W + QB_W
AB_IN_W = AB_Q_W + KA_W + VA_W + KB_W + VB_W
AB_OUT_IN = QA_W + VB_W
C_W = C_HEADS * HEAD_DIM
CD_Q_W = C_W + MIX_HALF
CD_IN_W = CD_Q_W + 2 * C_W
CD_OUT_IN = C_W + MIX_HALF

kernel_name = 'hybrid_gqa_diff_natten_fnet_prefix_dit'


def _rms_norm(x, g):
    xf = x.astype(jnp.float32)
    y = xf * lax.rsqrt(jnp.mean(xf * xf, axis=-1, keepdims=True) + EPS)
    return (y * g.astype(jnp.float32)).astype(x.dtype)


def _layer_norm(x, g, b):
    xf = x.astype(jnp.float32)
    mu = jnp.mean(xf, axis=-1, keepdims=True)
    xc = xf - mu
    var = jnp.mean(xc * xc, axis=-1, keepdims=True)
    return (xc * lax.rsqrt(var + EPS) * g.astype(jnp.float32) + b.astype(jnp.float32)).astype(x.dtype)


def _axial_rope_tables(n, dtype):
    pos = jnp.arange(n)
    row = (pos // GRID_W).astype(jnp.float32)
    col = (pos % GRID_W).astype(jnp.float32)
    half = HEAD_DIM // 2
    freqs = jnp.power(ROPE_THETA, -jnp.arange(0, half, 2, dtype=jnp.float32) / half)
    def axis_angles(p):
        a = p[:, None] * freqs[None, :]
        return jnp.concatenate([a, a], axis=-1)
    ang = jnp.concatenate([axis_angles(row), axis_angles(col)], axis=-1)
    return jnp.cos(ang).astype(dtype), jnp.sin(ang).astype(dtype)


def _rotate_axial(x):
    xa = x.reshape(x.shape[:-1] + (2, 2, HEAD_DIM // 4))
    return jnp.concatenate([-xa[..., 1:, :], xa[..., :1, :]], axis=-2).reshape(x.shape)


def _apply_rope(x, cos, sin):
    shp = (x.shape[1],) + (1,) * (x.ndim - 3) + (x.shape[-1],)
    return x * cos.reshape(shp) + _rotate_axial(x) * sin.reshape(shp)


def _sweep(q, fn):
    b, n = q.shape[:2]
    nb = n // Q_BLOCK
    qb = jnp.moveaxis(q.reshape((b, nb, Q_BLOCK) + q.shape[2:]), 1, 0)
    out = lax.map(fn, qb)
    return jnp.moveaxis(out, 0, 1).reshape((b, n) + out.shape[3:])


def _gqa_core(q, k, v):
    s = jnp.einsum('bqkgd,bmkd->bkgqm', q, k).astype(jnp.float32) * ATTN_SCALE
    p = jax.nn.softmax(s, axis=-1).astype(v.dtype)
    o = jnp.einsum('bkgqm,bmkd->bqkgd', p, v)
    return o.reshape(o.shape[:2] + (-1,))


def _diff_core(q, k, v, lam, lam_init, subln):
    s = jnp.einsum('bqhjd,bmhjd->bhjqm', q, k).astype(jnp.float32) * ATTN_SCALE
    p = jax.nn.softmax(s, axis=-1)
    a = (p[:, :, 0] - lam * p[:, :, 1]).astype(v.dtype)
    o = jnp.einsum('bhqm,bmhe->bqhe', a, v)
    o = _rms_norm(o, subln) * (1.0 - lam_init)
    return o.reshape(o.shape[:2] + (-1,))


def _neighbourhood_attention(q, k, v, k_ctx, v_ctx, rpb):
    b, n, h, d = q.shape
    rows = n // GRID_W
    kh = min(WIN_H, rows)
    r = jnp.arange(rows)
    cidx = jnp.arange(GRID_W)
    row_idx = jnp.clip(r - kh // 2, 0, rows - kh)[:, None] + jnp.arange(kh)[None, :]
    col_idx = jnp.clip(cidx - WIN_W // 2, 0, GRID_W - WIN_W)[:, None] + jnp.arange(WIN_W)[None, :]
    dr = (row_idx - r[:, None] + WIN_H - 1)[:, None, :, None]
    dc = (col_idx - cidx[:, None] + WIN_W - 1)[None, :, None, :]
    bias = rpb[:, dr, dc].astype(jnp.float32)
    kg = k.reshape(b, rows, GRID_W, h, d)
    vg = v.reshape(b, rows, GRID_W, h, d)
    rb = Q_BLOCK // GRID_W
    nb = rows // rb
    qg = jnp.moveaxis(q.reshape(b, nb, rb, GRID_W, h, d), 1, 0)
    ridx = row_idx.reshape(nb, rb, kh)
    bb = jnp.moveaxis(bias.reshape(h, nb, rb, GRID_W, kh, WIN_W), 1, 0)
    nwin = kh * WIN_W

    def block(args):
        qblk, ri, bi = args
        k_win = kg[:, ri][:, :, :, col_idx]
        v_win = vg[:, ri][:, :, :, col_idx]
        s_win = jnp.einsum('brchd,brkcwhd->bhrckw', qblk, k_win).astype(jnp.float32) * ATTN_SCALE + bi
        s_ctx = jnp.einsum('brchd,blhd->bhrcl', qblk, k_ctx).astype(jnp.float32) * ATTN_SCALE
        s = jnp.concatenate([s_win.reshape(b, h, rb, GRID_W, nwin), s_ctx], axis=-1)
        p = jax.nn.softmax(s, axis=-1).astype(v.dtype)
        p_win = p[..., :nwin].reshape(b, h, rb, GRID_W, kh, WIN_W)
        return (jnp.einsum('bhrckw,brkcwhd->brchd', p_win, v_win)
                + jnp.einsum('bhrcl,blhd->brchd', p[..., nwin:], v_ctx))

    out = lax.map(block, (qg, ridx, bb))
    return jnp.moveaxis(out, 0, 1).reshape(b, n, h * d)


def _fourier_mix(f):
    b, t, _ = f.shape
    fg = f.reshape(b, t, D_GROUPS, D_GROUP_DIM).astype(jnp.float32)
    y = jnp.fft.fft2(fg, axes=(1, 3), norm='ortho').real
    return y.reshape(b, t, MIX_HALF).astype(f.dtype)


def _ffn(u, w1, w2):
    hid = jax.nn.relu(u @ w1)
    return (hid * hid) @ w2


def _mixer_ab(ux, uc, w_in, w_out, q_norm, k_norm, lam_p, subln, lam_init, cos, sin, last):
    b, n, _ = ux.shape
    l = uc.shape[1]

    def split_q(hq, t):
        qa = hq[..., :QA_W].reshape(b, t, A_KV_HEADS, A_GROUP, HEAD_DIM)
        qb = hq[..., QA_W:].reshape(b, t, B_HEADS, 2, HEAD_DIM)
        return qa, qb

    def split_kv(hkv, t):
        o1 = KA_W
        o2 = o1 + VA_W
        o3 = o2 + KB_W
        ka = hkv[..., :o1].reshape(b, t, A_KV_HEADS, HEAD_DIM)
        va = hkv[..., o1:o2].reshape(b, t, A_KV_HEADS, HEAD_DIM)
        kb = hkv[..., o2:o3].reshape(b, t, B_HEADS, 2, HEAD_DIM)
        vb = hkv[..., o3:].reshape(b, t, B_HEADS, B_V_DIM)
        return ka, va, kb, vb

    hx = ux @ w_in
    qa, qb = split_q(hx[..., :AB_Q_W], n)
    ka, va, kb, vb = split_kv(hx[..., AB_Q_W:], n)
    ka_c, va_c, kb_c, vb_c = split_kv(uc @ w_in[:, AB_Q_W:], l)
    ka_c = _rms_norm(ka_c, k_norm)
    lp = lam_p.astype(jnp.float32)
    lam = jnp.exp(jnp.sum(lp[0] * lp[1])) - jnp.exp(jnp.sum(lp[2] * lp[3])) + lam_init

    qa = _apply_rope(_rms_norm(qa, q_norm), cos, sin)
    ka_all = jnp.concatenate([_apply_rope(_rms_norm(ka, k_norm), cos, sin), ka_c], axis=1)
    va_all = jnp.concatenate([va, va_c], axis=1)
    qb = _apply_rope(qb, cos, sin)
    kb_all = jnp.concatenate([_apply_rope(kb, cos, sin), kb_c], axis=1)
    vb_all = jnp.concatenate([vb, vb_c], axis=1)
    o_a = _sweep(qa, lambda qblk: _gqa_core(qblk, ka_all, va_all))
    o_b = _sweep(qb, lambda qblk: _diff_core(qblk, kb_all, vb_all, lam, lam_init, subln))
    out_x = jnp.concatenate([o_a, o_b], axis=-1) @ w_out
    if last:
        return out_x, None
    qa_c, qb_c = split_q(uc @ w_in[:, :AB_Q_W], l)
    o_a_c = _gqa_core(_rms_norm(qa_c, q_norm), ka_c, va_c)
    o_b_c = _diff_core(qb_c, kb_c, vb_c, lam, lam_init, subln)
    out_c = jnp.concatenate([o_a_c, o_b_c], axis=-1) @ w_out
    return out_x, out_c


def _mixer_cd(ux, uc, w_in, w_out, rpb, last):
    b, n, _ = ux.shape
    l = uc.shape[1]
    hx = ux @ w_in
    qc = hx[..., :C_W].reshape(b, n, C_HEADS, HEAD_DIM)
    fd = hx[..., C_W:CD_Q_W]
    kc = hx[..., CD_Q_W:CD_Q_W + C_W].reshape(b, n, C_HEADS, HEAD_DIM)
    vc = hx[..., CD_Q_W + C_W:].reshape(b, n, C_HEADS, HEAD_DIM)
    hc = uc @ w_in[:, CD_Q_W:]
    kc_c = hc[..., :C_W].reshape(b, l, C_HEADS, HEAD_DIM)
    vc_c = hc[..., C_W:].reshape(b, l, C_HEADS, HEAD_DIM)
    o_c = _neighbourhood_attention(qc, kc, vc, kc_c, vc_c, rpb)
    o_d = _fourier_mix(fd)
    out_x = jnp.concatenate([o_c, o_d], axis=-1) @ w_out
    if last:
        return out_x, None
    hq_c = uc @ w_in[:, :CD_Q_W]
    qc_c = hq_c[..., :C_W].reshape(b, l, C_HEADS, 1, HEAD_DIM)
    o_c_c = _gqa_core(qc_c, kc_c, vc_c)
    o_d_c = _fourier_mix(hq_c[..., C_W:])
    out_c = jnp.concatenate([o_c_c, o_d_c], axis=-1) @ w_out
    return out_x, out_c


def setup_inputs(seed: int = 0) -> dict:
    key = jax.random.key(seed)
    ks = jax.random.split(key, 19)

    def nrm(k, shape, s):
        return jax.random.normal(k, shape, jnp.float32) * s

    return {
        'x': nrm(ks[0], (BATCH, SEQ, D_MODEL), 1.0),
        'c': nrm(ks[1], (BATCH, D_MODEL), 1.0),
        'ctx': nrm(ks[2], (BATCH, CTX_LEN, D_MODEL), 1.0),
        'c_ctx': nrm(ks[3], (D_MODEL,), 1.0),
        'mod_w': nrm(ks[4], (DEPTH, D_MODEL, 6 * D_MODEL), 0.5 * D_MODEL ** -0.5),
        'mod_b': nrm(ks[5], (DEPTH, 6 * D_MODEL), 0.02),
        'ln_g': 1.0 + nrm(ks[6], (DEPTH, 2, D_MODEL), 0.02),
        'ln_b': nrm(ks[7], (DEPTH, 2, D_MODEL), 0.02),
        'ffn_w1': nrm(ks[8], (DEPTH, D_MODEL, FFN_DIM), D_MODEL ** -0.5),
        'ffn_w2': nrm(ks[9], (DEPTH, FFN_DIM, D_MODEL), BETA * FFN_DIM ** -0.5),
        'ab_w_in': nrm(ks[10], (N_EVEN, D_MODEL, AB_IN_W), D_MODEL ** -0.5),
        'ab_w_out': nrm(ks[11], (N_EVEN, AB_OUT_IN, D_MODEL), BETA * AB_OUT_IN ** -0.5),
        'a_q_norm': 1.0 + nrm(ks[12], (N_EVEN, HEAD_DIM), 0.02),
        'a_k_norm': 1.0 + nrm(ks[13], (N_EVEN, HEAD_DIM), 0.02),
        'b_lambda': nrm(ks[14], (N_EVEN, 4, HEAD_DIM), 0.1),
        'b_subln': 1.0 + nrm(ks[15], (N_EVEN, B_V_DIM), 0.02),
        'cd_w_in': nrm(ks[16], (N_ODD, D_MODEL, CD_IN_W), D_MODEL ** -0.5),
        'cd_w_out': nrm(ks[17], (N_ODD, CD_OUT_IN, D_MODEL), BETA * CD_OUT_IN ** -0.5),
        'c_rpb': nrm(ks[18], (N_ODD, C_HEADS, 2 * WIN_H - 1, 2 * WIN_W - 1), 0.1),
    }


def reference(x, c, ctx, c_ctx, mod_w, mod_b, ln_g, ln_b, ffn_w1, ffn_w2, ab_w_in, ab_w_out,
              a_q_norm, a_k_norm, b_lambda, b_subln, cd_w_in, cd_w_out, c_rpb):
    cos, sin = _axial_rope_tables(x.shape[1], x.dtype)
    sc = jax.nn.silu(c)
    scc = jax.nn.silu(c_ctx)
    for i in range(DEPTH):
        last = i == DEPTH - 1
        j = i // 2
        mx = (sc @ mod_w[i] + mod_b[i])[:, None, :]
        sh1, s1, g1, sh2, s2, g2 = jnp.split(mx, 6, axis=-1)
        n_ctx_mod = 2 if last else 6
        mc = scc @ mod_w[i][:, :n_ctx_mod * D_MODEL] + mod_b[i][:n_ctx_mod * D_MODEL]
        mcs = jnp.split(mc, n_ctx_mod)
        ux = x * (1 + s1) + sh1
        uc = ctx * (1 + mcs[1]) + mcs[0]
        if i % 2 == 0:
            lam_init = 0.8 - 0.6 * math.exp(-0.3 * i)
            ox, oc = _mixer_ab(ux, uc, ab_w_in[j], ab_w_out[j], a_q_norm[j], a_k_norm[j],
                               b_lambda[j], b_subln[j], lam_init, cos, sin, last)
        else:
            ox, oc = _mixer_cd(ux, uc, cd_w_in[j], cd_w_out[j], c_rpb[j], last)
        x = _layer_norm(ALPHA * x + g1 * ox, ln_g[i, 0], ln_b[i, 0])
        x = _layer_norm(ALPHA * x + g2 * _ffn(x * (1 + s2) + sh2, ffn_w1[i], ffn_w2[i]), ln_g[i, 1], ln_b[i, 1])
        if not last:
            ctx = _layer_norm(ALPHA * ctx + mcs[2] * oc, ln_g[i, 0], ln_b[i, 0])
            ctx = _layer_norm(ALPHA * ctx + mcs[5] * _ffn(ctx * (1 + mcs[4]) + mcs[3], ffn_w1[i], ffn_w2[i]),
                              ln_g[i, 1], ln_b[i, 1])
    return x
```

```python
import functools
import math

import numpy as np
import jax
import jax.numpy as jnp
from jax import lax
from jax.experimental import pallas as pl
from jax.experimental.pallas import tpu as pltpu

F32 = jnp.float32
BF16 = jnp.bfloat16

D_MODEL = 1024
SEQ = 2048
CTX_LEN = 256
DEPTH = 2
GRID_W = 64
GRID_H = SEQ // GRID_W
HEAD_DIM = 64
A_HEADS = 8
A_KV_HEADS = 2
B_HEADS = 4
C_HEADS = 8
WIN_H = 8
WIN_W = 16
D_GROUPS = 4
D_GROUP_DIM = 128
FFN_DIM = 4 * D_MODEL
ROPE_THETA = 10000.0
ALPHA = (2.0 * DEPTH) ** 0.25
EPS = 1e-6
ATTN_SCALE = HEAD_DIM ** -0.5
LAM_INIT0 = 0.8 - 0.6 * math.exp(-0.3 * 0)
NEG = -1e30

LANES = 128
MOD_ROWS = 16
VMEM_LIMIT = 56 * 1024 * 1024

QA_PERM = (0, 4, 1, 5, 2, 6, 3, 7)

NAT_ROWS = 2
NAT_Q = NAT_ROWS * GRID_W
NAT_WIN_ROWS = 10
NAT_WIN = NAT_WIN_ROWS * GRID_W
NAT_TILES = NAT_WIN_ROWS // 2
CB_ENTRIES = 18
CB_OFFSET = 2


def _cparams(sem):
    return pltpu.CompilerParams(dimension_semantics=sem, vmem_limit_bytes=VMEM_LIMIT)


def _dot(a, b):
    return jnp.dot(a, b, preferred_element_type=F32)


def _dot_nt(a, b):
    return lax.dot_general(a, b, (((1,), (1,)), ((), ())), preferred_element_type=F32)


def _split3(a):
    hi = a.astype(BF16)
    r1 = a - hi.astype(F32)
    mid = r1.astype(BF16)
    lo = (r1 - mid.astype(F32)).astype(BF16)
    return hi, mid, lo


def _layer_norm(z, g, b):
    mu = jnp.mean(z, axis=-1, keepdims=True)
    zc = z - mu
    var = jnp.mean(zc * zc, axis=-1, keepdims=True)
    return zc * lax.rsqrt(var + EPS) * g + b


@functools.lru_cache(maxsize=None)
def _rope_tables():
    pos = np.arange(SEQ)
    row = (pos // GRID_W).astype(np.float64)
    col = (pos % GRID_W).astype(np.float64)
    half = HEAD_DIM // 2
    freqs = np.power(ROPE_THETA, -np.arange(0, half, 2, dtype=np.float64) / half)
    def axis_angles(p):
        a = p[:, None] * freqs[None, :]
        return np.concatenate([a, a], axis=-1)
    ang = np.concatenate([axis_angles(row), axis_angles(col)], axis=-1)
    cos, sin = np.cos(ang), np.sin(ang)
    first = (np.arange(HEAD_DIM) % half) < (half // 2)
    sin_a = np.where(first[None, :], -sin, 0.0)
    sin_b = np.where(first[None, :], 0.0, sin)
    tile = lambda t: np.tile(t, (1, LANES // HEAD_DIM)).astype(np.float32)
    return tile(cos), tile(sin_a), tile(sin_b)


@functools.lru_cache(maxsize=None)
def _group_mean_matrix(width):
    g = np.arange(width) // HEAD_DIM
    return (g[:, None] == g[None, :]).astype(np.float32) / HEAD_DIM


@functools.lru_cache(maxsize=None)
def _dft_tables():
    t = np.arange(SEQ, dtype=np.int64)
    k = (t[:, None] * t[None, :]) % SEQ
    ang = 2.0 * np.pi * k.astype(np.float64) / SEQ
    ct = np.cos(ang).astype(np.float32)
    sn = (-np.sin(ang)).astype(np.float32)
    c = np.arange(D_GROUP_DIM, dtype=np.int64)
    kc = (c[:, None] * c[None, :]) % D_GROUP_DIM
    angc = 2.0 * np.pi * kc.astype(np.float64) / D_GROUP_DIM
    norm = 1.0 / math.sqrt(SEQ * D_GROUP_DIM)
    return ct, sn, (np.cos(angc) * norm).astype(np.float32), (np.sin(angc) * norm).astype(np.float32)


def _mod_kernel(c_ref, w_ref, b_ref, o_ref):
    c = c_ref[...]
    a = c / (1.0 + jnp.exp(-c))
    a_hi, a_mid, _ = _split3(a)
    w = w_ref[0]
    w_hi, w_mid, _ = _split3(w)
    o_ref[0] = _dot(a_hi, w_hi) + (_dot(a_mid, w_hi) + _dot(a_hi, w_mid)) + b_ref[0]


def _modulation(cs, mod_w, mod_b):
    tn = 1536
    n = 6 * D_MODEL
    return pl.pallas_call(
        _mod_kernel,
        grid=(DEPTH, n // tn),
        in_specs=[pl.BlockSpec((MOD_ROWS, D_MODEL), lambda l, j: (0, 0)),
                  pl.BlockSpec((1, D_MODEL, tn), lambda l, j: (l, 0, j)),
                  pl.BlockSpec((1, 1, tn), lambda l, j: (l, 0, j))],
        out_specs=pl.BlockSpec((1, MOD_ROWS, tn), lambda l, j: (l, 0, j)),
        out_shape=jax.ShapeDtypeStruct((DEPTH, MOD_ROWS, n), F32),
        compiler_params=_cparams(("arbitrary", "arbitrary")),
        name="modulation",
    )(cs, mod_w, mod_b.reshape(DEPTH, 1, n))


def _tok_spec(tm, width):
    return pl.BlockSpec((1, tm, width), lambda g, i: (g, i, 0))


def _mod_spec(mod_row):
    if mod_row is None:
        return pl.BlockSpec((1, 6, D_MODEL), lambda g, i: (g, 0, 0))
    return pl.BlockSpec((1, 6, D_MODEL), lambda g, i: (mod_row, 0, 0))


def _const_spec(shape):
    nd = len(shape)
    return pl.BlockSpec(shape, lambda g, i: (0,) * nd)


AB_QA, AB_QB, AB_KA, AB_KB, AB_VA, AB_VB = 0, 512, 1024, 1152, 1664, 1792
AB_W = 2304


def _rope_chunk(c, cos, sin_a, sin_b):
    return c * cos + pltpu.roll(c, LANES - 16, axis=1) * sin_a + pltpu.roll(c, 16, axis=1) * sin_b


def _inproj_ab_kernel(*refs, rope):
    if rope:
        (x_ref, mod_ref, w_ref, gq_ref, gk_ref, g512_ref, g128_ref, cos_ref, sa_ref, sb_ref,
         qa_ref, qb_ref, ka_ref, kb_ref, va_ref, vb_ref) = refs
        cos, sin_a, sin_b = cos_ref[...], sa_ref[...], sb_ref[...]
    else:
        (x_ref, mod_ref, w_ref, gq_ref, gk_ref, g512_ref, g128_ref,
         qa_ref, qb_ref, ka_ref, kb_ref, va_ref, vb_ref) = refs
    x = x_ref[0]
    u = (x * (1.0 + mod_ref[0, 1:2, :]) + mod_ref[0, 0:1, :]).astype(BF16)
    h = _dot(u, w_ref[...])

    def rms(v, g_ref, gm_ref):
        ms = _dot((v * v).astype(BF16), gm_ref[...])
        return v * lax.rsqrt(ms + EPS) * g_ref[...]

    def emit(v, o_ref, scale):
        for j in range(v.shape[1] // LANES):
            c = v[:, j * LANES:(j + 1) * LANES]
            if rope:
                c = _rope_chunk(c, cos, sin_a, sin_b)
            if scale != 1.0:
                c = c * scale
            o_ref[0, :, j * LANES:(j + 1) * LANES] = c.astype(o_ref.dtype)

    emit(rms(h[:, AB_QA:AB_QB], gq_ref, g512_ref), qa_ref, ATTN_SCALE)
    emit(h[:, AB_QB:AB_KA], qb_ref, ATTN_SCALE)
    emit(rms(h[:, AB_KA:AB_KB], gk_ref, g128_ref), ka_ref, 1.0)
    emit(h[:, AB_KB:AB_VA], kb_ref, 1.0)
    va_ref[0] = h[:, AB_VA:AB_VB].astype(BF16)
    vb_ref[0] = h[:, AB_VB:AB_W].astype(BF16)


def _inproj_ab(tok, mod, mod_row, w, gq, gk, rope_tabs, tm, name):
    g, r, _ = tok.shape
    rope = rope_tabs is not None
    consts = [w, gq, gk, jnp.asarray(_group_mean_matrix(512), BF16), jnp.asarray(_group_mean_matrix(128), BF16)]
    in_specs = [_tok_spec(tm, D_MODEL), _mod_spec(mod_row)] + [_const_spec(c.shape) for c in consts]
    args = [tok, mod] + consts
    if rope:
        in_specs += [pl.BlockSpec((tm, LANES), lambda gg, i: (i, 0))] * 3
        args += list(rope_tabs)
    widths = (512, 512, 128, 512, 128, 512)
    return pl.pallas_call(
        functools.partial(_inproj_ab_kernel, rope=rope),
        grid=(g, r // tm),
        in_specs=in_specs,
        out_specs=[_tok_spec(tm, wd) for wd in widths],
        out_shape=[jax.ShapeDtypeStruct((g, r, wd), BF16) for wd in widths],
        compiler_params=_cparams(("parallel", "parallel")),
        name=name,
    )(*args)


def _softmax_pv(q2, kv_pairs):
    scores = [_dot_nt(q2, k) for k, _ in kv_pairs]
    m = scores[0].max(axis=-1, keepdims=True)
    for s in scores[1:]:
        m = jnp.maximum(m, s.max(axis=-1, keepdims=True))
    acc = None
    den = None
    for s, (_, v) in zip(scores, kv_pairs):
        e = jnp.exp(s - m)
        d = e.sum(axis=-1, keepdims=True)
        a = _dot(e.astype(BF16), v)
        acc = a if acc is None else acc + a
        den = d if den is None else den + d
    return acc / den


def _split_pair(qp, lo):
    zero = jnp.zeros_like(qp)
    return jnp.concatenate([jnp.where(lo, qp, zero), jnp.where(lo, zero, qp)], axis=0)


def _attn_ab_kernel(*refs, with_x):
    if with_x:
        (qa_ref, qb_ref, kax_ref, vax_ref, kbx_ref, vbx_ref, kac_ref, vac_ref, kbc_ref, vbc_ref,
         lam_ref, subln_ref, o_ref) = refs
    else:
        (qa_ref, qb_ref, kac_ref, vac_ref, kbc_ref, vbc_ref, lam_ref, subln_ref, o_ref) = refs
    tq = qa_ref.shape[1]
    lo = lax.broadcasted_iota(jnp.int32, (1, LANES), 1) < HEAD_DIM

    lp = lam_ref[...]
    lam = (jnp.exp(jnp.sum(lp[0:1] * lp[1:2], axis=-1, keepdims=True))
           - jnp.exp(jnp.sum(lp[2:3] * lp[3:4], axis=-1, keepdims=True)) + LAM_INIT0)

    for p in range(A_HEADS // 2):
        sl = slice(p * LANES, (p + 1) * LANES)
        q2 = _split_pair(qa_ref[0, :, sl], lo)
        kv = [(kac_ref[0], vac_ref[0])]
        if with_x:
            kv.append((kax_ref[0], vax_ref[0]))
        o2 = _softmax_pv(q2, kv)
        o_ref[0, :, sl] = jnp.where(lo, o2[:tq], o2[tq:]).astype(o_ref.dtype)

    for i in range(B_HEADS):
        sl = slice(i * LANES, (i + 1) * LANES)
        q2 = _split_pair(qb_ref[0, :, sl], lo)
        kv = [(kbc_ref[0, :, sl], vbc_ref[0, :, sl])]
        if with_x:
            kv.append((kbx_ref[0, :, sl], vbx_ref[0, :, sl]))
        o2 = _softmax_pv(q2, kv)
        o = o2[:tq] - lam * o2[tq:]
        ms = jnp.mean(o * o, axis=-1, keepdims=True)
        o = o * lax.rsqrt(ms + EPS) * subln_ref[...] * (1.0 - LAM_INIT0)
        o_ref[0, :, 512 + i * LANES:512 + (i + 1) * LANES] = o.astype(o_ref.dtype)


def _attn_ab(q_parts, x_kv, c_kv, lam_p, subln, tq, name):
    qa, qb = q_parts
    b, nq, _ = qa.shape
    with_x = x_kv is not None

    def full(a):
        return pl.BlockSpec((1,) + a.shape[1:], lambda bb, i: (bb, 0, 0))

    args = [qa, qb]
    in_specs = [pl.BlockSpec((1, tq, 512), lambda bb, i: (bb, i, 0))] * 2
    if with_x:
        args += list(x_kv)
        in_specs += [full(a) for a in x_kv]
    args += list(c_kv) + [lam_p, subln]
    in_specs += [full(a) for a in c_kv] + [pl.BlockSpec(lam_p.shape, lambda bb, i: (0, 0)),
                                           pl.BlockSpec(subln.shape, lambda bb, i: (0, 0))]
    return pl.pallas_call(
        functools.partial(_attn_ab_kernel, with_x=with_x),
        grid=(b, nq // tq),
        in_specs=in_specs,
        out_specs=pl.BlockSpec((1, tq, D_MODEL), lambda bb, i: (bb, i, 0)),
        out_shape=jax.ShapeDtypeStruct((b, nq, D_MODEL), BF16),
        compiler_params=_cparams(("parallel", "parallel")),
        name=name,
    )(*args)


def _outproj_ln_kernel(*refs, n_parts):
    o_refs = refs[:n_parts]
    w_refs = refs[n_parts:2 * n_parts]
    x_ref, mod_ref, g_ref, b_ref, y_ref = refs[2 * n_parts:]
    y = _dot(o_refs[0][0], w_refs[0][...])
    for o_ref, w_ref in zip(o_refs[1:], w_refs[1:]):
        y = y + _dot(o_ref[0], w_ref[...])
    z = ALPHA * x_ref[0] + mod_ref[0, 2:3, :] * y
    y_ref[0] = _layer_norm(z, g_ref[...], b_ref[...])


def _outproj_ln(o_parts, w_parts, resid, mod, mod_row, ln_g, ln_b, tm, name):
    g, r, _ = resid.shape
    n = len(o_parts)
    in_specs = ([_tok_spec(tm, o.shape[2]) for o in o_parts] + [_const_spec(w.shape) for w in w_parts]
                + [_tok_spec(tm, D_MODEL), _mod_spec(mod_row), _const_spec(ln_g.shape), _const_spec(ln_b.shape)])
    return pl.pallas_call(
        functools.partial(_outproj_ln_kernel, n_parts=n),
        grid=(g, r // tm),
        in_specs=in_specs,
        out_specs=_tok_spec(tm, D_MODEL),
        out_shape=jax.ShapeDtypeStruct((g, r, D_MODEL), F32),
        compiler_params=_cparams(("parallel", "parallel")),
        name=name,
    )(*o_parts, *w_parts, resid, mod, ln_g, ln_b)


FFN_CHUNK = 1024


def _ffn_ln_kernel(x_ref, mod_ref, w1_ref, w2_ref, g_ref, b_ref, y_ref):
    x = x_ref[0]
    u = (x * (1.0 + mod_ref[0, 4:5, :]) + mod_ref[0, 3:4, :]).astype(BF16)
    acc = None
    for c in range(FFN_DIM // FFN_CHUNK):
        sl = slice(c * FFN_CHUNK, (c + 1) * FFN_CHUNK)
        h = jnp.maximum(_dot(u, w1_ref[:, sl]), 0.0)
        a = _dot((h * h).astype(BF16), w2_ref[sl, :])
        acc = a if acc is None else acc + a
    z = ALPHA * x + mod_ref[0, 5:6, :] * acc
    y_ref[0] = _layer_norm(z, g_ref[...], b_ref[...])


def _ffn_ln(tok, mod, mod_row, w1, w2, ln_g, ln_b, tm, name):
    g, r, _ = tok.shape
    single = pl.Buffered(1)
    in_specs = [_tok_spec(tm, D_MODEL), _mod_spec(mod_row),
                pl.BlockSpec(w1.shape, lambda gg, i: (0, 0), pipeline_mode=single),
                pl.BlockSpec(w2.shape, lambda gg, i: (0, 0), pipeline_mode=single),
                _const_spec(ln_g.shape), _const_spec(ln_b.shape)]
    return pl.pallas_call(
        _ffn_ln_kernel,
        grid=(g, r // tm),
        in_specs=in_specs,
        out_specs=_tok_spec(tm, D_MODEL),
        out_shape=jax.ShapeDtypeStruct((g, r, D_MODEL), F32),
        compiler_params=_cparams(("parallel", "parallel")),
        name=name,
    )(tok, mod, w1, w2, ln_g, ln_b)


def _inproj_cd_kernel(x_ref, mod_ref, w_ref, *o_refs, q_scale_first):
    x = x_ref[0]
    u = (x * (1.0 + mod_ref[0, 1:2, :]) + mod_ref[0, 0:1, :]).astype(BF16)
    h = _dot(u, w_ref[...])
    off = 0
    for k, o_ref in enumerate(o_refs):
        wd = o_ref.shape[2]
        v = h[:, off:off + wd]
        if q_scale_first and k == 0:
            v = v * ATTN_SCALE
        o_ref[0] = v.astype(o_ref.dtype)
        off += wd


def _inproj_cd(tok, mod, mod_row, w, n_out, q_scale_first, tm, name):
    g, r, _ = tok.shape
    return pl.pallas_call(
        functools.partial(_inproj_cd_kernel, q_scale_first=q_scale_first),
        grid=(g, r // tm),
        in_specs=[_tok_spec(tm, D_MODEL), _mod_spec(mod_row), _const_spec(w.shape)],
        out_specs=[_tok_spec(tm, 512)] * n_out,
        out_shape=[jax.ShapeDtypeStruct((g, r, 512), BF16)] * n_out,
        compiler_params=_cparams(("parallel", "parallel")),
        name=name,
    )(tok, mod, w)


def _dft_weight_kernel(w_ref, cc_ref, sc_ref, wa_ref, wb_ref):
    c_hi, c_mid, _ = _split3(cc_ref[...])
    s_hi, s_mid, _ = _split3(sc_ref[...])
    for gi in range(D_GROUPS):
        sl = slice(gi * D_GROUP_DIM, (gi + 1) * D_GROUP_DIM)
        w_hi, w_mid, _ = _split3(w_ref[:, sl])
        wa_ref[:, sl] = (_dot(w_hi, c_hi) + (_dot(w_mid, c_hi) + _dot(w_hi, c_mid))).astype(BF16)
        wb_ref[:, sl] = (_dot(w_hi, s_hi) + (_dot(w_mid, s_hi) + _dot(w_hi, s_mid))).astype(BF16)


def _dft_weights(w_fd, cc, sc):
    return pl.pallas_call(
        _dft_weight_kernel,
        out_shape=[jax.ShapeDtypeStruct(w_fd.shape, BF16)] * 2,
        compiler_params=pltpu.CompilerParams(vmem_limit_bytes=VMEM_LIMIT),
        name="dft_weights",
    )(w_fd, cc, sc)


DFT_ROWS = 512


def _dft_time_kernel(a_ref, b_ref, ct_ref, sn_ref, o_ref):
    a = a_ref[0]
    bm = b_ref[0]
    for r in range(SEQ // DFT_ROWS):
        sl = slice(r * DFT_ROWS, (r + 1) * DFT_ROWS)
        y = _dot(ct_ref[sl, :], a) + _dot(sn_ref[sl, :], bm)
        o_ref[0, sl, :] = y.astype(o_ref.dtype)


def _dft_time(a, bm, ct, sn):
    b = a.shape[0]
    single = pl.Buffered(1)
    tok = pl.BlockSpec((1, SEQ, 512), lambda bb: (bb, 0, 0))
    tab = pl.BlockSpec((SEQ, SEQ), lambda bb: (0, 0), pipeline_mode=single)
    return pl.pallas_call(
        _dft_time_kernel,
        grid=(b,),
        in_specs=[tok, tok, tab, tab],
        out_specs=tok,
        out_shape=jax.ShapeDtypeStruct((b, SEQ, 512), BF16),
        compiler_params=_cparams(("parallel",)),
        name="dft_time",
    )(a, bm, ct, sn)


def _bias_table_kernel(r_ref, o_ref):
    n = GRID_W * LANES
    row = lax.broadcasted_iota(jnp.int32, (2 * 32, n), 0)
    lane = lax.broadcasted_iota(jnp.int32, (2 * 32, n), 1)
    e_row, i_row = row >> 5, row & 31
    c = lane >> 7
    e_lane = (lane >> 6) & 1
    kc = lane & (GRID_W - 1)
    sel = jnp.where((e_row == e_lane) & (kc - c + (WIN_W - 1) == i_row), 1.0, 0.0).astype(BF16)
    hi, mid, lo = _split3(r_ref[...])
    t = _dot(hi, sel) + _dot(mid, sel) + _dot(lo, sel)
    c1 = c[0:1]
    kc1 = kc[0:1]
    cstart = jnp.clip(c1 - WIN_W // 2, 0, GRID_W - WIN_W)
    col_ok = (kc1 >= cstart) & (kc1 < cstart + WIN_W)
    o_ref[...] = jnp.where(col_ok, t, NEG)


def _bias_table(rpb):
    p = jnp.pad(rpb, ((0, 0), (CB_OFFSET, CB_OFFSET), (0, 1)))
    r2 = jnp.stack([p[:, 0:CB_ENTRIES], p[:, 1:CB_ENTRIES + 1]], axis=2)
    r2 = r2.reshape(C_HEADS * CB_ENTRIES, 2 * 32)
    t = pl.pallas_call(
        _bias_table_kernel,
        out_shape=jax.ShapeDtypeStruct((C_HEADS * CB_ENTRIES, GRID_W * LANES), F32),
        compiler_params=pltpu.CompilerParams(vmem_limit_bytes=VMEM_LIMIT),
        name="natten_bias_table",
    )(r2)
    return t.reshape(C_HEADS, CB_ENTRIES, GRID_W, LANES)


def _natten_kernel(q_ref, k_ref, v_ref, kc_ref, vc_ref, cb_ref, o_ref):
    r0 = NAT_ROWS * pl.program_id(1)
    ws = jnp.clip(r0 - WIN_H // 2, 0, GRID_H - NAT_WIN_ROWS)
    koff = pl.multiple_of(ws * GRID_W, LANES)
    lane = lax.broadcasted_iota(jnp.int32, (1, LANES), 1)
    lo = lane < HEAD_DIM
    e = jnp.where(lo, 0, 1)

    entry = []
    rmask = []
    for qr in range(NAT_ROWS):
        r = r0 + qr
        lo_r = jnp.clip(r - WIN_H // 2, 0, GRID_H - WIN_H) - r + (WIN_H - 1)
        ent_q, mask_q = [], []
        for j in range(NAT_TILES):
            dr0 = ws + 2 * j - r + (WIN_H - 1)
            ent_q.append(dr0 + CB_OFFSET)
            dr = dr0 + e
            ok = (dr >= lo_r) & (dr < lo_r + WIN_H)
            mask_q.append(jnp.where(ok, 0.0, NEG))
        entry.append(ent_q)
        rmask.append(mask_q)

    for p in range(C_HEADS // 2):
        sl = slice(p * LANES, (p + 1) * LANES)
        q2 = _split_pair(q_ref[0, :, sl], lo)
        kw = k_ref[0, pl.ds(koff, NAT_WIN), sl]
        vw = v_ref[0, pl.ds(koff, NAT_WIN), sl]
        s_win = _dot_nt(q2, kw)
        bias = jnp.concatenate(
            [jnp.concatenate([cb_ref[2 * p + hh, entry[qr][j]] + rmask[qr][j] for j in range(NAT_TILES)], axis=1)
             for hh in range(2) for qr in range(NAT_ROWS)], axis=0)
        s_win = s_win + bias
        s_ctx = _dot_nt(q2, kc_ref[0, :, sl])
        m = jnp.maximum(s_win.max(axis=-1, keepdims=True), s_ctx.max(axis=-1, keepdims=True))
        e_win = jnp.exp(s_win - m)
        e_ctx = jnp.exp(s_ctx - m)
        den = e_win.sum(axis=-1, keepdims=True) + e_ctx.sum(axis=-1, keepdims=True)
        acc = _dot(e_win.astype(BF16), vw) + _dot(e_ctx.astype(BF16), vc_ref[0, :, sl])
        o2 = acc / den
        o_ref[0, :, sl] = jnp.where(lo, o2[:NAT_Q], o2[NAT_Q:]).astype(o_ref.dtype)


def _natten(q, k, v, kc, vc, cb):
    b = q.shape[0]
    qspec = pl.BlockSpec((1, NAT_Q, 512), lambda bb, i: (bb, i, 0))
    full = lambda a: pl.BlockSpec((1,) + a.shape[1:], lambda bb, i: (bb, 0, 0))
    return pl.pallas_call(
        _natten_kernel,
        grid=(b, SEQ // NAT_Q),
        in_specs=[qspec, full(k), full(v), full(kc), full(vc),
                  pl.BlockSpec(cb.shape, lambda bb, i: (0, 0, 0, 0))],
        out_specs=qspec,
        out_shape=jax.ShapeDtypeStruct((b, SEQ, 512), BF16),
        compiler_params=_cparams(("parallel", "parallel")),
        name="natten",
    )(q, k, v, kc, vc, cb)


def kernel(x, c, ctx, c_ctx, mod_w, mod_b, ln_g, ln_b, ffn_w1, ffn_w2, ab_w_in, ab_w_out, a_q_norm, a_k_norm,
           b_lambda, b_subln, cd_w_in, cd_w_out, c_rpb):
    b = x.shape[0]
    nctx = b * CTX_LEN
    tm = 512
    tmc = min(tm, nctx)

    cs = jnp.concatenate([c, c_ctx[None, :], jnp.zeros((MOD_ROWS - b - 1, D_MODEL), F32)], axis=0)
    mods = _modulation(cs, mod_w, mod_b).reshape(DEPTH, MOD_ROWS, 6, D_MODEL)
    ctx_row = b

    ln_g4 = ln_g.reshape(DEPTH, 2, 1, D_MODEL)
    ln_b4 = ln_b.reshape(DEPTH, 2, 1, D_MODEL)
    w1 = ffn_w1.astype(BF16)
    w2 = ffn_w2.astype(BF16)
    cflat = ctx.reshape(1, nctx, D_MODEL)

    w_in = ab_w_in[0]
    perm = np.asarray(QA_PERM)
    w_qa = w_in[:, :512].reshape(D_MODEL, A_HEADS, HEAD_DIM)[:, perm].reshape(D_MODEL, 512)
    w_ab = jnp.concatenate([w_qa, w_in[:, 512:1024], w_in[:, 1024:1152], w_in[:, 1280:1792],
                            w_in[:, 1152:1280], w_in[:, 1792:2304]], axis=1).astype(BF16)
    w_out = ab_w_out[0]
    w_out_ab = jnp.concatenate([w_out[:512].reshape(A_HEADS, HEAD_DIM, D_MODEL)[perm].reshape(512, D_MODEL),
                                w_out[512:]], axis=0).astype(BF16)
    gq = jnp.tile(a_q_norm[0], A_HEADS)[None, :]
    gk = jnp.tile(a_k_norm[0], A_KV_HEADS)[None, :]
    rope_tabs = tuple(jnp.asarray(t) for t in _rope_tables())
    subln = b_subln[0][None, :]
    mod0 = mods[0]

    xq = _inproj_ab(x, mod0, None, w_ab, gq, gk, rope_tabs, tm, "inproj_ab_x")
    cq = _inproj_ab(cflat, mod0, ctx_row, w_ab, gq, gk, None, tmc, "inproj_ab_ctx")
    cq = [a.reshape(b, CTX_LEN, a.shape[2]) for a in cq]
    qa_x, qb_x, ka_x, kb_x, va_x, vb_x = xq
    qa_c, qb_c, ka_c, kb_c, va_c, vb_c = cq
    c_kv = (ka_c, va_c, kb_c, vb_c)
    o_x = _attn_ab((qa_x, qb_x), (ka_x, va_x, kb_x, vb_x), c_kv, b_lambda[0], subln, 256, "attn_ab_x")
    o_c = _attn_ab((qa_c, qb_c), None, c_kv, b_lambda[0], subln, CTX_LEN, "attn_ab_ctx")

    x1 = _outproj_ln([o_x], [w_out_ab], x, mod0, None, ln_g4[0, 0], ln_b4[0, 0], tm, "outproj_ab_x")
    c1 = _outproj_ln([o_c.reshape(1, nctx, D_MODEL)], [w_out_ab], cflat, mod0, ctx_row,
                     ln_g4[0, 0], ln_b4[0, 0], tmc, "outproj_ab_ctx")
    x2 = _ffn_ln(x1, mod0, None, w1[0], w2[0], ln_g4[0, 1], ln_b4[0, 1], tm, "ffn0_x")
    c2 = _ffn_ln(c1, mod0, ctx_row, w1[0], w2[0], ln_g4[0, 1], ln_b4[0, 1], tmc, "ffn0_ctx")

    mod1 = mods[1]
    w_in = cd_w_in[0]
    ct, sn, cc, sc = _dft_tables()
    w_fa, w_fb = _dft_weights(w_in[:, 512:1024], jnp.asarray(cc), jnp.asarray(sc))
    w_cd = jnp.concatenate([w_in[:, :512].astype(BF16), w_fa, w_fb, w_in[:, 1024:].astype(BF16)], axis=1)
    q_n, f_a, f_b, k_n, v_n = _inproj_cd(x2, mod1, None, w_cd, 5, True, tm, "inproj_cd_x")
    kc_n, vc_n = _inproj_cd(c2, mod1, ctx_row, w_in[:, 1024:].astype(BF16), 2, False, tmc, "inproj_cd_ctx")
    kc_n = kc_n.reshape(b, CTX_LEN, 512)
    vc_n = vc_n.reshape(b, CTX_LEN, 512)

    o_d = _dft_time(f_a, f_b, jnp.asarray(ct, BF16), jnp.asarray(sn, BF16))
    cb = _bias_table(c_rpb[0])
    o_n = _natten(q_n, k_n, v_n, kc_n, vc_n, cb)

    w_out = cd_w_out[0].astype(BF16)
    x3 = _outproj_ln([o_n, o_d], [w_out[:512], w_out[512:]], x2, mod1, None, ln_g4[1, 0], ln_b4[1, 0], tm,
                     "outproj_cd_x")
    return _ffn_ln(x3, mod1, None, w1[1], w2[1], ln_g4[1, 1], ln_b4[1, 1], tm, "ffn1_x")
```

```python
import functools
import math

import numpy as np
import jax
import jax.numpy as jnp
from jax import lax
from jax.experimental import pallas as pl
from jax.experimental.pallas import tpu as pltpu

F32 = jnp.float32
BF16 = jnp.bfloat16

D_MODEL = 1024
SEQ = 2048
CTX_LEN = 256
DEPTH = 2
GRID_W = 64
GRID_H = SEQ // GRID_W
HEAD_DIM = 64
A_HEADS = 8
A_KV_HEADS = 2
B_HEADS = 4
C_HEADS = 8
WIN_H = 8
WIN_W = 16
D_GROUPS = 4
D_GROUP_DIM = 128
FFN_DIM = 4 * D_MODEL
ROPE_THETA = 10000.0
ALPHA = (2.0 * DEPTH) ** 0.25
EPS = 1e-6
ATTN_SCALE = HEAD_DIM ** -0.5
LOG2E = math.log2(math.e)
Q_SCALE = ATTN_SCALE * LOG2E
LAM_INIT0 = 0.8 - 0.6 * math.exp(-0.3 * 0)
NEG = -1e30

LANES = 128
MOD_ROWS = 16
VMEM_LIMIT = 56 * 1024 * 1024

QA_PERM = (0, 4, 1, 5, 2, 6, 3, 7)

NAT_ROWS = 2
NAT_Q = NAT_ROWS * GRID_W
NAT_WIN_ROWS = 10
NAT_WIN = NAT_WIN_ROWS * GRID_W
NAT_TILES = NAT_WIN_ROWS // 2
CB_ENTRIES = 18
CB_OFFSET = 2


def _cparams(sem):
    return pltpu.CompilerParams(dimension_semantics=sem, vmem_limit_bytes=VMEM_LIMIT)


def _dot(a, b):
    return jnp.dot(a, b, preferred_element_type=F32)


def _dot_nt(a, b):
    return lax.dot_general(a, b, (((1,), (1,)), ((), ())), preferred_element_type=F32)


def _split3(a):
    hi = a.astype(BF16)
    r1 = a - hi.astype(F32)
    mid = r1.astype(BF16)
    lo = (r1 - mid.astype(F32)).astype(BF16)
    return hi, mid, lo


def _layer_norm(z, g, b):
    mu = jnp.mean(z, axis=-1, keepdims=True)
    zc = z - mu
    var = jnp.mean(zc * zc, axis=-1, keepdims=True)
    return zc * lax.rsqrt(var + EPS) * g + b


@functools.lru_cache(maxsize=None)
def _rope_tables():
    pos = np.arange(SEQ)
    row = (pos // GRID_W).astype(np.float64)
    col = (pos % GRID_W).astype(np.float64)
    half = HEAD_DIM // 2
    freqs = np.power(ROPE_THETA, -np.arange(0, half, 2, dtype=np.float64) / half)
    def axis_angles(p):
        a = p[:, None] * freqs[None, :]
        return np.concatenate([a, a], axis=-1)
    ang = np.concatenate([axis_angles(row), axis_angles(col)], axis=-1)
    cos, sin = np.cos(ang), np.sin(ang)
    first = (np.arange(HEAD_DIM) % half) < (half // 2)
    sin_a = np.where(first[None, :], -sin, 0.0)
    sin_b = np.where(first[None, :], 0.0, sin)
    tile = lambda t: np.tile(t, (1, LANES // HEAD_DIM)).astype(np.float32)
    return tile(cos), tile(sin_a), tile(sin_b)


@functools.lru_cache(maxsize=None)
def _group_mean_matrix(width):
    g = np.arange(width) // HEAD_DIM
    return (g[:, None] == g[None, :]).astype(np.float32) / HEAD_DIM


@functools.lru_cache(maxsize=None)
def _dft_tables():
    t = np.arange(SEQ, dtype=np.int64)
    k = (t[:, None] * t[None, :]) % SEQ
    ang = 2.0 * np.pi * k.astype(np.float64) / SEQ
    ct = np.cos(ang).astype(np.float32)
    sn = (-np.sin(ang)).astype(np.float32)
    c = np.arange(D_GROUP_DIM, dtype=np.int64)
    kc = (c[:, None] * c[None, :]) % D_GROUP_DIM
    angc = 2.0 * np.pi * kc.astype(np.float64) / D_GROUP_DIM
    norm = 1.0 / math.sqrt(SEQ * D_GROUP_DIM)
    return ct, sn, (np.cos(angc) * norm).astype(np.float32), (np.sin(angc) * norm).astype(np.float32)


def _mod_kernel(c_ref, w_ref, b_ref, o_ref):
    c = c_ref[...]
    a = c / (1.0 + jnp.exp(-c))
    a_hi, a_mid, _ = _split3(a)
    w = w_ref[0]
    w_hi, w_mid, _ = _split3(w)
    o_ref[0] = _dot(a_hi, w_hi) + (_dot(a_mid, w_hi) + _dot(a_hi, w_mid)) + b_ref[0]


def _modulation(cs, mod_w, mod_b):
    tn = 1536
    n = 6 * D_MODEL
    return pl.pallas_call(
        _mod_kernel,
        grid=(DEPTH, n // tn),
        in_specs=[pl.BlockSpec((MOD_ROWS, D_MODEL), lambda l, j: (0, 0)),
                  pl.BlockSpec((1, D_MODEL, tn), lambda l, j: (l, 0, j)),
                  pl.BlockSpec((1, 1, tn), lambda l, j: (l, 0, j))],
        out_specs=pl.BlockSpec((1, MOD_ROWS, tn), lambda l, j: (l, 0, j)),
        out_shape=jax.ShapeDtypeStruct((DEPTH, MOD_ROWS, n), F32),
        compiler_params=_cparams(("arbitrary", "arbitrary")),
        name="modulation",
    )(cs, mod_w, mod_b.reshape(DEPTH, 1, n))


def _tok_spec(tm, width):
    return pl.BlockSpec((1, tm, width), lambda g, i: (g, i, 0))


def _mod_spec(mod_row):
    if mod_row is None:
        return pl.BlockSpec((1, 6, D_MODEL), lambda g, i: (g, 0, 0))
    return pl.BlockSpec((1, 6, D_MODEL), lambda g, i: (mod_row, 0, 0))


def _const_spec(shape):
    nd = len(shape)
    return pl.BlockSpec(shape, lambda g, i: (0,) * nd)


AB_QA, AB_QB, AB_KA, AB_KB, AB_VA, AB_VB = 0, 512, 1024, 1152, 1664, 1792
AB_W = 2304


def _store_v_ext(o_ref, v):
    ones = jnp.ones((v.shape[0], LANES), o_ref.dtype)
    for j in range(v.shape[1] // LANES):
        o_ref[0, :, 2 * j * LANES:(2 * j + 1) * LANES] = v[:, j * LANES:(j + 1) * LANES].astype(o_ref.dtype)
        o_ref[0, :, (2 * j + 1) * LANES:(2 * j + 2) * LANES] = ones


def _rope_chunk(c, cos, sin_a, sin_b):
    return c * cos + pltpu.roll(c, LANES - 16, axis=1) * sin_a + pltpu.roll(c, 16, axis=1) * sin_b


def _inproj_ab_kernel(*refs, rope):
    if rope:
        (x_ref, mod_ref, w_ref, gq_ref, gk_ref, g512_ref, g128_ref, cos_ref, sa_ref, sb_ref,
         qa_ref, qb_ref, ka_ref, kb_ref, va_ref, vb_ref) = refs
        cos, sin_a, sin_b = cos_ref[...], sa_ref[...], sb_ref[...]
    else:
        (x_ref, mod_ref, w_ref, gq_ref, gk_ref, g512_ref, g128_ref,
         qa_ref, qb_ref, ka_ref, kb_ref, va_ref, vb_ref) = refs
    x = x_ref[0]
    u = (x * (1.0 + mod_ref[0, 1:2, :]) + mod_ref[0, 0:1, :]).astype(BF16)
    h = _dot(u, w_ref[...])

    def rms(v, g_ref, gm_ref):
        ms = _dot((v * v).astype(BF16), gm_ref[...])
        return v * lax.rsqrt(ms + EPS) * g_ref[...]

    def emit(v, o_ref, scale):
        for j in range(v.shape[1] // LANES):
            c = v[:, j * LANES:(j + 1) * LANES]
            if rope:
                c = _rope_chunk(c, cos, sin_a, sin_b)
            if scale != 1.0:
                c = c * scale
            o_ref[0, :, j * LANES:(j + 1) * LANES] = c.astype(o_ref.dtype)

    emit(rms(h[:, AB_QA:AB_QB], gq_ref, g512_ref), qa_ref, Q_SCALE)
    emit(h[:, AB_QB:AB_KA], qb_ref, Q_SCALE)
    emit(rms(h[:, AB_KA:AB_KB], gk_ref, g128_ref), ka_ref, 1.0)
    emit(h[:, AB_KB:AB_VA], kb_ref, 1.0)
    _store_v_ext(va_ref, h[:, AB_VA:AB_VB])
    _store_v_ext(vb_ref, h[:, AB_VB:AB_W])


def _inproj_ab(tok, mod, mod_row, w, gq, gk, rope_tabs, tm, name):
    g, r, _ = tok.shape
    rope = rope_tabs is not None
    consts = [w, gq, gk, jnp.asarray(_group_mean_matrix(512), BF16), jnp.asarray(_group_mean_matrix(128), BF16)]
    in_specs = [_tok_spec(tm, D_MODEL), _mod_spec(mod_row)] + [_const_spec(c.shape) for c in consts]
    args = [tok, mod] + consts
    if rope:
        in_specs += [pl.BlockSpec((tm, LANES), lambda gg, i: (i, 0))] * 3
        args += list(rope_tabs)
    widths = (512, 512, 128, 512, 2 * 128, 2 * 512)
    return pl.pallas_call(
        functools.partial(_inproj_ab_kernel, rope=rope),
        grid=(g, r // tm),
        in_specs=in_specs,
        out_specs=[_tok_spec(tm, wd) for wd in widths],
        out_shape=[jax.ShapeDtypeStruct((g, r, wd), BF16) for wd in widths],
        compiler_params=_cparams(("parallel", "parallel")),
        name=name,
    )(*args)


def _softmax_pv(q2, kv_pairs, bias=None):
    scores = [_dot_nt(q2, k) for k, _ in kv_pairs]
    if bias is not None:
        scores[0] = scores[0] + bias
    m = scores[0].max(axis=-1, keepdims=True)
    for s in scores[1:]:
        m = jnp.maximum(m, s.max(axis=-1, keepdims=True))
    acc = None
    for s, (_, v) in zip(scores, kv_pairs):
        a = _dot(jnp.exp2(s - m).astype(BF16), v)
        acc = a if acc is None else acc + a
    return acc[:, :LANES] / acc[:, LANES:]


def _split_pair(qp, lo):
    zero = jnp.zeros_like(qp)
    return jnp.concatenate([jnp.where(lo, qp, zero), jnp.where(lo, zero, qp)], axis=0)


ATTN_CHAIN_ROWS = 256


def _attn_ab_kernel(*refs, with_x):
    if with_x:
        (qa_ref, qb_ref, kax_ref, vax_ref, kbx_ref, vbx_ref, kac_ref, vac_ref, kbc_ref, vbc_ref,
         lam_ref, subln_ref, o_ref) = refs
    else:
        (qa_ref, qb_ref, kac_ref, vac_ref, kbc_ref, vbc_ref, lam_ref, subln_ref, o_ref) = refs
    tc = ATTN_CHAIN_ROWS
    lo = lax.broadcasted_iota(jnp.int32, (1, LANES), 1) < HEAD_DIM

    lp = lam_ref[...]
    lam = (jnp.exp(jnp.sum(lp[0:1] * lp[1:2], axis=-1, keepdims=True))
           - jnp.exp(jnp.sum(lp[2:3] * lp[3:4], axis=-1, keepdims=True)) + LAM_INIT0)

    for rb in range(qa_ref.shape[1] // tc):
        rows = slice(rb * tc, (rb + 1) * tc)
        for p in range(A_HEADS // 2):
            sl = slice(p * LANES, (p + 1) * LANES)
            q2 = _split_pair(qa_ref[0, rows, sl], lo)
            kv = [(kac_ref[0], vac_ref[0])]
            if with_x:
                kv.append((kax_ref[0], vax_ref[0]))
            o2 = _softmax_pv(q2, kv)
            o_ref[0, rows, sl] = jnp.where(lo, o2[:tc], o2[tc:]).astype(o_ref.dtype)

        for i in range(B_HEADS):
            sl = slice(i * LANES, (i + 1) * LANES)
            sl2 = slice(2 * i * LANES, (2 * i + 2) * LANES)
            q2 = _split_pair(qb_ref[0, rows, sl], lo)
            kv = [(kbc_ref[0, :, sl], vbc_ref[0, :, sl2])]
            if with_x:
                kv.append((kbx_ref[0, :, sl], vbx_ref[0, :, sl2]))
            o2 = _softmax_pv(q2, kv)
            o = o2[:tc] - lam * o2[tc:]
            ms = jnp.mean(o * o, axis=-1, keepdims=True)
            o = o * lax.rsqrt(ms + EPS) * subln_ref[...] * (1.0 - LAM_INIT0)
            o_ref[0, rows, 512 + i * LANES:512 + (i + 1) * LANES] = o.astype(o_ref.dtype)


def _attn_ab(q_parts, x_kv, c_kv, lam_p, subln, tq, name):
    qa, qb = q_parts
    b, nq, _ = qa.shape
    with_x = x_kv is not None

    def full(a):
        return pl.BlockSpec((1,) + a.shape[1:], lambda bb, i: (bb, 0, 0))

    args = [qa, qb]
    in_specs = [pl.BlockSpec((1, tq, 512), lambda bb, i: (bb, i, 0))] * 2
    if with_x:
        args += list(x_kv)
        in_specs += [full(a) for a in x_kv]
    args += list(c_kv) + [lam_p, subln]
    in_specs += [full(a) for a in c_kv] + [pl.BlockSpec(lam_p.shape, lambda bb, i: (0, 0)),
                                           pl.BlockSpec(subln.shape, lambda bb, i: (0, 0))]
    return pl.pallas_call(
        functools.partial(_attn_ab_kernel, with_x=with_x),
        grid=(b, nq // tq),
        in_specs=in_specs,
        out_specs=pl.BlockSpec((1, tq, D_MODEL), lambda bb, i: (bb, i, 0)),
        out_shape=jax.ShapeDtypeStruct((b, nq, D_MODEL), BF16),
        compiler_params=_cparams(("parallel", "parallel")),
        name=name,
    )(*args)


FFN_CHUNK = 1024


def _post_kernel(*refs, n_parts):
    o_refs = refs[:n_parts]
    wo_refs = refs[n_parts:2 * n_parts]
    x_ref, mod_ref, g_ref, b_ref, w1_ref, w2_ref, y_ref = refs[2 * n_parts:]
    y = _dot(o_refs[0][0], wo_refs[0][...])
    for o_ref, w_ref in zip(o_refs[1:], wo_refs[1:]):
        y = y + _dot(o_ref[0], w_ref[...])
    x1 = _layer_norm(ALPHA * x_ref[0] + mod_ref[0, 2:3, :] * y, g_ref[0:1, :], b_ref[0:1, :])
    u = (x1 * (1.0 + mod_ref[0, 4:5, :]) + mod_ref[0, 3:4, :]).astype(BF16)
    acc = None
    for c in range(FFN_DIM // FFN_CHUNK):
        sl = slice(c * FFN_CHUNK, (c + 1) * FFN_CHUNK)
        h = jnp.maximum(_dot(u, w1_ref[:, sl]), 0.0)
        a = _dot((h * h).astype(BF16), w2_ref[sl, :])
        acc = a if acc is None else acc + a
    y_ref[0] = _layer_norm(ALPHA * x1 + mod_ref[0, 5:6, :] * acc, g_ref[1:2, :], b_ref[1:2, :])


def _post(o_parts, wo_parts, resid, mod, mod_row, ln_g, ln_b, w1, w2, tm, name):
    g, r, _ = resid.shape
    n = len(o_parts)
    single = pl.Buffered(1)
    const1 = lambda a: pl.BlockSpec(a.shape, lambda gg, i: (0,) * a.ndim, pipeline_mode=single)
    in_specs = ([_tok_spec(tm, o.shape[2]) for o in o_parts] + [const1(w) for w in wo_parts]
                + [_tok_spec(tm, D_MODEL), _mod_spec(mod_row), _const_spec(ln_g.shape), _const_spec(ln_b.shape),
                   const1(w1), const1(w2)])
    return pl.pallas_call(
        functools.partial(_post_kernel, n_parts=n),
        grid=(g, r // tm),
        in_specs=in_specs,
        out_specs=_tok_spec(tm, D_MODEL),
        out_shape=jax.ShapeDtypeStruct((g, r, D_MODEL), F32),
        compiler_params=_cparams(("parallel", "parallel")),
        name=name,
    )(*o_parts, *wo_parts, resid, mod, ln_g, ln_b, w1, w2)


def _inproj_cd_kernel(x_ref, mod_ref, w_ref, *o_refs, with_q):
    x = x_ref[0]
    u = (x * (1.0 + mod_ref[0, 1:2, :]) + mod_ref[0, 0:1, :]).astype(BF16)
    h = _dot(u, w_ref[...])
    off = 0
    for k, o_ref in enumerate(o_refs[:-1]):
        v = h[:, off:off + 512]
        if with_q and k == 0:
            v = v * Q_SCALE
        o_ref[0] = v.astype(o_ref.dtype)
        off += 512
    _store_v_ext(o_refs[-1], h[:, off:off + 512])


def _inproj_cd(tok, mod, mod_row, w, with_q, tm, name):
    g, r, _ = tok.shape
    widths = (w.shape[1] // 512 - 1) * (512,) + (1024,)
    return pl.pallas_call(
        functools.partial(_inproj_cd_kernel, with_q=with_q),
        grid=(g, r // tm),
        in_specs=[_tok_spec(tm, D_MODEL), _mod_spec(mod_row), _const_spec(w.shape)],
        out_specs=[_tok_spec(tm, wd) for wd in widths],
        out_shape=[jax.ShapeDtypeStruct((g, r, wd), BF16) for wd in widths],
        compiler_params=_cparams(("parallel", "parallel")),
        name=name,
    )(tok, mod, w)


def _dft_weight_kernel(w_ref, cc_ref, sc_ref, wa_ref, wb_ref):
    c_hi, c_mid, _ = _split3(cc_ref[...])
    s_hi, s_mid, _ = _split3(sc_ref[...])
    for gi in range(D_GROUPS):
        sl = slice(gi * D_GROUP_DIM, (gi + 1) * D_GROUP_DIM)
        w_hi, w_mid, _ = _split3(w_ref[:, sl])
        wa_ref[:, sl] = (_dot(w_hi, c_hi) + (_dot(w_mid, c_hi) + _dot(w_hi, c_mid))).astype(BF16)
        wb_ref[:, sl] = (_dot(w_hi, s_hi) + (_dot(w_mid, s_hi) + _dot(w_hi, s_mid))).astype(BF16)


def _dft_weights(w_fd, cc, sc):
    return pl.pallas_call(
        _dft_weight_kernel,
        out_shape=[jax.ShapeDtypeStruct(w_fd.shape, BF16)] * 2,
        compiler_params=pltpu.CompilerParams(vmem_limit_bytes=VMEM_LIMIT),
        name="dft_weights",
    )(w_fd, cc, sc)


DFT_ROWS = 512


def _dft_time_kernel(a_ref, b_ref, ct_ref, sn_ref, o_ref):
    a = a_ref[0]
    bm = b_ref[0]
    for r in range(SEQ // DFT_ROWS):
        sl = slice(r * DFT_ROWS, (r + 1) * DFT_ROWS)
        y = _dot(ct_ref[sl, :], a) + _dot(sn_ref[sl, :], bm)
        o_ref[0, sl, :] = y.astype(o_ref.dtype)


def _dft_time(a, bm, ct, sn):
    b = a.shape[0]
    single = pl.Buffered(1)
    tok = pl.BlockSpec((1, SEQ, 512), lambda bb: (bb, 0, 0))
    tab = pl.BlockSpec((SEQ, SEQ), lambda bb: (0, 0), pipeline_mode=single)
    return pl.pallas_call(
        _dft_time_kernel,
        grid=(b,),
        in_specs=[tok, tok, tab, tab],
        out_specs=tok,
        out_shape=jax.ShapeDtypeStruct((b, SEQ, 512), BF16),
        compiler_params=_cparams(("parallel",)),
        name="dft_time",
    )(a, bm, ct, sn)


def _bias_table_kernel(r_ref, o_ref):
    n = GRID_W * LANES
    row = lax.broadcasted_iota(jnp.int32, (2 * 32, n), 0)
    lane = lax.broadcasted_iota(jnp.int32, (2 * 32, n), 1)
    e_row, i_row = row >> 5, row & 31
    c = lane >> 7
    e_lane = (lane >> 6) & 1
    kc = lane & (GRID_W - 1)
    sel = jnp.where((e_row == e_lane) & (kc - c + (WIN_W - 1) == i_row), 1.0, 0.0).astype(BF16)
    hi, mid, lo = _split3(r_ref[...])
    t = _dot(hi, sel) + _dot(mid, sel) + _dot(lo, sel)
    c1 = c[0:1]
    kc1 = kc[0:1]
    cstart = jnp.clip(c1 - WIN_W // 2, 0, GRID_W - WIN_W)
    col_ok = (kc1 >= cstart) & (kc1 < cstart + WIN_W)
    o_ref[...] = jnp.where(col_ok, t * LOG2E, NEG)


def _bias_table(rpb):
    p = jnp.pad(rpb, ((0, 0), (CB_OFFSET, CB_OFFSET), (0, 1)))
    r2 = jnp.stack([p[:, 0:CB_ENTRIES], p[:, 1:CB_ENTRIES + 1]], axis=2)
    r2 = r2.reshape(C_HEADS * CB_ENTRIES, 2 * 32)
    t = pl.pallas_call(
        _bias_table_kernel,
        out_shape=jax.ShapeDtypeStruct((C_HEADS * CB_ENTRIES, GRID_W * LANES), F32),
        compiler_params=pltpu.CompilerParams(vmem_limit_bytes=VMEM_LIMIT),
        name="natten_bias_table",
    )(r2)
    return t.reshape(C_HEADS, CB_ENTRIES, GRID_W, LANES)


NAT_BLOCKS = 4


def _natten_block(blk, q_ref, k_ref, v_ref, kc_ref, vc_ref, cb_ref, o_ref, lo, e):
    rows = slice(blk * NAT_Q, (blk + 1) * NAT_Q)
    r0 = NAT_ROWS * (NAT_BLOCKS * pl.program_id(1) + blk)
    ws = jnp.clip(r0 - WIN_H // 2, 0, GRID_H - NAT_WIN_ROWS)
    koff = pl.multiple_of(ws * GRID_W, LANES)

    entry = []
    rmask = []
    for qr in range(NAT_ROWS):
        r = r0 + qr
        lo_r = jnp.clip(r - WIN_H // 2, 0, GRID_H - WIN_H) - r + (WIN_H - 1)
        ent_q, mask_q = [], []
        for j in range(NAT_TILES):
            dr0 = ws + 2 * j - r + (WIN_H - 1)
            ent_q.append(dr0 + CB_OFFSET)
            dr = dr0 + e
            ok = (dr >= lo_r) & (dr < lo_r + WIN_H)
            mask_q.append(jnp.where(ok, 0.0, NEG))
        entry.append(ent_q)
        rmask.append(mask_q)

    for p in range(C_HEADS // 2):
        sl = slice(p * LANES, (p + 1) * LANES)
        sl2 = slice(2 * p * LANES, (2 * p + 2) * LANES)
        q2 = _split_pair(q_ref[0, rows, sl], lo)
        bias = jnp.concatenate(
            [jnp.concatenate([cb_ref[2 * p + hh, entry[qr][j]] + rmask[qr][j] for j in range(NAT_TILES)], axis=1)
             for hh in range(2) for qr in range(NAT_ROWS)], axis=0)
        kv = [(k_ref[0, pl.ds(koff, NAT_WIN), sl], v_ref[0, pl.ds(koff, NAT_WIN), sl2]),
              (kc_ref[0, :, sl], vc_ref[0, :, sl2])]
        o2 = _softmax_pv(q2, kv, bias)
        o_ref[0, rows, sl] = jnp.where(lo, o2[:NAT_Q], o2[NAT_Q:]).astype(o_ref.dtype)


def _natten_kernel(q_ref, k_ref, v_ref, kc_ref, vc_ref, cb_ref, o_ref):
    lane = lax.broadcasted_iota(jnp.int32, (1, LANES), 1)
    lo = lane < HEAD_DIM
    e = jnp.where(lo, 0, 1)
    for blk in range(NAT_BLOCKS):
        _natten_block(blk, q_ref, k_ref, v_ref, kc_ref, vc_ref, cb_ref, o_ref, lo, e)


def _natten(q, k, v, kc, vc, cb):
    b = q.shape[0]
    tq = NAT_BLOCKS * NAT_Q
    qspec = pl.BlockSpec((1, tq, 512), lambda bb, i: (bb, i, 0))
    full = lambda a: pl.BlockSpec((1,) + a.shape[1:], lambda bb, i: (bb, 0, 0))
    return pl.pallas_call(
        _natten_kernel,
        grid=(b, SEQ // tq),
        in_specs=[qspec, full(k), full(v), full(kc), full(vc),
                  pl.BlockSpec(cb.shape, lambda bb, i: (0, 0, 0, 0))],
        out_specs=qspec,
        out_shape=jax.ShapeDtypeStruct((b, SEQ, 512), BF16),
        compiler_params=_cparams(("parallel", "parallel")),
        name="natten",
    )(q, k, v, kc, vc, cb)


def kernel(x, c, ctx, c_ctx, mod_w, mod_b, ln_g, ln_b, ffn_w1, ffn_w2, ab_w_in, ab_w_out, a_q_norm, a_k_norm,
           b_lambda, b_subln, cd_w_in, cd_w_out, c_rpb):
    b = x.shape[0]
    nctx = b * CTX_LEN
    tm = 512
    tmc = min(tm, nctx)

    cs = jnp.concatenate([c, c_ctx[None, :], jnp.zeros((MOD_ROWS - b - 1, D_MODEL), F32)], axis=0)
    mods = _modulation(cs, mod_w, mod_b).reshape(DEPTH, MOD_ROWS, 6, D_MODEL)
    ctx_row = b

    w1 = ffn_w1.astype(BF16)
    w2 = ffn_w2.astype(BF16)
    cflat = ctx.reshape(1, nctx, D_MODEL)

    w_in = ab_w_in[0]
    perm = np.asarray(QA_PERM)
    w_qa = w_in[:, :512].reshape(D_MODEL, A_HEADS, HEAD_DIM)[:, perm].reshape(D_MODEL, 512)
    w_ab = jnp.concatenate([w_qa, w_in[:, 512:1024], w_in[:, 1024:1152], w_in[:, 1280:1792],
                            w_in[:, 1152:1280], w_in[:, 1792:2304]], axis=1).astype(BF16)
    w_out = ab_w_out[0]
    w_out_ab = jnp.concatenate([w_out[:512].reshape(A_HEADS, HEAD_DIM, D_MODEL)[perm].reshape(512, D_MODEL),
                                w_out[512:]], axis=0).astype(BF16)
    gq = jnp.tile(a_q_norm[0], A_HEADS)[None, :]
    gk = jnp.tile(a_k_norm[0], A_KV_HEADS)[None, :]
    rope_tabs = tuple(jnp.asarray(t) for t in _rope_tables())
    subln = b_subln[0][None, :]
    mod0 = mods[0]

    xq = _inproj_ab(x, mod0, None, w_ab, gq, gk, rope_tabs, tm, "inproj_ab_x")
    cq = _inproj_ab(cflat, mod0, ctx_row, w_ab, gq, gk, None, tmc, "inproj_ab_ctx")
    cq = [a.reshape(b, CTX_LEN, a.shape[2]) for a in cq]
    qa_x, qb_x, ka_x, kb_x, va_x, vb_x = xq
    qa_c, qb_c, ka_c, kb_c, va_c, vb_c = cq
    c_kv = (ka_c, va_c, kb_c, vb_c)
    o_x = _attn_ab((qa_x, qb_x), (ka_x, va_x, kb_x, vb_x), c_kv, b_lambda[0], subln, 512, "attn_ab_x")
    o_c = _attn_ab((qa_c, qb_c), None, c_kv, b_lambda[0], subln, CTX_LEN, "attn_ab_ctx")

    x2 = _post([o_x], [w_out_ab], x, mod0, None, ln_g[0], ln_b[0], w1[0], w2[0], tm, "post0_x")
    c2 = _post([o_c.reshape(1, nctx, D_MODEL)], [w_out_ab], cflat, mod0, ctx_row, ln_g[0], ln_b[0], w1[0], w2[0],
               tmc, "post0_ctx")

    mod1 = mods[1]
    w_in = cd_w_in[0]
    ct, sn, cc, sc = _dft_tables()
    w_fa, w_fb = _dft_weights(w_in[:, 512:1024], jnp.asarray(cc), jnp.asarray(sc))
    w_cd = jnp.concatenate([w_in[:, :512].astype(BF16), w_fa, w_fb, w_in[:, 1024:].astype(BF16)], axis=1)
    q_n, f_a, f_b, k_n, v_n = _inproj_cd(x2, mod1, None, w_cd, True, tm, "inproj_cd_x")
    kc_n, vc_n = _inproj_cd(c2, mod1, ctx_row, w_in[:, 1024:].astype(BF16), False, tmc, "inproj_cd_ctx")
    kc_n = kc_n.reshape(b, CTX_LEN, 512)
    vc_n = vc_n.reshape(b, CTX_LEN, 1024)

    o_d = _dft_time(f_a, f_b, jnp.asarray(ct).astype(BF16), jnp.asarray(sn).astype(BF16))
    cb = _bias_table(c_rpb[0])
    o_n = _natten(q_n, k_n, v_n, kc_n, vc_n, cb)

    w_out = cd_w_out[0].astype(BF16)
    return _post([o_n, o_d], [w_out[:512], w_out[512:]], x2, mod1, None, ln_g[1], ln_b[1], w1[1], w2[1], tm, "post1_x")
```

```python
import functools
import math

import numpy as np
import jax
import jax.numpy as jnp
from jax import lax
from jax.experimental import pallas as pl
from jax.experimental.pallas import tpu as pltpu

F32 = jnp.float32
BF16 = jnp.bfloat16

D_MODEL = 1024
SEQ = 2048
CTX_LEN = 256
DEPTH = 2
GRID_W = 64
GRID_H = SEQ // GRID_W
HEAD_DIM = 64
A_HEADS = 8
A_KV_HEADS = 2
B_HEADS = 4
C_HEADS = 8
WIN_H = 8
WIN_W = 16
D_GROUPS = 4
D_GROUP_DIM = 128
FFN_DIM = 4 * D_MODEL
ROPE_THETA = 10000.0
ALPHA = (2.0 * DEPTH) ** 0.25
EPS = 1e-6
ATTN_SCALE = HEAD_DIM ** -0.5
LOG2E = math.log2(math.e)
Q_SCALE = ATTN_SCALE * LOG2E
LAM_INIT0 = 0.8 - 0.6 * math.exp(-0.3 * 0)
NEG = -1e30

LANES = 128
MOD_ROWS = 16
VMEM_LIMIT = 56 * 1024 * 1024

QA_PERM = (0, 4, 1, 5, 2, 6, 3, 7)

NAT_ROWS = 2
NAT_Q = NAT_ROWS * GRID_W
NAT_WIN_ROWS = 10
NAT_WIN = NAT_WIN_ROWS * GRID_W
NAT_TILES = NAT_WIN_ROWS // 2
CB_ENTRIES = 18
CB_OFFSET = 2


def _cparams(sem):
    return pltpu.CompilerParams(dimension_semantics=sem, vmem_limit_bytes=VMEM_LIMIT)


def _dot(a, b):
    return jnp.dot(a, b, preferred_element_type=F32)


def _dot_nt(a, b):
    return lax.dot_general(a, b, (((1,), (1,)), ((), ())), preferred_element_type=F32)


def _split3(a):
    hi = a.astype(BF16)
    r1 = a - hi.astype(F32)
    mid = r1.astype(BF16)
    lo = (r1 - mid.astype(F32)).astype(BF16)
    return hi, mid, lo


def _layer_norm(z, g, b):
    mu = jnp.mean(z, axis=-1, keepdims=True)
    zc = z - mu
    var = jnp.mean(zc * zc, axis=-1, keepdims=True)
    return zc * lax.rsqrt(var + EPS) * g + b


@functools.lru_cache(maxsize=None)
def _rope_tables():
    pos = np.arange(SEQ)
    row = (pos // GRID_W).astype(np.float64)
    col = (pos % GRID_W).astype(np.float64)
    half = HEAD_DIM // 2
    freqs = np.power(ROPE_THETA, -np.arange(0, half, 2, dtype=np.float64) / half)
    def axis_angles(p):
        a = p[:, None] * freqs[None, :]
        return np.concatenate([a, a], axis=-1)
    ang = np.concatenate([axis_angles(row), axis_angles(col)], axis=-1)
    cos, sin = np.cos(ang), np.sin(ang)
    first = (np.arange(HEAD_DIM) % half) < (half // 2)
    sin_a = np.where(first[None, :], -sin, 0.0)
    sin_b = np.where(first[None, :], 0.0, sin)
    tile = lambda t: np.tile(t, (1, LANES // HEAD_DIM)).astype(np.float32)
    return tile(cos), tile(sin_a), tile(sin_b)


@functools.lru_cache(maxsize=None)
def _group_mean_matrix(width):
    g = np.arange(width) // HEAD_DIM
    return (g[:, None] == g[None, :]).astype(np.float32) / HEAD_DIM


@functools.lru_cache(maxsize=None)
def _dft_tables():
    t = np.arange(SEQ, dtype=np.int64)
    k = (t[:, None] * t[None, :]) % SEQ
    ang = 2.0 * np.pi * k.astype(np.float64) / SEQ
    ct = np.cos(ang).astype(np.float32)
    sn = (-np.sin(ang)).astype(np.float32)
    c = np.arange(D_GROUP_DIM, dtype=np.int64)
    kc = (c[:, None] * c[None, :]) % D_GROUP_DIM
    angc = 2.0 * np.pi * kc.astype(np.float64) / D_GROUP_DIM
    norm = 1.0 / math.sqrt(SEQ * D_GROUP_DIM)
    return ct, sn, (np.cos(angc) * norm).astype(np.float32), (np.sin(angc) * norm).astype(np.float32)


def _mod_kernel(c_ref, w_ref, b_ref, o_ref):
    c = c_ref[...]
    a = c / (1.0 + jnp.exp(-c))
    a_hi, a_mid, _ = _split3(a)
    w = w_ref[0]
    w_hi, w_mid, _ = _split3(w)
    o_ref[0] = _dot(a_hi, w_hi) + (_dot(a_mid, w_hi) + _dot(a_hi, w_mid)) + b_ref[0]


def _modulation(cs, mod_w, mod_b):
    tn = 1536
    n = 6 * D_MODEL
    return pl.pallas_call(
        _mod_kernel,
        grid=(DEPTH, n // tn),
        in_specs=[pl.BlockSpec((MOD_ROWS, D_MODEL), lambda l, j: (0, 0)),
                  pl.BlockSpec((1, D_MODEL, tn), lambda l, j: (l, 0, j)),
                  pl.BlockSpec((1, 1, tn), lambda l, j: (l, 0, j))],
        out_specs=pl.BlockSpec((1, MOD_ROWS, tn), lambda l, j: (l, 0, j)),
        out_shape=jax.ShapeDtypeStruct((DEPTH, MOD_ROWS, n), F32),
        compiler_params=_cparams(("arbitrary", "arbitrary")),
        name="modulation",
    )(cs, mod_w, mod_b.reshape(DEPTH, 1, n))


def _tok_spec(tm, width):
    return pl.BlockSpec((1, tm, width), lambda g, i: (g, i, 0))


def _mod_spec(mod_row):
    if mod_row is None:
        return pl.BlockSpec((1, 6, D_MODEL), lambda g, i: (g, 0, 0))
    return pl.BlockSpec((1, 6, D_MODEL), lambda g, i: (mod_row, 0, 0))


def _const_spec(shape):
    nd = len(shape)
    return pl.BlockSpec(shape, lambda g, i: (0,) * nd)


AB_QA, AB_QB, AB_KA, AB_KB, AB_VA, AB_VB = 0, 512, 1024, 1152, 1664, 1792
AB_W = 2304


TOKEN_CHAIN_ROWS = 256


def _store_v_ext(o_ref, rows, v):
    ones = jnp.ones((v.shape[0], LANES), o_ref.dtype)
    for j in range(v.shape[1] // LANES):
        o_ref[0, rows, 2 * j * LANES:(2 * j + 1) * LANES] = v[:, j * LANES:(j + 1) * LANES].astype(o_ref.dtype)
        o_ref[0, rows, (2 * j + 1) * LANES:(2 * j + 2) * LANES] = ones


def _rope_chunk(c, cos, sin_a, sin_b):
    return c * cos + pltpu.roll(c, LANES - 16, axis=1) * sin_a + pltpu.roll(c, 16, axis=1) * sin_b


def _inproj_ab_kernel(*refs, rope):
    if rope:
        (x_ref, mod_ref, w_ref, gq_ref, gk_ref, g512_ref, g128_ref, cos_ref, sa_ref, sb_ref,
         qa_ref, qb_ref, ka_ref, kb_ref, va_ref, vb_ref) = refs
    else:
        (x_ref, mod_ref, w_ref, gq_ref, gk_ref, g512_ref, g128_ref,
         qa_ref, qb_ref, ka_ref, kb_ref, va_ref, vb_ref) = refs

    def rms(v, g_ref, gm_ref):
        ms = _dot((v * v).astype(BF16), gm_ref[...])
        return v * lax.rsqrt(ms + EPS) * g_ref[...]

    tm = x_ref.shape[1]
    tc = min(TOKEN_CHAIN_ROWS, tm)
    for rb in range(tm // tc):
        rows = slice(rb * tc, (rb + 1) * tc)
        if rope:
            cos, sin_a, sin_b = cos_ref[rows, :], sa_ref[rows, :], sb_ref[rows, :]
        u = (x_ref[0, rows, :] * (1.0 + mod_ref[0, 1:2, :]) + mod_ref[0, 0:1, :]).astype(BF16)
        h = _dot(u, w_ref[...])

        def emit(v, o_ref, scale):
            for j in range(v.shape[1] // LANES):
                c = v[:, j * LANES:(j + 1) * LANES]
                if rope:
                    c = _rope_chunk(c, cos, sin_a, sin_b)
                if scale != 1.0:
                    c = c * scale
                o_ref[0, rows, j * LANES:(j + 1) * LANES] = c.astype(o_ref.dtype)

        emit(rms(h[:, AB_QA:AB_QB], gq_ref, g512_ref), qa_ref, Q_SCALE)
        emit(h[:, AB_QB:AB_KA], qb_ref, Q_SCALE)
        emit(rms(h[:, AB_KA:AB_KB], gk_ref, g128_ref), ka_ref, 1.0)
        emit(h[:, AB_KB:AB_VA], kb_ref, 1.0)
        _store_v_ext(va_ref, rows, h[:, AB_VA:AB_VB])
        vb_ref[0, rows, :] = h[:, AB_VB:AB_W].astype(BF16)


def _inproj_ab(tok, mod, mod_row, w, gq, gk, rope_tabs, tm, name):
    g, r, _ = tok.shape
    rope = rope_tabs is not None
    consts = [w, gq, gk, jnp.asarray(_group_mean_matrix(512), BF16), jnp.asarray(_group_mean_matrix(128), BF16)]
    in_specs = [_tok_spec(tm, D_MODEL), _mod_spec(mod_row)] + [_const_spec(c.shape) for c in consts]
    args = [tok, mod] + consts
    if rope:
        in_specs += [pl.BlockSpec((tm, LANES), lambda gg, i: (i, 0))] * 3
        args += list(rope_tabs)
    widths = (512, 512, 128, 512, 2 * 128, 512)
    return pl.pallas_call(
        functools.partial(_inproj_ab_kernel, rope=rope),
        grid=(g, r // tm),
        in_specs=in_specs,
        out_specs=[_tok_spec(tm, wd) for wd in widths],
        out_shape=[jax.ShapeDtypeStruct((g, r, wd), BF16) for wd in widths],
        compiler_params=_cparams(("parallel", "parallel")),
        name=name,
    )(*args)


def _softmax_pv(q2, kv_pairs, bias=None):
    scores = [_dot_nt(q2, k) for k, _ in kv_pairs]
    if bias is not None:
        scores[0] = scores[0] + bias
    m = scores[0].max(axis=-1, keepdims=True)
    for s in scores[1:]:
        m = jnp.maximum(m, s.max(axis=-1, keepdims=True))
    acc = None
    for s, (_, v) in zip(scores, kv_pairs):
        a = _dot(jnp.exp2(s - m).astype(BF16), v)
        acc = a if acc is None else acc + a
    return acc[:, :LANES] / acc[:, LANES:]


def _diff_softmax_pv(q2, kv_pairs, lam, tc):
    scores = [_dot_nt(q2, k) for k, _ in kv_pairs]
    m = scores[0].max(axis=-1, keepdims=True)
    for s in scores[1:]:
        m = jnp.maximum(m, s.max(axis=-1, keepdims=True))
    es = [jnp.exp2(s - m) for s in scores]
    den = es[0].sum(axis=-1, keepdims=True)
    for e in es[1:]:
        den = den + e.sum(axis=-1, keepdims=True)
    r = 1.0 / den
    r0 = r[:tc]
    r1 = lam * r[tc:]
    acc = None
    for e, (_, v) in zip(es, kv_pairs):
        a = _dot((e[:tc] * r0 - e[tc:] * r1).astype(BF16), v)
        acc = a if acc is None else acc + a
    return acc


def _split_pair(qp, lo):
    zero = jnp.zeros_like(qp)
    return jnp.concatenate([jnp.where(lo, qp, zero), jnp.where(lo, zero, qp)], axis=0)


ATTN_CHAIN_ROWS = 256


def _attn_ab_kernel(*refs, with_x):
    if with_x:
        (qa_ref, qb_ref, kax_ref, vax_ref, kbx_ref, vbx_ref, kac_ref, vac_ref, kbc_ref, vbc_ref,
         lam_ref, subln_ref, o_ref) = refs
    else:
        (qa_ref, qb_ref, kac_ref, vac_ref, kbc_ref, vbc_ref, lam_ref, subln_ref, o_ref) = refs
    tc = ATTN_CHAIN_ROWS
    lo = lax.broadcasted_iota(jnp.int32, (1, LANES), 1) < HEAD_DIM

    lp = lam_ref[...]
    lam = (jnp.exp(jnp.sum(lp[0:1] * lp[1:2], axis=-1, keepdims=True))
           - jnp.exp(jnp.sum(lp[2:3] * lp[3:4], axis=-1, keepdims=True)) + LAM_INIT0)

    for rb in range(qa_ref.shape[1] // tc):
        rows = slice(rb * tc, (rb + 1) * tc)
        for p in range(A_HEADS // 2):
            sl = slice(p * LANES, (p + 1) * LANES)
            q2 = _split_pair(qa_ref[0, rows, sl], lo)
            kv = [(kac_ref[0], vac_ref[0])]
            if with_x:
                kv.append((kax_ref[0], vax_ref[0]))
            o2 = _softmax_pv(q2, kv)
            o_ref[0, rows, sl] = jnp.where(lo, o2[:tc], o2[tc:]).astype(o_ref.dtype)

        for i in range(B_HEADS):
            sl = slice(i * LANES, (i + 1) * LANES)
            q2 = _split_pair(qb_ref[0, rows, sl], lo)
            kv = [(kbc_ref[0, :, sl], vbc_ref[0, :, sl])]
            if with_x:
                kv.append((kbx_ref[0, :, sl], vbx_ref[0, :, sl]))
            o = _diff_softmax_pv(q2, kv, lam, tc)
            ms = jnp.mean(o * o, axis=-1, keepdims=True)
            o = o * lax.rsqrt(ms + EPS) * subln_ref[...] * (1.0 - LAM_INIT0)
            o_ref[0, rows, 512 + i * LANES:512 + (i + 1) * LANES] = o.astype(o_ref.dtype)


def _attn_ab(q_parts, x_kv, c_kv, lam_p, subln, tq, name):
    qa, qb = q_parts
    b, nq, _ = qa.shape
    with_x = x_kv is not None

    def full(a):
        return pl.BlockSpec((1,) + a.shape[1:], lambda bb, i: (bb, 0, 0))

    args = [qa, qb]
    in_specs = [pl.BlockSpec((1, tq, 512), lambda bb, i: (bb, i, 0))] * 2
    if with_x:
        args += list(x_kv)
        in_specs += [full(a) for a in x_kv]
    args += list(c_kv) + [lam_p, subln]
    in_specs += [full(a) for a in c_kv] + [pl.BlockSpec(lam_p.shape, lambda bb, i: (0, 0)),
                                           pl.BlockSpec(subln.shape, lambda bb, i: (0, 0))]
    return pl.pallas_call(
        functools.partial(_attn_ab_kernel, with_x=with_x),
        grid=(b, nq // tq),
        in_specs=in_specs,
        out_specs=pl.BlockSpec((1, tq, D_MODEL), lambda bb, i: (bb, i, 0)),
        out_shape=jax.ShapeDtypeStruct((b, nq, D_MODEL), BF16),
        compiler_params=_cparams(("parallel", "parallel")),
        name=name,
    )(*args)


FFN_CHUNK = 1024


def _post_kernel(*refs, n_parts):
    o_refs = refs[:n_parts]
    wo_refs = refs[n_parts:2 * n_parts]
    x_ref, mod_ref, g_ref, b_ref, w1_ref, w2_ref, y_ref = refs[2 * n_parts:]
    tm = x_ref.shape[1]
    tc = min(TOKEN_CHAIN_ROWS, tm)
    for rb in range(tm // tc):
        rows = slice(rb * tc, (rb + 1) * tc)
        y = _dot(o_refs[0][0, rows, :], wo_refs[0][...])
        for o_ref, w_ref in zip(o_refs[1:], wo_refs[1:]):
            y = y + _dot(o_ref[0, rows, :], w_ref[...])
        x1 = _layer_norm(ALPHA * x_ref[0, rows, :] + mod_ref[0, 2:3, :] * y, g_ref[0:1, :], b_ref[0:1, :])
        u = (x1 * (1.0 + mod_ref[0, 4:5, :]) + mod_ref[0, 3:4, :]).astype(BF16)
        acc = None
        for c in range(FFN_DIM // FFN_CHUNK):
            sl = slice(c * FFN_CHUNK, (c + 1) * FFN_CHUNK)
            h = jnp.maximum(_dot(u, w1_ref[:, sl]), 0.0)
            a = _dot((h * h).astype(BF16), w2_ref[sl, :])
            acc = a if acc is None else acc + a
        y_ref[0, rows, :] = _layer_norm(ALPHA * x1 + mod_ref[0, 5:6, :] * acc, g_ref[1:2, :], b_ref[1:2, :])


def _post(o_parts, wo_parts, resid, mod, mod_row, ln_g, ln_b, w1, w2, tm, name):
    g, r, _ = resid.shape
    n = len(o_parts)
    single = pl.Buffered(1)
    const1 = lambda a: pl.BlockSpec(a.shape, lambda gg, i: (0,) * a.ndim, pipeline_mode=single)
    in_specs = ([_tok_spec(tm, o.shape[2]) for o in o_parts] + [const1(w) for w in wo_parts]
                + [_tok_spec(tm, D_MODEL), _mod_spec(mod_row), _const_spec(ln_g.shape), _const_spec(ln_b.shape),
                   const1(w1), const1(w2)])
    return pl.pallas_call(
        functools.partial(_post_kernel, n_parts=n),
        grid=(g, r // tm),
        in_specs=in_specs,
        out_specs=_tok_spec(tm, D_MODEL),
        out_shape=jax.ShapeDtypeStruct((g, r, D_MODEL), F32),
        compiler_params=_cparams(("parallel", "parallel")),
        name=name,
    )(*o_parts, *wo_parts, resid, mod, ln_g, ln_b, w1, w2)


def _inproj_cd_kernel(x_ref, mod_ref, w_ref, *o_refs, with_q):
    x = x_ref[0]
    u = (x * (1.0 + mod_ref[0, 1:2, :]) + mod_ref[0, 0:1, :]).astype(BF16)
    h = _dot(u, w_ref[...])
    off = 0
    for k, o_ref in enumerate(o_refs[:-1]):
        v = h[:, off:off + 512]
        if with_q and k == 0:
            v = v * Q_SCALE
        o_ref[0] = v.astype(o_ref.dtype)
        off += 512
    _store_v_ext(o_refs[-1], slice(None), h[:, off:off + 512])


def _inproj_cd(tok, mod, mod_row, w, with_q, tm, name):
    g, r, _ = tok.shape
    widths = (w.shape[1] // 512 - 1) * (512,) + (1024,)
    return pl.pallas_call(
        functools.partial(_inproj_cd_kernel, with_q=with_q),
        grid=(g, r // tm),
        in_specs=[_tok_spec(tm, D_MODEL), _mod_spec(mod_row), _const_spec(w.shape)],
        out_specs=[_tok_spec(tm, wd) for wd in widths],
        out_shape=[jax.ShapeDtypeStruct((g, r, wd), BF16) for wd in widths],
        compiler_params=_cparams(("parallel", "parallel")),
        name=name,
    )(tok, mod, w)


def _dft_weight_kernel(w_ref, cc_ref, sc_ref, wa_ref, wb_ref):
    c_hi, c_mid, _ = _split3(cc_ref[...])
    s_hi, s_mid, _ = _split3(sc_ref[...])
    for gi in range(D_GROUPS):
        sl = slice(gi * D_GROUP_DIM, (gi + 1) * D_GROUP_DIM)
        w_hi, w_mid, _ = _split3(w_ref[:, sl])
        wa_ref[:, sl] = (_dot(w_hi, c_hi) + (_dot(w_mid, c_hi) + _dot(w_hi, c_mid))).astype(BF16)
        wb_ref[:, sl] = (_dot(w_hi, s_hi) + (_dot(w_mid, s_hi) + _dot(w_hi, s_mid))).astype(BF16)


def _dft_weights(w_fd, cc, sc):
    return pl.pallas_call(
        _dft_weight_kernel,
        out_shape=[jax.ShapeDtypeStruct(w_fd.shape, BF16)] * 2,
        compiler_params=pltpu.CompilerParams(vmem_limit_bytes=VMEM_LIMIT),
        name="dft_weights",
    )(w_fd, cc, sc)


DFT_ROWS = 512


def _dft_time_kernel(a_ref, b_ref, ct_ref, sn_ref, o_ref):
    a = a_ref[0]
    bm = b_ref[0]
    for r in range(SEQ // DFT_ROWS):
        sl = slice(r * DFT_ROWS, (r + 1) * DFT_ROWS)
        y = _dot(ct_ref[sl, :], a) + _dot(sn_ref[sl, :], bm)
        o_ref[0, sl, :] = y.astype(o_ref.dtype)


def _dft_time(a, bm, ct, sn):
    b = a.shape[0]
    single = pl.Buffered(1)
    tok = pl.BlockSpec((1, SEQ, 512), lambda bb: (bb, 0, 0))
    tab = pl.BlockSpec((SEQ, SEQ), lambda bb: (0, 0), pipeline_mode=single)
    return pl.pallas_call(
        _dft_time_kernel,
        grid=(b,),
        in_specs=[tok, tok, tab, tab],
        out_specs=tok,
        out_shape=jax.ShapeDtypeStruct((b, SEQ, 512), BF16),
        compiler_params=_cparams(("parallel",)),
        name="dft_time",
    )(a, bm, ct, sn)


def _bias_table_kernel(r_ref, o_ref):
    n = GRID_W * LANES
    row = lax.broadcasted_iota(jnp.int32, (2 * 32, n), 0)
    lane = lax.broadcasted_iota(jnp.int32, (2 * 32, n), 1)
    e_row, i_row = row >> 5, row & 31
    c = lane >> 7
    e_lane = (lane >> 6) & 1
    kc = lane & (GRID_W - 1)
    sel = jnp.where((e_row == e_lane) & (kc - c + (WIN_W - 1) == i_row), 1.0, 0.0).astype(BF16)
    hi, mid, lo = _split3(r_ref[...])
    t = _dot(hi, sel) + _dot(mid, sel) + _dot(lo, sel)
    c1 = c[0:1]
    kc1 = kc[0:1]
    cstart = jnp.clip(c1 - WIN_W // 2, 0, GRID_W - WIN_W)
    col_ok = (kc1 >= cstart) & (kc1 < cstart + WIN_W)
    o_ref[...] = jnp.where(col_ok, t * LOG2E, NEG)


def _bias_table(rpb):
    p = jnp.pad(rpb, ((0, 0), (CB_OFFSET, CB_OFFSET), (0, 1)))
    r2 = jnp.stack([p[:, 0:CB_ENTRIES], p[:, 1:CB_ENTRIES + 1]], axis=2)
    r2 = r2.reshape(C_HEADS * CB_ENTRIES, 2 * 32)
    t = pl.pallas_call(
        _bias_table_kernel,
        out_shape=jax.ShapeDtypeStruct((C_HEADS * CB_ENTRIES, GRID_W * LANES), F32),
        compiler_params=pltpu.CompilerParams(vmem_limit_bytes=VMEM_LIMIT),
        name="natten_bias_table",
    )(r2)
    return t.reshape(C_HEADS, CB_ENTRIES, GRID_W, LANES)


NAT_BLOCKS = 4


def _natten_block(blk, q_ref, k_ref, v_ref, kc_ref, vc_ref, cb_ref, o_ref, lo, e):
    rows = slice(blk * NAT_Q, (blk + 1) * NAT_Q)
    r0 = NAT_ROWS * (NAT_BLOCKS * pl.program_id(1) + blk)
    ws = jnp.clip(r0 - WIN_H // 2, 0, GRID_H - NAT_WIN_ROWS)
    koff = pl.multiple_of(ws * GRID_W, LANES)

    entry = []
    rmask = []
    for qr in range(NAT_ROWS):
        r = r0 + qr
        lo_r = jnp.clip(r - WIN_H // 2, 0, GRID_H - WIN_H) - r + (WIN_H - 1)
        ent_q, mask_q = [], []
        for j in range(NAT_TILES):
            dr0 = ws + 2 * j - r + (WIN_H - 1)
            ent_q.append(dr0 + CB_OFFSET)
            dr = dr0 + e
            ok = (dr >= lo_r) & (dr < lo_r + WIN_H)
            mask_q.append(jnp.where(ok, 0.0, NEG))
        entry.append(ent_q)
        rmask.append(mask_q)

    for p in range(C_HEADS // 2):
        sl = slice(p * LANES, (p + 1) * LANES)
        sl2 = slice(2 * p * LANES, (2 * p + 2) * LANES)
        q2 = _split_pair(q_ref[0, rows, sl], lo)
        bias = jnp.concatenate(
            [jnp.concatenate([cb_ref[2 * p + hh, entry[qr][j]] + rmask[qr][j] for j in range(NAT_TILES)], axis=1)
             for hh in range(2) for qr in range(NAT_ROWS)], axis=0)
        kv = [(k_ref[0, pl.ds(koff, NAT_WIN), sl], v_ref[0, pl.ds(koff, NAT_WIN), sl2]),
              (kc_ref[0, :, sl], vc_ref[0, :, sl2])]
        o2 = _softmax_pv(q2, kv, bias)
        o_ref[0, rows, sl] = jnp.where(lo, o2[:NAT_Q], o2[NAT_Q:]).astype(o_ref.dtype)


def _natten_kernel(q_ref, k_ref, v_ref, kc_ref, vc_ref, cb_ref, o_ref):
    lane = lax.broadcasted_iota(jnp.int32, (1, LANES), 1)
    lo = lane < HEAD_DIM
    e = jnp.where(lo, 0, 1)
    for blk in range(NAT_BLOCKS):
        _natten_block(blk, q_ref, k_ref, v_ref, kc_ref, vc_ref, cb_ref, o_ref, lo, e)


def _natten(q, k, v, kc, vc, cb):
    b = q.shape[0]
    tq = NAT_BLOCKS * NAT_Q
    qspec = pl.BlockSpec((1, tq, 512), lambda bb, i: (bb, i, 0))
    full = lambda a: pl.BlockSpec((1,) + a.shape[1:], lambda bb, i: (bb, 0, 0))
    return pl.pallas_call(
        _natten_kernel,
        grid=(b, SEQ // tq),
        in_specs=[qspec, full(k), full(v), full(kc), full(vc),
                  pl.BlockSpec(cb.shape, lambda bb, i: (0, 0, 0, 0))],
        out_specs=qspec,
        out_shape=jax.ShapeDtypeStruct((b, SEQ, 512), BF16),
        compiler_params=_cparams(("parallel", "parallel")),
        name="natten",
    )(q, k, v, kc, vc, cb)


def kernel(x, c, ctx, c_ctx, mod_w, mod_b, ln_g, ln_b, ffn_w1, ffn_w2, ab_w_in, ab_w_out, a_q_norm, a_k_norm,
           b_lambda, b_subln, cd_w_in, cd_w_out, c_rpb):
    b = x.shape[0]
    nctx = b * CTX_LEN
    tm = 512
    tmc = min(tm, nctx)

    cs = jnp.concatenate([c, c_ctx[None, :], jnp.zeros((MOD_ROWS - b - 1, D_MODEL), F32)], axis=0)
    mods = _modulation(cs, mod_w, mod_b).reshape(DEPTH, MOD_ROWS, 6, D_MODEL)
    ctx_row = b

    w1 = ffn_w1.astype(BF16)
    w2 = ffn_w2.astype(BF16)
    cflat = ctx.reshape(1, nctx, D_MODEL)

    w_in = ab_w_in[0]
    perm = np.asarray(QA_PERM)
    w_qa = w_in[:, :512].reshape(D_MODEL, A_HEADS, HEAD_DIM)[:, perm].reshape(D_MODEL, 512)
    w_ab = jnp.concatenate([w_qa, w_in[:, 512:1024], w_in[:, 1024:1152], w_in[:, 1280:1792],
                            w_in[:, 1152:1280], w_in[:, 1792:2304]], axis=1).astype(BF16)
    w_out = ab_w_out[0]
    w_out_ab = jnp.concatenate([w_out[:512].reshape(A_HEADS, HEAD_DIM, D_MODEL)[perm].reshape(512, D_MODEL),
                                w_out[512:]], axis=0).astype(BF16)
    gq = jnp.tile(a_q_norm[0], A_HEADS)[None, :]
    gk = jnp.tile(a_k_norm[0], A_KV_HEADS)[None, :]
    rope_tabs = tuple(jnp.asarray(t) for t in _rope_tables())
    subln = b_subln[0][None, :]
    mod0 = mods[0]

    xq = _inproj_ab(x, mod0, None, w_ab, gq, gk, rope_tabs, tm, "inproj_ab_x")
    cq = _inproj_ab(cflat, mod0, ctx_row, w_ab, gq, gk, None, tmc, "inproj_ab_ctx")
    cq = [a.reshape(b, CTX_LEN, a.shape[2]) for a in cq]
    qa_x, qb_x, ka_x, kb_x, va_x, vb_x = xq
    qa_c, qb_c, ka_c, kb_c, va_c, vb_c = cq
    c_kv = (ka_c, va_c, kb_c, vb_c)
    o_x = _attn_ab((qa_x, qb_x), (ka_x, va_x, kb_x, vb_x), c_kv, b_lambda[0], subln, 512, "attn_ab_x")
    o_c = _attn_ab((qa_c, qb_c), None, c_kv, b_lambda[0], subln, CTX_LEN, "attn_ab_ctx")

    x2 = _post([o_x], [w_out_ab], x, mod0, None, ln_g[0], ln_b[0], w1[0], w2[0], tm, "post0_x")
    c2 = _post([o_c.reshape(1, nctx, D_MODEL)], [w_out_ab], cflat, mod0, ctx_row, ln_g[0], ln_b[0], w1[0], w2[0],
               tmc, "post0_ctx")

    mod1 = mods[1]
    w_in = cd_w_in[0]
    ct, sn, cc, sc = _dft_tables()
    w_fa, w_fb = _dft_weights(w_in[:, 512:1024], jnp.asarray(cc), jnp.asarray(sc))
    w_cd = jnp.concatenate([w_in[:, :512].astype(BF16), w_fa, w_fb, w_in[:, 1024:].astype(BF16)], axis=1)
    q_n, f_a, f_b, k_n, v_n = _inproj_cd(x2, mod1, None, w_cd, True, tm, "inproj_cd_x")
    kc_n, vc_n = _inproj_cd(c2, mod1, ctx_row, w_in[:, 1024:].astype(BF16), False, tmc, "inproj_cd_ctx")
    kc_n = kc_n.reshape(b, CTX_LEN, 512)
    vc_n = vc_n.reshape(b, CTX_LEN, 1024)

    o_d = _dft_time(f_a, f_b, jnp.asarray(ct).astype(BF16), jnp.asarray(sn).astype(BF16))
    cb = _bias_table(c_rpb[0])
    o_n = _natten(q_n, k_n, v_n, kc_n, vc_n, cb)

    w_out = cd_w_out[0].astype(BF16)
    return _post([o_n, o_d], [w_out[:512], w_out[512:]], x2, mod1, None, ln_g[1], ln_b[1], w1[1], w2[1], tm, "post1_x")
```

```python
import functools
import math

import numpy as np
import jax
import jax.numpy as jnp
from jax import lax
from jax.experimental import pallas as pl
from jax.experimental.pallas import tpu as pltpu

F32 = jnp.float32
BF16 = jnp.bfloat16

D_MODEL = 1024
SEQ = 2048
CTX_LEN = 256
DEPTH = 2
GRID_W = 64
GRID_H = SEQ // GRID_W
HEAD_DIM = 64
A_HEADS = 8
A_KV_HEADS = 2
B_HEADS = 4
C_HEADS = 8
WIN_H = 8
WIN_W = 16
D_GROUPS = 4
D_GROUP_DIM = 128
FFN_DIM = 4 * D_MODEL
ROPE_THETA = 10000.0
ALPHA = (2.0 * DEPTH) ** 0.25
EPS = 1e-6
ATTN_SCALE = HEAD_DIM ** -0.5
LOG2E = math.log2(math.e)
Q_SCALE = ATTN_SCALE * LOG2E
LAM_INIT0 = 0.8 - 0.6 * math.exp(-0.3 * 0)
NEG = -1e30

LANES = 128
MOD_ROWS = 16
VMEM_LIMIT = 56 * 1024 * 1024

QA_PERM = (0, 4, 1, 5, 2, 6, 3, 7)

NAT_ROWS = 4
NAT_Q = NAT_ROWS * GRID_W
NAT_WIN_ROWS = NAT_ROWS + WIN_H
NAT_WIN = NAT_WIN_ROWS * GRID_W
NAT_TILES = NAT_WIN_ROWS // 2
_NAT_DR0 = [min(max(r0 - WIN_H // 2, 0), GRID_H - NAT_WIN_ROWS) + 2 * j - (r0 + qr) + WIN_H - 1
            for r0 in range(0, GRID_H, NAT_ROWS) for qr in range(NAT_ROWS) for j in range(NAT_TILES)]
CB_OFFSET = -min(_NAT_DR0)
CB_ENTRIES = max(_NAT_DR0) + CB_OFFSET + 1


def _cparams(sem):
    return pltpu.CompilerParams(dimension_semantics=sem, vmem_limit_bytes=VMEM_LIMIT)


def _dot(a, b):
    return jnp.dot(a, b, preferred_element_type=F32)


def _dot_nt(a, b):
    return lax.dot_general(a, b, (((1,), (1,)), ((), ())), preferred_element_type=F32)


def _split3(a):
    hi = a.astype(BF16)
    r1 = a - hi.astype(F32)
    mid = r1.astype(BF16)
    lo = (r1 - mid.astype(F32)).astype(BF16)
    return hi, mid, lo


def _layer_norm(z, g, b):
    mu = jnp.mean(z, axis=-1, keepdims=True)
    zc = z - mu
    var = jnp.mean(zc * zc, axis=-1, keepdims=True)
    return zc * lax.rsqrt(var + EPS) * g + b


@functools.lru_cache(maxsize=None)
def _rope_tables():
    pos = np.arange(SEQ)
    row = (pos // GRID_W).astype(np.float64)
    col = (pos % GRID_W).astype(np.float64)
    half = HEAD_DIM // 2
    freqs = np.power(ROPE_THETA, -np.arange(0, half, 2, dtype=np.float64) / half)
    def axis_angles(p):
        a = p[:, None] * freqs[None, :]
        return np.concatenate([a, a], axis=-1)
    ang = np.concatenate([axis_angles(row), axis_angles(col)], axis=-1)
    cos, sin = np.cos(ang), np.sin(ang)
    first = (np.arange(HEAD_DIM) % half) < (half // 2)
    sin_a = np.where(first[None, :], -sin, 0.0)
    sin_b = np.where(first[None, :], 0.0, sin)
    tile = lambda t: np.tile(t, (1, LANES // HEAD_DIM)).astype(np.float32)
    return tile(cos), tile(sin_a), tile(sin_b)


@functools.lru_cache(maxsize=None)
def _group_mean_matrix(width):
    g = np.arange(width) // HEAD_DIM
    return (g[:, None] == g[None, :]).astype(np.float32) / HEAD_DIM


@functools.lru_cache(maxsize=None)
def _dft_tables():
    t = np.arange(SEQ, dtype=np.int64)
    k = (t[:, None] * t[None, :]) % SEQ
    ang = 2.0 * np.pi * k.astype(np.float64) / SEQ
    ct = np.cos(ang).astype(np.float32)
    sn = (-np.sin(ang)).astype(np.float32)
    c = np.arange(D_GROUP_DIM, dtype=np.int64)
    kc = (c[:, None] * c[None, :]) % D_GROUP_DIM
    angc = 2.0 * np.pi * kc.astype(np.float64) / D_GROUP_DIM
    norm = 1.0 / math.sqrt(SEQ * D_GROUP_DIM)
    return ct, sn, (np.cos(angc) * norm).astype(np.float32), (np.sin(angc) * norm).astype(np.float32)


def _mod_kernel(c_ref, w_ref, b_ref, o_ref):
    c = c_ref[...]
    a = c / (1.0 + jnp.exp(-c))
    a_hi, a_mid, _ = _split3(a)
    w = w_ref[0]
    w_hi, w_mid, _ = _split3(w)
    o_ref[0] = _dot(a_hi, w_hi) + (_dot(a_mid, w_hi) + _dot(a_hi, w_mid)) + b_ref[0]


def _modulation(cs, mod_w, mod_b):
    tn = 1536
    n = 6 * D_MODEL
    return pl.pallas_call(
        _mod_kernel,
        grid=(DEPTH, n // tn),
        in_specs=[pl.BlockSpec((MOD_ROWS, D_MODEL), lambda l, j: (0, 0)),
                  pl.BlockSpec((1, D_MODEL, tn), lambda l, j: (l, 0, j)),
                  pl.BlockSpec((1, 1, tn), lambda l, j: (l, 0, j))],
        out_specs=pl.BlockSpec((1, MOD_ROWS, tn), lambda l, j: (l, 0, j)),
        out_shape=jax.ShapeDtypeStruct((DEPTH, MOD_ROWS, n), F32),
        compiler_params=_cparams(("arbitrary", "arbitrary")),
        name="modulation",
    )(cs, mod_w, mod_b.reshape(DEPTH, 1, n))


def _tok_spec(tm, width):
    return pl.BlockSpec((1, tm, width), lambda g, i: (g, i, 0))


def _mod_spec(mod_row):
    if mod_row is None:
        return pl.BlockSpec((1, 6, D_MODEL), lambda g, i: (g, 0, 0))
    return pl.BlockSpec((1, 6, D_MODEL), lambda g, i: (mod_row, 0, 0))


def _const_spec(shape):
    nd = len(shape)
    return pl.BlockSpec(shape, lambda g, i: (0,) * nd)


AB_QA, AB_QB, AB_KA, AB_KB, AB_VA, AB_VB = 0, 512, 1024, 1152, 1664, 1792
AB_W = 2304


TOKEN_CHAIN_ROWS = 256


def _store_v_ext(o_ref, rows, v):
    ones = jnp.ones((v.shape[0], LANES), o_ref.dtype)
    for j in range(v.shape[1] // LANES):
        o_ref[0, rows, 2 * j * LANES:(2 * j + 1) * LANES] = v[:, j * LANES:(j + 1) * LANES].astype(o_ref.dtype)
        o_ref[0, rows, (2 * j + 1) * LANES:(2 * j + 2) * LANES] = ones


def _rope_chunk(c, cos, sin_a, sin_b):
    return c * cos + pltpu.roll(c, LANES - 16, axis=1) * sin_a + pltpu.roll(c, 16, axis=1) * sin_b


def _inproj_ab_kernel(*refs, rope):
    if rope:
        (x_ref, mod_ref, w_ref, gq_ref, gk_ref, g512_ref, g128_ref, cos_ref, sa_ref, sb_ref,
         qa_ref, qb_ref, ka_ref, kb_ref, va_ref, vb_ref) = refs
    else:
        (x_ref, mod_ref, w_ref, gq_ref, gk_ref, g512_ref, g128_ref,
         qa_ref, qb_ref, ka_ref, kb_ref, va_ref, vb_ref) = refs

    def rms(v, g_ref, gm_ref):
        ms = _dot((v * v).astype(BF16), gm_ref[...])
        return v * lax.rsqrt(ms + EPS) * g_ref[...]

    tm = x_ref.shape[1]
    tc = min(TOKEN_CHAIN_ROWS, tm)
    for rb in range(tm // tc):
        rows = slice(rb * tc, (rb + 1) * tc)
        if rope:
            cos, sin_a, sin_b = cos_ref[rows, :], sa_ref[rows, :], sb_ref[rows, :]
        u = (x_ref[0, rows, :] * (1.0 + mod_ref[0, 1:2, :]) + mod_ref[0, 0:1, :]).astype(BF16)
        h = _dot(u, w_ref[...])

        def emit(v, o_ref, scale):
            for j in range(v.shape[1] // LANES):
                c = v[:, j * LANES:(j + 1) * LANES]
                if rope:
                    c = _rope_chunk(c, cos, sin_a, sin_b)
                if scale != 1.0:
                    c = c * scale
                o_ref[0, rows, j * LANES:(j + 1) * LANES] = c.astype(o_ref.dtype)

        emit(rms(h[:, AB_QA:AB_QB], gq_ref, g512_ref), qa_ref, Q_SCALE)
        emit(h[:, AB_QB:AB_KA], qb_ref, Q_SCALE)
        emit(rms(h[:, AB_KA:AB_KB], gk_ref, g128_ref), ka_ref, 1.0)
        emit(h[:, AB_KB:AB_VA], kb_ref, 1.0)
        _store_v_ext(va_ref, rows, h[:, AB_VA:AB_VB])
        _store_v_ext(vb_ref, rows, h[:, AB_VB:AB_W])


def _inproj_ab(tok, mod, mod_row, w, gq, gk, rope_tabs, tm, name):
    g, r, _ = tok.shape
    rope = rope_tabs is not None
    consts = [w, gq, gk, jnp.asarray(_group_mean_matrix(512), BF16), jnp.asarray(_group_mean_matrix(128), BF16)]
    in_specs = [_tok_spec(tm, D_MODEL), _mod_spec(mod_row)] + [_const_spec(c.shape) for c in consts]
    args = [tok, mod] + consts
    if rope:
        in_specs += [pl.BlockSpec((tm, LANES), lambda gg, i: (i, 0))] * 3
        args += list(rope_tabs)
    widths = (512, 512, 128, 512, 2 * 128, 2 * 512)
    return pl.pallas_call(
        functools.partial(_inproj_ab_kernel, rope=rope),
        grid=(g, r // tm),
        in_specs=in_specs,
        out_specs=[_tok_spec(tm, wd) for wd in widths],
        out_shape=[jax.ShapeDtypeStruct((g, r, wd), BF16) for wd in widths],
        compiler_params=_cparams(("parallel", "parallel")),
        name=name,
    )(*args)


def _softmax_pv(q2, kv_pairs, bias=None):
    scores = [_dot_nt(q2, k) for k, _ in kv_pairs]
    if bias is not None:
        scores[0] = scores[0] + bias
    m = scores[0].max(axis=-1, keepdims=True)
    for s in scores[1:]:
        m = jnp.maximum(m, s.max(axis=-1, keepdims=True))
    acc = None
    for s, (_, v) in zip(scores, kv_pairs):
        a = _dot(jnp.exp2(s - m).astype(BF16), v)
        acc = a if acc is None else acc + a
    return acc[:, :LANES] / acc[:, LANES:]


def _split_pair(qp, lo):
    zero = jnp.zeros_like(qp)
    return jnp.concatenate([jnp.where(lo, qp, zero), jnp.where(lo, zero, qp)], axis=0)


ATTN_CHAIN_ROWS = 256


def _attn_ab_kernel(*refs, with_x):
    if with_x:
        (qa_ref, qb_ref, kax_ref, vax_ref, kbx_ref, vbx_ref, kac_ref, vac_ref, kbc_ref, vbc_ref,
         lam_ref, subln_ref, o_ref) = refs
    else:
        (qa_ref, qb_ref, kac_ref, vac_ref, kbc_ref, vbc_ref, lam_ref, subln_ref, o_ref) = refs
    tc = ATTN_CHAIN_ROWS
    lo = lax.broadcasted_iota(jnp.int32, (1, LANES), 1) < HEAD_DIM

    lp = lam_ref[...]
    lam = (jnp.exp(jnp.sum(lp[0:1] * lp[1:2], axis=-1, keepdims=True))
           - jnp.exp(jnp.sum(lp[2:3] * lp[3:4], axis=-1, keepdims=True)) + LAM_INIT0)

    for rb in range(qa_ref.shape[1] // tc):
        rows = slice(rb * tc, (rb + 1) * tc)
        for p in range(A_HEADS // 2):
            sl = slice(p * LANES, (p + 1) * LANES)
            q2 = _split_pair(qa_ref[0, rows, sl], lo)
            kv = [(kac_ref[0], vac_ref[0])]
            if with_x:
                kv.append((kax_ref[0], vax_ref[0]))
            o2 = _softmax_pv(q2, kv)
            o_ref[0, rows, sl] = jnp.where(lo, o2[:tc], o2[tc:]).astype(o_ref.dtype)

        for i in range(B_HEADS):
            sl = slice(i * LANES, (i + 1) * LANES)
            sl2 = slice(2 * i * LANES, (2 * i + 2) * LANES)
            q2 = _split_pair(qb_ref[0, rows, sl], lo)
            kv = [(kbc_ref[0, :, sl], vbc_ref[0, :, sl2])]
            if with_x:
                kv.append((kbx_ref[0, :, sl], vbx_ref[0, :, sl2]))
            o2 = _softmax_pv(q2, kv)
            o = o2[:tc] - lam * o2[tc:]
            ms = jnp.mean(o * o, axis=-1, keepdims=True)
            o = o * lax.rsqrt(ms + EPS) * subln_ref[...] * (1.0 - LAM_INIT0)
            o_ref[0, rows, 512 + i * LANES:512 + (i + 1) * LANES] = o.astype(o_ref.dtype)


def _attn_ab(q_parts, x_kv, c_kv, lam_p, subln, tq, name):
    qa, qb = q_parts
    b, nq, _ = qa.shape
    with_x = x_kv is not None

    def full(a):
        return pl.BlockSpec((1,) + a.shape[1:], lambda bb, i: (bb, 0, 0))

    args = [qa, qb]
    in_specs = [pl.BlockSpec((1, tq, 512), lambda bb, i: (bb, i, 0))] * 2
    if with_x:
        args += list(x_kv)
        in_specs += [full(a) for a in x_kv]
    args += list(c_kv) + [lam_p, subln]
    in_specs += [full(a) for a in c_kv] + [pl.BlockSpec(lam_p.shape, lambda bb, i: (0, 0)),
                                           pl.BlockSpec(subln.shape, lambda bb, i: (0, 0))]
    return pl.pallas_call(
        functools.partial(_attn_ab_kernel, with_x=with_x),
        grid=(b, nq // tq),
        in_specs=in_specs,
        out_specs=pl.BlockSpec((1, tq, D_MODEL), lambda bb, i: (bb, i, 0)),
        out_shape=jax.ShapeDtypeStruct((b, nq, D_MODEL), BF16),
        compiler_params=_cparams(("parallel", "parallel")),
        name=name,
    )(*args)


FFN_CHUNK = 1024


def _post_kernel(*refs, n_parts):
    o_refs = refs[:n_parts]
    wo_refs = refs[n_parts:2 * n_parts]
    x_ref, mod_ref, g_ref, b_ref, w1_ref, w2_ref, y_ref = refs[2 * n_parts:]
    y = _dot(o_refs[0][0], wo_refs[0][...])
    for o_ref, w_ref in zip(o_refs[1:], wo_refs[1:]):
        y = y + _dot(o_ref[0], w_ref[...])
    x1 = _layer_norm(ALPHA * x_ref[0] + mod_ref[0, 2:3, :] * y, g_ref[0:1, :], b_ref[0:1, :])
    u = (x1 * (1.0 + mod_ref[0, 4:5, :]) + mod_ref[0, 3:4, :]).astype(BF16)
    acc = None
    for c in range(FFN_DIM // FFN_CHUNK):
        sl = slice(c * FFN_CHUNK, (c + 1) * FFN_CHUNK)
        h = jnp.maximum(_dot(u, w1_ref[:, sl]), 0.0)
        a = _dot((h * h).astype(BF16), w2_ref[sl, :])
        acc = a if acc is None else acc + a
    y_ref[0] = _layer_norm(ALPHA * x1 + mod_ref[0, 5:6, :] * acc, g_ref[1:2, :], b_ref[1:2, :])


def _post(o_parts, wo_parts, resid, mod, mod_row, ln_g, ln_b, w1, w2, tm, name):
    g, r, _ = resid.shape
    n = len(o_parts)
    single = pl.Buffered(1)
    const1 = lambda a: pl.BlockSpec(a.shape, lambda gg, i: (0,) * a.ndim, pipeline_mode=single)
    in_specs = ([_tok_spec(tm, o.shape[2]) for o in o_parts] + [const1(w) for w in wo_parts]
                + [_tok_spec(tm, D_MODEL), _mod_spec(mod_row), _const_spec(ln_g.shape), _const_spec(ln_b.shape),
                   const1(w1), const1(w2)])
    return pl.pallas_call(
        functools.partial(_post_kernel, n_parts=n),
        grid=(g, r // tm),
        in_specs=in_specs,
        out_specs=_tok_spec(tm, D_MODEL),
        out_shape=jax.ShapeDtypeStruct((g, r, D_MODEL), F32),
        compiler_params=_cparams(("parallel", "parallel")),
        name=name,
    )(*o_parts, *wo_parts, resid, mod, ln_g, ln_b, w1, w2)


def _inproj_cd_kernel(x_ref, mod_ref, w_ref, *o_refs, with_q):
    x = x_ref[0]
    u = (x * (1.0 + mod_ref[0, 1:2, :]) + mod_ref[0, 0:1, :]).astype(BF16)
    h = _dot(u, w_ref[...])
    off = 0
    for k, o_ref in enumerate(o_refs[:-1]):
        v = h[:, off:off + 512]
        if with_q and k == 0:
            v = v * Q_SCALE
        o_ref[0] = v.astype(o_ref.dtype)
        off += 512
    _store_v_ext(o_refs[-1], slice(None), h[:, off:off + 512])


def _inproj_cd(tok, mod, mod_row, w, with_q, tm, name):
    g, r, _ = tok.shape
    widths = (w.shape[1] // 512 - 1) * (512,) + (1024,)
    return pl.pallas_call(
        functools.partial(_inproj_cd_kernel, with_q=with_q),
        grid=(g, r // tm),
        in_specs=[_tok_spec(tm, D_MODEL), _mod_spec(mod_row), _const_spec(w.shape)],
        out_specs=[_tok_spec(tm, wd) for wd in widths],
        out_shape=[jax.ShapeDtypeStruct((g, r, wd), BF16) for wd in widths],
        compiler_params=_cparams(("parallel", "parallel")),
        name=name,
    )(tok, mod, w)


def _dft_weight_kernel(w_ref, cc_ref, sc_ref, wa_ref, wb_ref):
    c_hi, c_mid, _ = _split3(cc_ref[...])
    s_hi, s_mid, _ = _split3(sc_ref[...])
    for gi in range(D_GROUPS):
        sl = slice(gi * D_GROUP_DIM, (gi + 1) * D_GROUP_DIM)
        w_hi, w_mid, _ = _split3(w_ref[:, sl])
        wa_ref[:, sl] = (_dot(w_hi, c_hi) + (_dot(w_mid, c_hi) + _dot(w_hi, c_mid))).astype(BF16)
        wb_ref[:, sl] = (_dot(w_hi, s_hi) + (_dot(w_mid, s_hi) + _dot(w_hi, s_mid))).astype(BF16)


def _dft_weights(w_fd, cc, sc):
    return pl.pallas_call(
        _dft_weight_kernel,
        out_shape=[jax.ShapeDtypeStruct(w_fd.shape, BF16)] * 2,
        compiler_params=pltpu.CompilerParams(vmem_limit_bytes=VMEM_LIMIT),
        name="dft_weights",
    )(w_fd, cc, sc)


DFT_ROWS = 512


def _dft_time_kernel(a_ref, b_ref, ct_ref, sn_ref, o_ref):
    a = a_ref[0]
    bm = b_ref[0]
    for r in range(SEQ // DFT_ROWS):
        sl = slice(r * DFT_ROWS, (r + 1) * DFT_ROWS)
        y = _dot(ct_ref[sl, :], a) + _dot(sn_ref[sl, :], bm)
        o_ref[0, sl, :] = y.astype(o_ref.dtype)


def _dft_time(a, bm, ct, sn):
    b = a.shape[0]
    single = pl.Buffered(1)
    tok = pl.BlockSpec((1, SEQ, 512), lambda bb: (bb, 0, 0))
    tab = pl.BlockSpec((SEQ, SEQ), lambda bb: (0, 0), pipeline_mode=single)
    return pl.pallas_call(
        _dft_time_kernel,
        grid=(b,),
        in_specs=[tok, tok, tab, tab],
        out_specs=tok,
        out_shape=jax.ShapeDtypeStruct((b, SEQ, 512), BF16),
        compiler_params=_cparams(("parallel",)),
        name="dft_time",
    )(a, bm, ct, sn)


def _bias_table_kernel(r_ref, o_ref):
    n = GRID_W * LANES
    row = lax.broadcasted_iota(jnp.int32, (2 * 32, n), 0)
    lane = lax.broadcasted_iota(jnp.int32, (2 * 32, n), 1)
    e_row, i_row = row >> 5, row & 31
    c = lane >> 7
    e_lane = (lane >> 6) & 1
    kc = lane & (GRID_W - 1)
    sel = jnp.where((e_row == e_lane) & (kc - c + (WIN_W - 1) == i_row), 1.0, 0.0).astype(BF16)
    hi, mid, lo = _split3(r_ref[...])
    t = _dot(hi, sel) + _dot(mid, sel) + _dot(lo, sel)
    c1 = c[0:1]
    kc1 = kc[0:1]
    cstart = jnp.clip(c1 - WIN_W // 2, 0, GRID_W - WIN_W)
    col_ok = (kc1 >= cstart) & (kc1 < cstart + WIN_W)
    o_ref[...] = jnp.where(col_ok, t * LOG2E, NEG)


def _bias_table(rpb):
    n_dr = 2 * WIN_H - 1
    p = jnp.pad(rpb, ((0, 0), (CB_OFFSET, CB_ENTRIES + 1 - n_dr - CB_OFFSET), (0, 1)))
    r2 = jnp.stack([p[:, 0:CB_ENTRIES], p[:, 1:CB_ENTRIES + 1]], axis=2)
    r2 = r2.reshape(C_HEADS * CB_ENTRIES, 2 * 32)
    t = pl.pallas_call(
        _bias_table_kernel,
        out_shape=jax.ShapeDtypeStruct((C_HEADS * CB_ENTRIES, GRID_W * LANES), F32),
        compiler_params=pltpu.CompilerParams(vmem_limit_bytes=VMEM_LIMIT),
        name="natten_bias_table",
    )(r2)
    return t.reshape(C_HEADS, CB_ENTRIES, GRID_W, LANES)


NAT_BLOCKS = 2


def _natten_block(blk, q_ref, k_ref, v_ref, kc_ref, vc_ref, cb_ref, o_ref, lo, e):
    rows = slice(blk * NAT_Q, (blk + 1) * NAT_Q)
    r0 = NAT_ROWS * (NAT_BLOCKS * pl.program_id(1) + blk)
    ws = jnp.clip(r0 - WIN_H // 2, 0, GRID_H - NAT_WIN_ROWS)
    koff = pl.multiple_of(ws * GRID_W, LANES)

    entry = []
    rmask = []
    for qr in range(NAT_ROWS):
        r = r0 + qr
        lo_r = jnp.clip(r - WIN_H // 2, 0, GRID_H - WIN_H) - r + (WIN_H - 1)
        ent_q, mask_q = [], []
        for j in range(NAT_TILES):
            dr0 = ws + 2 * j - r + (WIN_H - 1)
            ent_q.append(dr0 + CB_OFFSET)
            dr = dr0 + e
            ok = (dr >= lo_r) & (dr < lo_r + WIN_H)
            mask_q.append(jnp.where(ok, 0.0, NEG))
        entry.append(ent_q)
        rmask.append(mask_q)

    for p in range(C_HEADS // 2):
        sl = slice(p * LANES, (p + 1) * LANES)
        sl2 = slice(2 * p * LANES, (2 * p + 2) * LANES)
        q2 = _split_pair(q_ref[0, rows, sl], lo)
        bias = jnp.concatenate(
            [jnp.concatenate([cb_ref[2 * p + hh, entry[qr][j]] + rmask[qr][j] for j in range(NAT_TILES)], axis=1)
             for hh in range(2) for qr in range(NAT_ROWS)], axis=0)
        kv = [(k_ref[0, pl.ds(koff, NAT_WIN), sl], v_ref[0, pl.ds(koff, NAT_WIN), sl2]),
              (kc_ref[0, :, sl], vc_ref[0, :, sl2])]
        o2 = _softmax_pv(q2, kv, bias)
        o_ref[0, rows, sl] = jnp.where(lo, o2[:NAT_Q], o2[NAT_Q:]).astype(o_ref.dtype)


def _natten_kernel(q_ref, k_ref, v_ref, kc_ref, vc_ref, cb_ref, o_ref):
    lane = lax.broadcasted_iota(jnp.int32, (1, LANES), 1)
    lo = lane < HEAD_DIM
    e = jnp.where(lo, 0, 1)
    for blk in range(NAT_BLOCKS):
        _natten_block(blk, q_ref, k_ref, v_ref, kc_ref, vc_ref, cb_ref, o_ref, lo, e)


def _natten(q, k, v, kc, vc, cb):
    b = q.shape[0]
    tq = NAT_BLOCKS * NAT_Q
    qspec = pl.BlockSpec((1, tq, 512), lambda bb, i: (bb, i, 0))
    full = lambda a: pl.BlockSpec((1,) + a.shape[1:], lambda bb, i: (bb, 0, 0))
    return pl.pallas_call(
        _natten_kernel,
        grid=(b, SEQ // tq),
        in_specs=[qspec, full(k), full(v), full(kc), full(vc),
                  pl.BlockSpec(cb.shape, lambda bb, i: (0, 0, 0, 0))],
        out_specs=qspec,
        out_shape=jax.ShapeDtypeStruct((b, SEQ, 512), BF16),
        compiler_params=_cparams(("parallel", "parallel")),
        name="natten",
    )(q, k, v, kc, vc, cb)


def kernel(x, c, ctx, c_ctx, mod_w, mod_b, ln_g, ln_b, ffn_w1, ffn_w2, ab_w_in, ab_w_out, a_q_norm, a_k_norm,
           b_lambda, b_subln, cd_w_in, cd_w_out, c_rpb):
    b = x.shape[0]
    nctx = b * CTX_LEN
    tm = 512
    tmc = min(tm, nctx)

    cs = jnp.concatenate([c, c_ctx[None, :], jnp.zeros((MOD_ROWS - b - 1, D_MODEL), F32)], axis=0)
    mods = _modulation(cs, mod_w, mod_b).reshape(DEPTH, MOD_ROWS, 6, D_MODEL)
    ctx_row = b

    w1 = ffn_w1.astype(BF16)
    w2 = ffn_w2.astype(BF16)
    cflat = ctx.reshape(1, nctx, D_MODEL)

    w_in = ab_w_in[0]
    perm = np.asarray(QA_PERM)
    w_qa = w_in[:, :512].reshape(D_MODEL, A_HEADS, HEAD_DIM)[:, perm].reshape(D_MODEL, 512)
    w_ab = jnp.concatenate([w_qa, w_in[:, 512:1024], w_in[:, 1024:1152], w_in[:, 1280:1792],
                            w_in[:, 1152:1280], w_in[:, 1792:2304]], axis=1).astype(BF16)
    w_out = ab_w_out[0]
    w_out_ab = jnp.concatenate([w_out[:512].reshape(A_HEADS, HEAD_DIM, D_MODEL)[perm].reshape(512, D_MODEL),
                                w_out[512:]], axis=0).astype(BF16)
    gq = jnp.tile(a_q_norm[0], A_HEADS)[None, :]
    gk = jnp.tile(a_k_norm[0], A_KV_HEADS)[None, :]
    rope_tabs = tuple(jnp.asarray(t) for t in _rope_tables())
    subln = b_subln[0][None, :]
    mod0 = mods[0]

    xq = _inproj_ab(x, mod0, None, w_ab, gq, gk, rope_tabs, tm, "inproj_ab_x")
    cq = _inproj_ab(cflat, mod0, ctx_row, w_ab, gq, gk, None, tmc, "inproj_ab_ctx")
    cq = [a.reshape(b, CTX_LEN, a.shape[2]) for a in cq]
    qa_x, qb_x, ka_x, kb_x, va_x, vb_x = xq
    qa_c, qb_c, ka_c, kb_c, va_c, vb_c = cq
    c_kv = (ka_c, va_c, kb_c, vb_c)
    o_x = _attn_ab((qa_x, qb_x), (ka_x, va_x, kb_x, vb_x), c_kv, b_lambda[0], subln, 512, "attn_ab_x")
    o_c = _attn_ab((qa_c, qb_c), None, c_kv, b_lambda[0], subln, CTX_LEN, "attn_ab_ctx")

    x2 = _post([o_x], [w_out_ab], x, mod0, None, ln_g[0], ln_b[0], w1[0], w2[0], tm, "post0_x")
    c2 = _post([o_c.reshape(1, nctx, D_MODEL)], [w_out_ab], cflat, mod0, ctx_row, ln_g[0], ln_b[0], w1[0], w2[0],
               tmc, "post0_ctx")

    mod1 = mods[1]
    w_in = cd_w_in[0]
    ct, sn, cc, sc = _dft_tables()
    w_fa, w_fb = _dft_weights(w_in[:, 512:1024], jnp.asarray(cc), jnp.asarray(sc))
    w_cd = jnp.concatenate([w_in[:, :512].astype(BF16), w_fa, w_fb, w_in[:, 1024:].astype(BF16)], axis=1)
    q_n, f_a, f_b, k_n, v_n = _inproj_cd(x2, mod1, None, w_cd, True, tm, "inproj_cd_x")
    kc_n, vc_n = _inproj_cd(c2, mod1, ctx_row, w_in[:, 1024:].astype(BF16), False, tmc, "inproj_cd_ctx")
    kc_n = kc_n.reshape(b, CTX_LEN, 512)
    vc_n = vc_n.reshape(b, CTX_LEN, 1024)

    o_d = _dft_time(f_a, f_b, jnp.asarray(ct).astype(BF16), jnp.asarray(sn).astype(BF16))
    cb = _bias_table(c_rpb[0])
    o_n = _natten(q_n, k_n, v_n, kc_n, vc_n, cb)

    w_out = cd_w_out[0].astype(BF16)
    return _post([o_n, o_d], [w_out[:512], w_out[512:]], x2, mod1, None, ln_g[1], ln_b[1], w1[1], w2[1], tm, "post1_x")
```

```python
import functools
import math

import numpy as np
import jax
import jax.numpy as jnp
from jax import lax
from jax.experimental import pallas as pl
from jax.experimental.pallas import tpu as pltpu

F32 = jnp.float32
BF16 = jnp.bfloat16

D_MODEL = 1024
SEQ = 2048
CTX_LEN = 256
DEPTH = 2
GRID_W = 64
GRID_H = SEQ // GRID_W
HEAD_DIM = 64
A_HEADS = 8
A_KV_HEADS = 2
B_HEADS = 4
C_HEADS = 8
WIN_H = 8
WIN_W = 16
D_GROUPS = 4
D_GROUP_DIM = 128
FFN_DIM = 4 * D_MODEL
ROPE_THETA = 10000.0
ALPHA = (2.0 * DEPTH) ** 0.25
EPS = 1e-6
ATTN_SCALE = HEAD_DIM ** -0.5
LOG2E = math.log2(math.e)
Q_SCALE = ATTN_SCALE * LOG2E
LAM_INIT0 = 0.8 - 0.6 * math.exp(-0.3 * 0)
NEG = -1e30

LANES = 128
MOD_ROWS = 16
VMEM_LIMIT = 56 * 1024 * 1024

QA_PERM = (0, 4, 1, 5, 2, 6, 3, 7)

NAT_ROWS = 4
NAT_Q = NAT_ROWS * GRID_W
NAT_WIN_ROWS = NAT_ROWS + WIN_H
NAT_WIN = NAT_WIN_ROWS * GRID_W
NAT_TILES = NAT_WIN_ROWS // 2
_NAT_DR0 = [min(max(r0 - WIN_H // 2, 0), GRID_H - NAT_WIN_ROWS) + 2 * j - (r0 + qr) + WIN_H - 1
            for r0 in range(0, GRID_H, NAT_ROWS) for qr in range(NAT_ROWS) for j in range(NAT_TILES)]
CB_OFFSET = -min(_NAT_DR0)
CB_ENTRIES = max(_NAT_DR0) + CB_OFFSET + 1


def _cparams(sem):
    return pltpu.CompilerParams(dimension_semantics=sem, vmem_limit_bytes=VMEM_LIMIT)


def _dot(a, b):
    return jnp.dot(a, b, preferred_element_type=F32)


def _dot_nt(a, b):
    return lax.dot_general(a, b, (((1,), (1,)), ((), ())), preferred_element_type=F32)


def _split3(a):
    hi = a.astype(BF16)
    r1 = a - hi.astype(F32)
    mid = r1.astype(BF16)
    lo = (r1 - mid.astype(F32)).astype(BF16)
    return hi, mid, lo


def _layer_norm(z, g, b):
    mu = jnp.mean(z, axis=-1, keepdims=True)
    zc = z - mu
    var = jnp.mean(zc * zc, axis=-1, keepdims=True)
    return zc * lax.rsqrt(var + EPS) * g + b


@functools.lru_cache(maxsize=None)
def _rope_tables():
    pos = np.arange(SEQ)
    row = (pos // GRID_W).astype(np.float64)
    col = (pos % GRID_W).astype(np.float64)
    half = HEAD_DIM // 2
    freqs = np.power(ROPE_THETA, -np.arange(0, half, 2, dtype=np.float64) / half)
    def axis_angles(p):
        a = p[:, None] * freqs[None, :]
        return np.concatenate([a, a], axis=-1)
    ang = np.concatenate([axis_angles(row), axis_angles(col)], axis=-1)
    cos, sin = np.cos(ang), np.sin(ang)
    first = (np.arange(HEAD_DIM) % half) < (half // 2)
    sin_a = np.where(first[None, :], -sin, 0.0)
    sin_b = np.where(first[None, :], 0.0, sin)
    tile = lambda t: np.tile(t, (1, LANES // HEAD_DIM)).astype(np.float32)
    return tile(cos), tile(sin_a), tile(sin_b)


@functools.lru_cache(maxsize=None)
def _group_mean_matrix(width):
    g = np.arange(width) // HEAD_DIM
    return (g[:, None] == g[None, :]).astype(np.float32) / HEAD_DIM


@functools.lru_cache(maxsize=None)
def _dft_tables():
    t = np.arange(SEQ, dtype=np.int64)
    k = (t[:, None] * t[None, :]) % SEQ
    ang = 2.0 * np.pi * k.astype(np.float64) / SEQ
    ct = np.cos(ang).astype(np.float32)
    sn = (-np.sin(ang)).astype(np.float32)
    c = np.arange(D_GROUP_DIM, dtype=np.int64)
    kc = (c[:, None] * c[None, :]) % D_GROUP_DIM
    angc = 2.0 * np.pi * kc.astype(np.float64) / D_GROUP_DIM
    norm = 1.0 / math.sqrt(SEQ * D_GROUP_DIM)
    return ct, sn, (np.cos(angc) * norm).astype(np.float32), (np.sin(angc) * norm).astype(np.float32)


def _mod_kernel(c_ref, w_ref, b_ref, o_ref):
    c = c_ref[...]
    a = c / (1.0 + jnp.exp(-c))
    a_hi, a_mid, _ = _split3(a)
    w = w_ref[0]
    w_hi, w_mid, _ = _split3(w)
    o_ref[0, 0] = _dot(a_hi, w_hi) + (_dot(a_mid, w_hi) + _dot(a_hi, w_mid)) + b_ref[0]


def _modulation(cs, mod_w, mod_b):
    n = 6 * D_MODEL
    return pl.pallas_call(
        _mod_kernel,
        grid=(DEPTH, 6),
        in_specs=[pl.BlockSpec((MOD_ROWS, D_MODEL), lambda l, j: (0, 0)),
                  pl.BlockSpec((1, D_MODEL, D_MODEL), lambda l, j: (l, 0, j)),
                  pl.BlockSpec((1, 1, D_MODEL), lambda l, j: (l, 0, j))],
        out_specs=pl.BlockSpec((1, 1, MOD_ROWS, D_MODEL), lambda l, j: (l, j, 0, 0)),
        out_shape=jax.ShapeDtypeStruct((DEPTH, 6, MOD_ROWS, D_MODEL), F32),
        compiler_params=_cparams(("arbitrary", "arbitrary")),
        name="modulation",
    )(cs, mod_w, mod_b.reshape(DEPTH, 1, n))


def _tok_spec(tm, width):
    return pl.BlockSpec((1, tm, width), lambda g, i: (g, i, 0))


MOD_SPEC = pl.BlockSpec((6, MOD_ROWS, D_MODEL), lambda g, i: (0, 0, 0))


def _mod_vectors(mod_ref, mod_row):
    r = pl.program_id(0) if mod_row is None else mod_row
    return [mod_ref[k, pl.ds(r, 1), :] for k in range(6)]


def _const_spec(shape):
    nd = len(shape)
    return pl.BlockSpec(shape, lambda g, i: (0,) * nd)


AB_QA, AB_QB, AB_KA, AB_KB, AB_VA, AB_VB = 0, 512, 1024, 1152, 1664, 1792
AB_W = 2304


TOKEN_CHAIN_ROWS = 256


def _store_v_ext(o_ref, rows, v):
    ones = jnp.ones((v.shape[0], LANES), o_ref.dtype)
    for j in range(v.shape[1] // LANES):
        o_ref[0, rows, 2 * j * LANES:(2 * j + 1) * LANES] = v[:, j * LANES:(j + 1) * LANES].astype(o_ref.dtype)
        o_ref[0, rows, (2 * j + 1) * LANES:(2 * j + 2) * LANES] = ones


def _rope_chunk(c, cos, sin_a, sin_b):
    return c * cos + pltpu.roll(c, LANES - 16, axis=1) * sin_a + pltpu.roll(c, 16, axis=1) * sin_b


def _inproj_ab_kernel(*refs, rope, mod_row):
    if rope:
        (x_ref, mod_ref, w_ref, gq_ref, gk_ref, g512_ref, g128_ref, cos_ref, sa_ref, sb_ref,
         qa_ref, qb_ref, ka_ref, kb_ref, va_ref, vb_ref) = refs
    else:
        (x_ref, mod_ref, w_ref, gq_ref, gk_ref, g512_ref, g128_ref,
         qa_ref, qb_ref, ka_ref, kb_ref, va_ref, vb_ref) = refs

    def rms(v, g_ref, gm_ref):
        ms = _dot((v * v).astype(BF16), gm_ref[...])
        return v * lax.rsqrt(ms + EPS) * g_ref[...]

    shift, scale = _mod_vectors(mod_ref, mod_row)[0:2]
    tm = x_ref.shape[1]
    tc = min(TOKEN_CHAIN_ROWS, tm)
    for rb in range(tm // tc):
        rows = slice(rb * tc, (rb + 1) * tc)
        if rope:
            cos, sin_a, sin_b = cos_ref[rows, :], sa_ref[rows, :], sb_ref[rows, :]
        u = (x_ref[0, rows, :] * (1.0 + scale) + shift).astype(BF16)
        h = _dot(u, w_ref[...])

        def emit(v, o_ref, scale):
            for j in range(v.shape[1] // LANES):
                c = v[:, j * LANES:(j + 1) * LANES]
                if rope:
                    c = _rope_chunk(c, cos, sin_a, sin_b)
                if scale != 1.0:
                    c = c * scale
                o_ref[0, rows, j * LANES:(j + 1) * LANES] = c.astype(o_ref.dtype)

        emit(rms(h[:, AB_QA:AB_QB], gq_ref, g512_ref), qa_ref, Q_SCALE)
        emit(h[:, AB_QB:AB_KA], qb_ref, Q_SCALE)
        emit(rms(h[:, AB_KA:AB_KB], gk_ref, g128_ref), ka_ref, 1.0)
        emit(h[:, AB_KB:AB_VA], kb_ref, 1.0)
        _store_v_ext(va_ref, rows, h[:, AB_VA:AB_VB])
        _store_v_ext(vb_ref, rows, h[:, AB_VB:AB_W])


def _inproj_ab(tok, mod, mod_row, w, gq, gk, rope_tabs, tm, name):
    g, r, _ = tok.shape
    rope = rope_tabs is not None
    consts = [w, gq, gk, jnp.asarray(_group_mean_matrix(512), BF16), jnp.asarray(_group_mean_matrix(128), BF16)]
    in_specs = [_tok_spec(tm, D_MODEL), MOD_SPEC] + [_const_spec(c.shape) for c in consts]
    args = [tok, mod] + consts
    if rope:
        in_specs += [pl.BlockSpec((tm, LANES), lambda gg, i: (i, 0))] * 3
        args += list(rope_tabs)
    widths = (512, 512, 128, 512, 2 * 128, 2 * 512)
    return pl.pallas_call(
        functools.partial(_inproj_ab_kernel, rope=rope, mod_row=mod_row),
        grid=(g, r // tm),
        in_specs=in_specs,
        out_specs=[_tok_spec(tm, wd) for wd in widths],
        out_shape=[jax.ShapeDtypeStruct((g, r, wd), BF16) for wd in widths],
        compiler_params=_cparams(("parallel", "parallel")),
        name=name,
    )(*args)


def _softmax_pv(q2, kv_pairs, bias=None):
    scores = [_dot_nt(q2, k) for k, _ in kv_pairs]
    if bias is not None:
        scores[0] = scores[0] + bias
    m = scores[0].max(axis=-1, keepdims=True)
    for s in scores[1:]:
        m = jnp.maximum(m, s.max(axis=-1, keepdims=True))
    acc = None
    for s, (_, v) in zip(scores, kv_pairs):
        a = _dot(jnp.exp2(s - m).astype(BF16), v)
        acc = a if acc is None else acc + a
    return acc[:, :LANES] / acc[:, LANES:]


def _split_pair(qp, lo):
    zero = jnp.zeros_like(qp)
    return jnp.concatenate([jnp.where(lo, qp, zero), jnp.where(lo, zero, qp)], axis=0)


ATTN_CHAIN_ROWS = 256


def _attn_ab_kernel(*refs, with_x):
    if with_x:
        (qa_ref, qb_ref, kax_ref, vax_ref, kbx_ref, vbx_ref, kac_ref, vac_ref, kbc_ref, vbc_ref,
         lam_ref, subln_ref, o_ref) = refs
    else:
        (qa_ref, qb_ref, kac_ref, vac_ref, kbc_ref, vbc_ref, lam_ref, subln_ref, o_ref) = refs
    tc = ATTN_CHAIN_ROWS
    lo = lax.broadcasted_iota(jnp.int32, (1, LANES), 1) < HEAD_DIM

    lp = lam_ref[...]
    lam = (jnp.exp(jnp.sum(lp[0:1] * lp[1:2], axis=-1, keepdims=True))
           - jnp.exp(jnp.sum(lp[2:3] * lp[3:4], axis=-1, keepdims=True)) + LAM_INIT0)

    for rb in range(qa_ref.shape[1] // tc):
        rows = slice(rb * tc, (rb + 1) * tc)
        for p in range(A_HEADS // 2):
            sl = slice(p * LANES, (p + 1) * LANES)
            q2 = _split_pair(qa_ref[0, rows, sl], lo)
            kv = [(kac_ref[0], vac_ref[0])]
            if with_x:
                kv.append((kax_ref[0], vax_ref[0]))
            o2 = _softmax_pv(q2, kv)
            o_ref[0, rows, sl] = jnp.where(lo, o2[:tc], o2[tc:]).astype(o_ref.dtype)

        for i in range(B_HEADS):
            sl = slice(i * LANES, (i + 1) * LANES)
            sl2 = slice(2 * i * LANES, (2 * i + 2) * LANES)
            q2 = _split_pair(qb_ref[0, rows, sl], lo)
            kv = [(kbc_ref[0, :, sl], vbc_ref[0, :, sl2])]
            if with_x:
                kv.append((kbx_ref[0, :, sl], vbx_ref[0, :, sl2]))
            o2 = _softmax_pv(q2, kv)
            o = o2[:tc] - lam * o2[tc:]
            ms = jnp.mean(o * o, axis=-1, keepdims=True)
            o = o * lax.rsqrt(ms + EPS) * subln_ref[...] * (1.0 - LAM_INIT0)
            o_ref[0, rows, 512 + i * LANES:512 + (i + 1) * LANES] = o.astype(o_ref.dtype)


def _attn_ab(q_parts, x_kv, c_kv, lam_p, subln, tq, name):
    qa, qb = q_parts
    b, nq, _ = qa.shape
    with_x = x_kv is not None

    def full(a):
        return pl.BlockSpec((1,) + a.shape[1:], lambda bb, i: (bb, 0, 0))

    args = [qa, qb]
    in_specs = [pl.BlockSpec((1, tq, 512), lambda bb, i: (bb, i, 0))] * 2
    if with_x:
        args += list(x_kv)
        in_specs += [full(a) for a in x_kv]
    args += list(c_kv) + [lam_p, subln]
    in_specs += [full(a) for a in c_kv] + [pl.BlockSpec(lam_p.shape, lambda bb, i: (0, 0)),
                                           pl.BlockSpec(subln.shape, lambda bb, i: (0, 0))]
    return pl.pallas_call(
        functools.partial(_attn_ab_kernel, with_x=with_x),
        grid=(b, nq // tq),
        in_specs=in_specs,
        out_specs=pl.BlockSpec((1, tq, D_MODEL), lambda bb, i: (bb, i, 0)),
        out_shape=jax.ShapeDtypeStruct((b, nq, D_MODEL), BF16),
        compiler_params=_cparams(("parallel", "parallel")),
        name=name,
    )(*args)


FFN_CHUNK = 1024


def _post_kernel(*refs, n_parts, mod_row):
    o_refs = refs[:n_parts]
    wo_refs = refs[n_parts:2 * n_parts]
    x_ref, mod_ref, g_ref, b_ref, w1_ref, w2_ref, y_ref = refs[2 * n_parts:]
    _, _, gate1, shift2, scale2, gate2 = _mod_vectors(mod_ref, mod_row)
    y = _dot(o_refs[0][0], wo_refs[0][...])
    for o_ref, w_ref in zip(o_refs[1:], wo_refs[1:]):
        y = y + _dot(o_ref[0], w_ref[...])
    x1 = _layer_norm(ALPHA * x_ref[0] + gate1 * y, g_ref[0:1, :], b_ref[0:1, :])
    u = (x1 * (1.0 + scale2) + shift2).astype(BF16)
    acc = None
    for c in range(FFN_DIM // FFN_CHUNK):
        sl = slice(c * FFN_CHUNK, (c + 1) * FFN_CHUNK)
        h = jnp.maximum(_dot(u, w1_ref[:, sl]), 0.0)
        a = _dot((h * h).astype(BF16), w2_ref[sl, :])
        acc = a if acc is None else acc + a
    y_ref[0] = _layer_norm(ALPHA * x1 + gate2 * acc, g_ref[1:2, :], b_ref[1:2, :])


def _post(o_parts, wo_parts, resid, mod, mod_row, ln_g, ln_b, w1, w2, tm, name):
    g, r, _ = resid.shape
    n = len(o_parts)
    single = pl.Buffered(1)
    const1 = lambda a: pl.BlockSpec(a.shape, lambda gg, i: (0,) * a.ndim, pipeline_mode=single)
    in_specs = ([_tok_spec(tm, o.shape[2]) for o in o_parts] + [const1(w) for w in wo_parts]
                + [_tok_spec(tm, D_MODEL), MOD_SPEC, _const_spec(ln_g.shape), _const_spec(ln_b.shape),
                   const1(w1), const1(w2)])
    return pl.pallas_call(
        functools.partial(_post_kernel, n_parts=n, mod_row=mod_row),
        grid=(g, r // tm),
        in_specs=in_specs,
        out_specs=_tok_spec(tm, D_MODEL),
        out_shape=jax.ShapeDtypeStruct((g, r, D_MODEL), F32),
        compiler_params=_cparams(("parallel", "parallel")),
        name=name,
    )(*o_parts, *wo_parts, resid, mod, ln_g, ln_b, w1, w2)


def _inproj_cd_kernel(x_ref, mod_ref, w_ref, *o_refs, with_q, mod_row):
    x = x_ref[0]
    shift, scale = _mod_vectors(mod_ref, mod_row)[0:2]
    u = (x * (1.0 + scale) + shift).astype(BF16)
    h = _dot(u, w_ref[...])
    off = 0
    for k, o_ref in enumerate(o_refs[:-1]):
        v = h[:, off:off + 512]
        if with_q and k == 0:
            v = v * Q_SCALE
        o_ref[0] = v.astype(o_ref.dtype)
        off += 512
    _store_v_ext(o_refs[-1], slice(None), h[:, off:off + 512])


def _inproj_cd(tok, mod, mod_row, w, with_q, tm, name):
    g, r, _ = tok.shape
    widths = (w.shape[1] // 512 - 1) * (512,) + (1024,)
    return pl.pallas_call(
        functools.partial(_inproj_cd_kernel, with_q=with_q, mod_row=mod_row),
        grid=(g, r // tm),
        in_specs=[_tok_spec(tm, D_MODEL), MOD_SPEC, _const_spec(w.shape)],
        out_specs=[_tok_spec(tm, wd) for wd in widths],
        out_shape=[jax.ShapeDtypeStruct((g, r, wd), BF16) for wd in widths],
        compiler_params=_cparams(("parallel", "parallel")),
        name=name,
    )(tok, mod, w)


def _dft_weight_kernel(w_ref, cc_ref, sc_ref, wa_ref, wb_ref):
    c_hi, c_mid, _ = _split3(cc_ref[...])
    s_hi, s_mid, _ = _split3(sc_ref[...])
    for gi in range(D_GROUPS):
        sl = slice(gi * D_GROUP_DIM, (gi + 1) * D_GROUP_DIM)
        w_hi, w_mid, _ = _split3(w_ref[:, sl])
        wa_ref[:, sl] = (_dot(w_hi, c_hi) + (_dot(w_mid, c_hi) + _dot(w_hi, c_mid))).astype(BF16)
        wb_ref[:, sl] = (_dot(w_hi, s_hi) + (_dot(w_mid, s_hi) + _dot(w_hi, s_mid))).astype(BF16)


def _dft_weights(w_fd, cc, sc):
    return pl.pallas_call(
        _dft_weight_kernel,
        out_shape=[jax.ShapeDtypeStruct(w_fd.shape, BF16)] * 2,
        compiler_params=pltpu.CompilerParams(vmem_limit_bytes=VMEM_LIMIT),
        name="dft_weights",
    )(w_fd, cc, sc)


DFT_ROWS = 512


def _dft_time_kernel(a_ref, b_ref, ct_ref, sn_ref, o_ref):
    a = a_ref[0]
    bm = b_ref[0]
    for r in range(SEQ // DFT_ROWS):
        sl = slice(r * DFT_ROWS, (r + 1) * DFT_ROWS)
        y = _dot(ct_ref[sl, :], a) + _dot(sn_ref[sl, :], bm)
        o_ref[0, sl, :] = y.astype(o_ref.dtype)


def _dft_time(a, bm, ct, sn):
    b = a.shape[0]
    single = pl.Buffered(1)
    tok = pl.BlockSpec((1, SEQ, 512), lambda bb: (bb, 0, 0))
    tab = pl.BlockSpec((SEQ, SEQ), lambda bb: (0, 0), pipeline_mode=single)
    return pl.pallas_call(
        _dft_time_kernel,
        grid=(b,),
        in_specs=[tok, tok, tab, tab],
        out_specs=tok,
        out_shape=jax.ShapeDtypeStruct((b, SEQ, 512), BF16),
        compiler_params=_cparams(("parallel",)),
        name="dft_time",
    )(a, bm, ct, sn)


def _bias_table_kernel(r_ref, o_ref):
    n = GRID_W * LANES
    row = lax.broadcasted_iota(jnp.int32, (2 * 32, n), 0)
    lane = lax.broadcasted_iota(jnp.int32, (2 * 32, n), 1)
    e_row, i_row = row >> 5, row & 31
    c = lane >> 7
    e_lane = (lane >> 6) & 1
    kc = lane & (GRID_W - 1)
    sel = jnp.where((e_row == e_lane) & (kc - c + (WIN_W - 1) == i_row), 1.0, 0.0).astype(BF16)
    hi, mid, lo = _split3(r_ref[...])
    t = _dot(hi, sel) + _dot(mid, sel) + _dot(lo, sel)
    c1 = c[0:1]
    kc1 = kc[0:1]
    cstart = jnp.clip(c1 - WIN_W // 2, 0, GRID_W - WIN_W)
    col_ok = (kc1 >= cstart) & (kc1 < cstart + WIN_W)
    o_ref[...] = jnp.where(col_ok, t * LOG2E, NEG)


def _bias_table(rpb):
    n_dr = 2 * WIN_H - 1
    p = jnp.pad(rpb, ((0, 0), (CB_OFFSET, CB_ENTRIES + 1 - n_dr - CB_OFFSET), (0, 1)))
    r2 = jnp.stack([p[:, 0:CB_ENTRIES], p[:, 1:CB_ENTRIES + 1]], axis=2)
    r2 = r2.reshape(C_HEADS * CB_ENTRIES, 2 * 32)
    t = pl.pallas_call(
        _bias_table_kernel,
        out_shape=jax.ShapeDtypeStruct((C_HEADS * CB_ENTRIES, GRID_W * LANES), F32),
        compiler_params=pltpu.CompilerParams(vmem_limit_bytes=VMEM_LIMIT),
        name="natten_bias_table",
    )(r2)
    return t.reshape(C_HEADS, CB_ENTRIES, GRID_W, LANES)


NAT_BLOCKS = 4


def _natten_block(blk, q_ref, k_ref, v_ref, kc_ref, vc_ref, cb_ref, o_ref, lo, e):
    rows = slice(blk * NAT_Q, (blk + 1) * NAT_Q)
    r0 = NAT_ROWS * (NAT_BLOCKS * pl.program_id(1) + blk)
    ws = jnp.clip(r0 - WIN_H // 2, 0, GRID_H - NAT_WIN_ROWS)
    koff = pl.multiple_of(ws * GRID_W, LANES)

    entry = []
    rmask = []
    for qr in range(NAT_ROWS):
        r = r0 + qr
        lo_r = jnp.clip(r - WIN_H // 2, 0, GRID_H - WIN_H) - r + (WIN_H - 1)
        ent_q, mask_q = [], []
        for j in range(NAT_TILES):
            dr0 = ws + 2 * j - r + (WIN_H - 1)
            ent_q.append(dr0 + CB_OFFSET)
            dr = dr0 + e
            ok = (dr >= lo_r) & (dr < lo_r + WIN_H)
            mask_q.append(jnp.where(ok, 0.0, NEG))
        entry.append(ent_q)
        rmask.append(mask_q)

    for p in range(C_HEADS // 2):
        sl = slice(p * LANES, (p + 1) * LANES)
        sl2 = slice(2 * p * LANES, (2 * p + 2) * LANES)
        q2 = _split_pair(q_ref[0, rows, sl], lo)
        bias = jnp.concatenate(
            [jnp.concatenate([cb_ref[2 * p + hh, entry[qr][j]] + rmask[qr][j] for j in range(NAT_TILES)], axis=1)
             for hh in range(2) for qr in range(NAT_ROWS)], axis=0)
        kv = [(k_ref[0, pl.ds(koff, NAT_WIN), sl], v_ref[0, pl.ds(koff, NAT_WIN), sl2]),
              (kc_ref[0, :, sl], vc_ref[0, :, sl2])]
        o2 = _softmax_pv(q2, kv, bias)
        o_ref[0, rows, sl] = jnp.where(lo, o2[:NAT_Q], o2[NAT_Q:]).astype(o_ref.dtype)


def _natten_kernel(q_ref, k_ref, v_ref, kc_ref, vc_ref, cb_ref, o_ref):
    lane = lax.broadcasted_iota(jnp.int32, (1, LANES), 1)
    lo = lane < HEAD_DIM
    e = jnp.where(lo, 0, 1)
    for blk in range(NAT_BLOCKS):
        _natten_block(blk, q_ref, k_ref, v_ref, kc_ref, vc_ref, cb_ref, o_ref, lo, e)


def _natten(q, k, v, kc, vc, cb):
    b = q.shape[0]
    tq = NAT_BLOCKS * NAT_Q
    qspec = pl.BlockSpec((1, tq, 512), lambda bb, i: (bb, i, 0))
    full = lambda a: pl.BlockSpec((1,) + a.shape[1:], lambda bb, i: (bb, 0, 0))
    return pl.pallas_call(
        _natten_kernel,
        grid=(b, SEQ // tq),
        in_specs=[qspec, full(k), full(v), full(kc), full(vc),
                  pl.BlockSpec(cb.shape, lambda bb, i: (0, 0, 0, 0))],
        out_specs=qspec,
        out_shape=jax.ShapeDtypeStruct((b, SEQ, 512), BF16),
        compiler_params=_cparams(("parallel", "parallel")),
        name="natten",
    )(q, k, v, kc, vc, cb)


def kernel(x, c, ctx, c_ctx, mod_w, mod_b, ln_g, ln_b, ffn_w1, ffn_w2, ab_w_in, ab_w_out, a_q_norm, a_k_norm,
           b_lambda, b_subln, cd_w_in, cd_w_out, c_rpb):
    b = x.shape[0]
    nctx = b * CTX_LEN
    tm = 512
    tmc = min(tm, nctx)

    cs = jnp.concatenate([c, c_ctx[None, :], jnp.zeros((MOD_ROWS - b - 1, D_MODEL), F32)], axis=0)
    mods = _modulation(cs, mod_w, mod_b)
    ctx_row = b

    w1 = [ffn_w1[i].astype(BF16) for i in range(DEPTH)]
    w2 = [ffn_w2[i].astype(BF16) for i in range(DEPTH)]
    cflat = ctx.reshape(1, nctx, D_MODEL)

    w_in = ab_w_in[0]
    perm = np.asarray(QA_PERM)
    w_qa = w_in[:, :512].reshape(D_MODEL, A_HEADS, HEAD_DIM)[:, perm].reshape(D_MODEL, 512)
    w_ab = jnp.concatenate([w_qa, w_in[:, 512:1024], w_in[:, 1024:1152], w_in[:, 1280:1792],
                            w_in[:, 1152:1280], w_in[:, 1792:2304]], axis=1).astype(BF16)
    w_out = ab_w_out[0]
    w_out_ab = jnp.concatenate([w_out[:512].reshape(A_HEADS, HEAD_DIM, D_MODEL)[perm].reshape(512, D_MODEL),
                                w_out[512:]], axis=0).astype(BF16)
    gq = jnp.tile(a_q_norm[0], A_HEADS)[None, :]
    gk = jnp.tile(a_k_norm[0], A_KV_HEADS)[None, :]
    rope_tabs = tuple(jnp.asarray(t) for t in _rope_tables())
    subln = b_subln[0][None, :]
    mod0 = mods[0]

    xq = _inproj_ab(x, mod0, None, w_ab, gq, gk, rope_tabs, tm, "inproj_ab_x")
    cq = _inproj_ab(cflat, mod0, ctx_row, w_ab, gq, gk, None, tmc, "inproj_ab_ctx")
    cq = [a.reshape(b, CTX_LEN, a.shape[2]) for a in cq]
    qa_x, qb_x, ka_x, kb_x, va_x, vb_x = xq
    qa_c, qb_c, ka_c, kb_c, va_c, vb_c = cq
    c_kv = (ka_c, va_c, kb_c, vb_c)
    o_x = _attn_ab((qa_x, qb_x), (ka_x, va_x, kb_x, vb_x), c_kv, b_lambda[0], subln, 1024, "attn_ab_x")
    o_c = _attn_ab((qa_c, qb_c), None, c_kv, b_lambda[0], subln, CTX_LEN, "attn_ab_ctx")

    x2 = _post([o_x], [w_out_ab], x, mod0, None, ln_g[0], ln_b[0], w1[0], w2[0], tm, "post0_x")
    c2 = _post([o_c.reshape(1, nctx, D_MODEL)], [w_out_ab], cflat, mod0, ctx_row, ln_g[0], ln_b[0], w1[0], w2[0],
               tmc, "post0_ctx")

    mod1 = mods[1]
    w_in = cd_w_in[0]
    ct, sn, cc, sc = _dft_tables()
    w_fa, w_fb = _dft_weights(w_in[:, 512:1024], jnp.asarray(cc), jnp.asarray(sc))
    w_cd = jnp.concatenate([w_in[:, :512].astype(BF16), w_fa, w_fb, w_in[:, 1024:].astype(BF16)], axis=1)
    q_n, f_a, f_b, k_n, v_n = _inproj_cd(x2, mod1, None, w_cd, True, tm, "inproj_cd_x")
    kc_n, vc_n = _inproj_cd(c2, mod1, ctx_row, w_in[:, 1024:].astype(BF16), False, tmc, "inproj_cd_ctx")
    kc_n = kc_n.reshape(b, CTX_LEN, 512)
    vc_n = vc_n.reshape(b, CTX_LEN, 1024)

    o_d = _dft_time(f_a, f_b, jnp.asarray(ct).astype(BF16), jnp.asarray(sn).astype(BF16))
    cb = _bias_table(c_rpb[0])
    o_n = _natten(q_n, k_n, v_n, kc_n, vc_n, cb)

    w_out = cd_w_out[0].astype(BF16)
    return _post([o_n, o_d], [w_out[:512], w_out[512:]], x2, mod1, None, ln_g[1], ln_b[1], w1[1], w2[1], tm, "post1_x")
```

```python
import functools
import math

import numpy as np
import jax
import jax.numpy as jnp
from jax import lax
from jax.experimental import pallas as pl
from jax.experimental.pallas import tpu as pltpu

F32 = jnp.float32
BF16 = jnp.bfloat16

D_MODEL = 1024
SEQ = 2048
CTX_LEN = 256
DEPTH = 2
GRID_W = 64
GRID_H = SEQ // GRID_W
HEAD_DIM = 64
A_HEADS = 8
A_KV_HEADS = 2
B_HEADS = 4
C_HEADS = 8
WIN_H = 8
WIN_W = 16
D_GROUPS = 4
D_GROUP_DIM = 128
FFN_DIM = 4 * D_MODEL
ROPE_THETA = 10000.0
ALPHA = (2.0 * DEPTH) ** 0.25
EPS = 1e-6
ATTN_SCALE = HEAD_DIM ** -0.5
LOG2E = math.log2(math.e)
Q_SCALE = ATTN_SCALE * LOG2E
LAM_INIT0 = 0.8 - 0.6 * math.exp(-0.3 * 0)
NEG = -1e30

LANES = 128
MOD_ROWS = 16
VMEM_LIMIT = 56 * 1024 * 1024

QA_PERM = (0, 4, 1, 5, 2, 6, 3, 7)

NAT_ROWS = 4
NAT_Q = NAT_ROWS * GRID_W
NAT_WIN_ROWS = NAT_ROWS + WIN_H
NAT_WIN = NAT_WIN_ROWS * GRID_W
NAT_TILES = NAT_WIN_ROWS // 2
_NAT_DR0 = [min(max(r0 - WIN_H // 2, 0), GRID_H - NAT_WIN_ROWS) + 2 * j - (r0 + qr) + WIN_H - 1
            for r0 in range(0, GRID_H, NAT_ROWS) for qr in range(NAT_ROWS) for j in range(NAT_TILES)]
CB_OFFSET = -min(_NAT_DR0)
CB_ENTRIES = max(_NAT_DR0) + CB_OFFSET + 1


def _cparams(sem):
    return pltpu.CompilerParams(dimension_semantics=sem, vmem_limit_bytes=VMEM_LIMIT)


def _dot(a, b):
    return jnp.dot(a, b, preferred_element_type=F32)


def _dot_nt(a, b):
    return lax.dot_general(a, b, (((1,), (1,)), ((), ())), preferred_element_type=F32)


def _split3(a):
    hi = a.astype(BF16)
    r1 = a - hi.astype(F32)
    mid = r1.astype(BF16)
    lo = (r1 - mid.astype(F32)).astype(BF16)
    return hi, mid, lo


def _layer_norm(z, g, b):
    mu = jnp.mean(z, axis=-1, keepdims=True)
    zc = z - mu
    var = jnp.mean(zc * zc, axis=-1, keepdims=True)
    return zc * lax.rsqrt(var + EPS) * g + b


@functools.lru_cache(maxsize=None)
def _rope_tables():
    pos = np.arange(SEQ)
    row = (pos // GRID_W).astype(np.float64)
    col = (pos % GRID_W).astype(np.float64)
    half = HEAD_DIM // 2
    freqs = np.power(ROPE_THETA, -np.arange(0, half, 2, dtype=np.float64) / half)
    def axis_angles(p):
        a = p[:, None] * freqs[None, :]
        return np.concatenate([a, a], axis=-1)
    ang = np.concatenate([axis_angles(row), axis_angles(col)], axis=-1)
    cos, sin = np.cos(ang), np.sin(ang)
    first = (np.arange(HEAD_DIM) % half) < (half // 2)
    sin_a = np.where(first[None, :], -sin, 0.0)
    sin_b = np.where(first[None, :], 0.0, sin)
    tile = lambda t: np.tile(t, (1, LANES // HEAD_DIM)).astype(np.float32)
    return tile(cos), tile(sin_a), tile(sin_b)


@functools.lru_cache(maxsize=None)
def _group_mean_matrix(width):
    g = np.arange(width) // HEAD_DIM
    return (g[:, None] == g[None, :]).astype(np.float32) / HEAD_DIM


@functools.lru_cache(maxsize=None)
def _dft_tables():
    t = np.arange(SEQ, dtype=np.int64)
    k = (t[:, None] * t[None, :]) % SEQ
    ang = 2.0 * np.pi * k.astype(np.float64) / SEQ
    ct = np.cos(ang).astype(np.float32)
    sn = (-np.sin(ang)).astype(np.float32)
    c = np.arange(D_GROUP_DIM, dtype=np.int64)
    kc = (c[:, None] * c[None, :]) % D_GROUP_DIM
    angc = 2.0 * np.pi * kc.astype(np.float64) / D_GROUP_DIM
    norm = 1.0 / math.sqrt(SEQ * D_GROUP_DIM)
    return ct, sn, (np.cos(angc) * norm).astype(np.float32), (np.sin(angc) * norm).astype(np.float32)


def _mod_kernel(c_ref, w_ref, b_ref, o_ref):
    c = c_ref[...]
    a = c / (1.0 + jnp.exp(-c))
    a_hi, a_mid, _ = _split3(a)
    w = w_ref[0]
    w_hi, w_mid, _ = _split3(w)
    o_ref[0, 0] = _dot(a_hi, w_hi) + (_dot(a_mid, w_hi) + _dot(a_hi, w_mid)) + b_ref[0]


def _modulation(cs, mod_w, mod_b):
    n = 6 * D_MODEL
    return pl.pallas_call(
        _mod_kernel,
        grid=(DEPTH, 6),
        in_specs=[pl.BlockSpec((MOD_ROWS, D_MODEL), lambda l, j: (0, 0)),
                  pl.BlockSpec((1, D_MODEL, D_MODEL), lambda l, j: (l, 0, j)),
                  pl.BlockSpec((1, 1, D_MODEL), lambda l, j: (l, 0, j))],
        out_specs=pl.BlockSpec((1, 1, MOD_ROWS, D_MODEL), lambda l, j: (l, j, 0, 0)),
        out_shape=jax.ShapeDtypeStruct((DEPTH, 6, MOD_ROWS, D_MODEL), F32),
        compiler_params=_cparams(("arbitrary", "arbitrary")),
        name="modulation",
    )(cs, mod_w, mod_b.reshape(DEPTH, 1, n))


def _tok_spec(tm, width):
    return pl.BlockSpec((1, tm, width), lambda g, i: (g, i, 0))


MOD_SPEC = pl.BlockSpec((6, MOD_ROWS, D_MODEL), lambda g, i: (0, 0, 0))


def _mod_vectors(mod_ref, mod_row):
    r = pl.program_id(0) if mod_row is None else mod_row
    return [mod_ref[k, pl.ds(r, 1), :] for k in range(6)]


def _const_spec(shape):
    nd = len(shape)
    return pl.BlockSpec(shape, lambda g, i: (0,) * nd)


AB_QA, AB_QB, AB_KA, AB_VA, AB_KB, AB_VB, AB_W = 0, 512, 1024, 1152, 1280, 1792, 2304


TOKEN_CHAIN_ROWS = 256


def _store_v_ext(o_ref, rows, v):
    ones = jnp.ones((v.shape[0], LANES), o_ref.dtype)
    for j in range(v.shape[1] // LANES):
        o_ref[0, rows, 2 * j * LANES:(2 * j + 1) * LANES] = v[:, j * LANES:(j + 1) * LANES].astype(o_ref.dtype)
        o_ref[0, rows, (2 * j + 1) * LANES:(2 * j + 2) * LANES] = ones


def _rope_chunk(c, cos, sin_a, sin_b):
    return c * cos + pltpu.roll(c, LANES - 16, axis=1) * sin_a + pltpu.roll(c, 16, axis=1) * sin_b


def _inproj_ab_kernel(*refs, rope, mod_row):
    if rope:
        (x_ref, mod_ref, w_ref, gq_ref, gk_ref, g512_ref, g128_ref, cos_ref, sa_ref, sb_ref,
         qa_ref, qb_ref, ka_ref, kb_ref, va_ref, vb_ref) = refs
    else:
        (x_ref, mod_ref, w_ref, gq_ref, gk_ref, g512_ref, g128_ref,
         qa_ref, qb_ref, ka_ref, kb_ref, va_ref, vb_ref) = refs

    def rms(v, g_ref, gm_ref):
        ms = _dot((v * v).astype(BF16), gm_ref[...])
        return v * lax.rsqrt(ms + EPS) * g_ref[...]

    shift, scale = _mod_vectors(mod_ref, mod_row)[0:2]
    lo = lax.broadcasted_iota(jnp.int32, (1, LANES), 1) < HEAD_DIM
    tm = x_ref.shape[1]
    tc = min(TOKEN_CHAIN_ROWS, tm)
    for rb in range(tm // tc):
        rows = slice(rb * tc, (rb + 1) * tc)
        if rope:
            cos, sin_a, sin_b = cos_ref[rows, :], sa_ref[rows, :], sb_ref[rows, :]
        u = (x_ref[0, rows, :] * (1.0 + scale) + shift).astype(BF16)
        h = _dot(u, w_ref[...])

        def chunks(v, q_scale):
            out = []
            for j in range(v.shape[1] // LANES):
                c = v[:, j * LANES:(j + 1) * LANES]
                if rope:
                    c = _rope_chunk(c, cos, sin_a, sin_b)
                out.append(c * q_scale if q_scale != 1.0 else c)
            return out

        def store(o_ref, cs):
            for j, c in enumerate(cs):
                o_ref[0, rows, j * LANES:(j + 1) * LANES] = c.astype(o_ref.dtype)

        qa = chunks(rms(h[:, AB_QA:AB_QB], gq_ref, g512_ref), Q_SCALE)
        qa_perm = []
        for p in range(A_HEADS // 2):
            a, b = qa[p // 2], qa[A_HEADS // 4 + p // 2]
            if p % 2 == 0:
                qa_perm.append(jnp.where(lo, a, pltpu.roll(b, HEAD_DIM, axis=1)))
            else:
                qa_perm.append(jnp.where(lo, pltpu.roll(a, HEAD_DIM, axis=1), b))
        store(qa_ref, qa_perm)
        store(qb_ref, chunks(h[:, AB_QB:AB_KA], Q_SCALE))
        store(ka_ref, chunks(rms(h[:, AB_KA:AB_VA], gk_ref, g128_ref), 1.0))
        store(kb_ref, chunks(h[:, AB_KB:AB_VB], 1.0))
        _store_v_ext(va_ref, rows, h[:, AB_VA:AB_KB])
        _store_v_ext(vb_ref, rows, h[:, AB_VB:AB_W])


def _inproj_ab(tok, mod, mod_row, w, gq, gk, rope_tabs, tm, name):
    g, r, _ = tok.shape
    rope = rope_tabs is not None
    consts = [w, gq, gk, jnp.asarray(_group_mean_matrix(512), BF16), jnp.asarray(_group_mean_matrix(128), BF16)]
    in_specs = [_tok_spec(tm, D_MODEL), MOD_SPEC] + [_const_spec(c.shape) for c in consts]
    args = [tok, mod] + consts
    if rope:
        in_specs += [pl.BlockSpec((tm, LANES), lambda gg, i: (i, 0))] * 3
        args += list(rope_tabs)
    widths = (512, 512, 128, 512, 2 * 128, 2 * 512)
    return pl.pallas_call(
        functools.partial(_inproj_ab_kernel, rope=rope, mod_row=mod_row),
        grid=(g, r // tm),
        in_specs=in_specs,
        out_specs=[_tok_spec(tm, wd) for wd in widths],
        out_shape=[jax.ShapeDtypeStruct((g, r, wd), BF16) for wd in widths],
        compiler_params=_cparams(("parallel", "parallel")),
        name=name,
    )(*args)


def _softmax_pv(q2, kv_pairs, bias=None):
    scores = [_dot_nt(q2, k) for k, _ in kv_pairs]
    if bias is not None:
        scores[0] = scores[0] + bias
    m = scores[0].max(axis=-1, keepdims=True)
    for s in scores[1:]:
        m = jnp.maximum(m, s.max(axis=-1, keepdims=True))
    acc = None
    for s, (_, v) in zip(scores, kv_pairs):
        a = _dot(jnp.exp2(s - m).astype(BF16), v)
        acc = a if acc is None else acc + a
    return acc[:, :LANES] / acc[:, LANES:]


def _split_pair(qp, lo):
    zero = jnp.zeros_like(qp)
    return jnp.concatenate([jnp.where(lo, qp, zero), jnp.where(lo, zero, qp)], axis=0)


ATTN_CHAIN_ROWS = 256


def _attn_ab_kernel(*refs, with_x):
    if with_x:
        (qa_ref, qb_ref, kax_ref, vax_ref, kbx_ref, vbx_ref, kac_ref, vac_ref, kbc_ref, vbc_ref,
         lam_ref, subln_ref, o_ref) = refs
    else:
        (qa_ref, qb_ref, kac_ref, vac_ref, kbc_ref, vbc_ref, lam_ref, subln_ref, o_ref) = refs
    tc = ATTN_CHAIN_ROWS
    lo = lax.broadcasted_iota(jnp.int32, (1, LANES), 1) < HEAD_DIM

    lp = lam_ref[...]
    lam = (jnp.exp(jnp.sum(lp[0:1] * lp[1:2], axis=-1, keepdims=True))
           - jnp.exp(jnp.sum(lp[2:3] * lp[3:4], axis=-1, keepdims=True)) + LAM_INIT0)

    for rb in range(qa_ref.shape[1] // tc):
        rows = slice(rb * tc, (rb + 1) * tc)
        for p in range(A_HEADS // 2):
            sl = slice(p * LANES, (p + 1) * LANES)
            q2 = _split_pair(qa_ref[0, rows, sl], lo)
            kv = [(kac_ref[0], vac_ref[0])]
            if with_x:
                kv.append((kax_ref[0], vax_ref[0]))
            o2 = _softmax_pv(q2, kv)
            o_ref[0, rows, sl] = jnp.where(lo, o2[:tc], o2[tc:]).astype(o_ref.dtype)

        for i in range(B_HEADS):
            sl = slice(i * LANES, (i + 1) * LANES)
            sl2 = slice(2 * i * LANES, (2 * i + 2) * LANES)
            q2 = _split_pair(qb_ref[0, rows, sl], lo)
            kv = [(kbc_ref[0, :, sl], vbc_ref[0, :, sl2])]
            if with_x:
                kv.append((kbx_ref[0, :, sl], vbx_ref[0, :, sl2]))
            o2 = _softmax_pv(q2, kv)
            o = o2[:tc] - lam * o2[tc:]
            ms = jnp.mean(o * o, axis=-1, keepdims=True)
            o = o * lax.rsqrt(ms + EPS) * subln_ref[...] * (1.0 - LAM_INIT0)
            o_ref[0, rows, 512 + i * LANES:512 + (i + 1) * LANES] = o.astype(o_ref.dtype)


def _attn_ab(q_parts, x_kv, c_kv, lam_p, subln, tq, name):
    qa, qb = q_parts
    b, nq, _ = qa.shape
    with_x = x_kv is not None

    def full(a):
        return pl.BlockSpec((1,) + a.shape[1:], lambda bb, i: (bb, 0, 0))

    args = [qa, qb]
    in_specs = [pl.BlockSpec((1, tq, 512), lambda bb, i: (bb, i, 0))] * 2
    if with_x:
        args += list(x_kv)
        in_specs += [full(a) for a in x_kv]
    args += list(c_kv) + [lam_p, subln]
    in_specs += [full(a) for a in c_kv] + [pl.BlockSpec(lam_p.shape, lambda bb, i: (0, 0)),
                                           pl.BlockSpec(subln.shape, lambda bb, i: (0, 0))]
    return pl.pallas_call(
        functools.partial(_attn_ab_kernel, with_x=with_x),
        grid=(b, nq // tq),
        in_specs=in_specs,
        out_specs=pl.BlockSpec((1, tq, D_MODEL), lambda bb, i: (bb, i, 0)),
        out_shape=jax.ShapeDtypeStruct((b, nq, D_MODEL), BF16),
        compiler_params=_cparams(("parallel", "parallel")),
        name=name,
    )(*args)


FFN_CHUNK = 1024


def _post_kernel(*refs, n_parts, mod_row):
    o_refs = refs[:n_parts]
    wo_refs = refs[n_parts:2 * n_parts]
    x_ref, mod_ref, g_ref, b_ref, w1_ref, w2_ref, y_ref = refs[2 * n_parts:]
    _, _, gate1, shift2, scale2, gate2 = _mod_vectors(mod_ref, mod_row)
    y = _dot(o_refs[0][0], wo_refs[0][...])
    for o_ref, w_ref in zip(o_refs[1:], wo_refs[1:]):
        y = y + _dot(o_ref[0], w_ref[...])
    x1 = _layer_norm(ALPHA * x_ref[0] + gate1 * y, g_ref[0:1, :], b_ref[0:1, :])
    u = (x1 * (1.0 + scale2) + shift2).astype(BF16)
    acc = None
    for c in range(FFN_DIM // FFN_CHUNK):
        sl = slice(c * FFN_CHUNK, (c + 1) * FFN_CHUNK)
        h = jnp.maximum(_dot(u, w1_ref[:, sl]), 0.0)
        a = _dot((h * h).astype(BF16), w2_ref[sl, :])
        acc = a if acc is None else acc + a
    y_ref[0] = _layer_norm(ALPHA * x1 + gate2 * acc, g_ref[1:2, :], b_ref[1:2, :])


def _post(o_parts, wo_parts, resid, mod, mod_row, ln_g, ln_b, w1, w2, tm, name):
    g, r, _ = resid.shape
    n = len(o_parts)
    single = pl.Buffered(1)
    const1 = lambda a: pl.BlockSpec(a.shape, lambda gg, i: (0,) * a.ndim, pipeline_mode=single)
    in_specs = ([_tok_spec(tm, o.shape[2]) for o in o_parts] + [const1(w) for w in wo_parts]
                + [_tok_spec(tm, D_MODEL), MOD_SPEC, _const_spec(ln_g.shape), _const_spec(ln_b.shape),
                   const1(w1), const1(w2)])
    return pl.pallas_call(
        functools.partial(_post_kernel, n_parts=n, mod_row=mod_row),
        grid=(g, r // tm),
        in_specs=in_specs,
        out_specs=_tok_spec(tm, D_MODEL),
        out_shape=jax.ShapeDtypeStruct((g, r, D_MODEL), F32),
        compiler_params=_cparams(("parallel", "parallel")),
        name=name,
    )(*o_parts, *wo_parts, resid, mod, ln_g, ln_b, w1, w2)


CD_Q, CD_F, CD_K, CD_V, CD_W = 0, 512, 1024, 1536, 2048


def _inproj_cd_kernel(*refs, with_q, mod_row):
    if with_q:
        x_ref, mod_ref, w_ref, wa_ref, wb_ref, q_ref, fa_ref, fb_ref, k_ref, v_ref = refs
    else:
        x_ref, mod_ref, w_ref, k_ref, v_ref = refs
    shift, scale = _mod_vectors(mod_ref, mod_row)[0:2]
    u = (x_ref[0] * (1.0 + scale) + shift).astype(BF16)
    if with_q:
        q_ref[0] = (_dot(u, w_ref[:, CD_Q:CD_F]) * Q_SCALE).astype(q_ref.dtype)
        fa_ref[0] = _dot(u, wa_ref[...]).astype(fa_ref.dtype)
        fb_ref[0] = _dot(u, wb_ref[...]).astype(fb_ref.dtype)
    kv = _dot(u, w_ref[:, CD_K:CD_W])
    k_ref[0] = kv[:, :CD_V - CD_K].astype(k_ref.dtype)
    _store_v_ext(v_ref, slice(None), kv[:, CD_V - CD_K:])


def _inproj_cd(tok, mod, mod_row, w, w_ab, tm, name):
    g, r, _ = tok.shape
    with_q = w_ab is not None
    weights = [w] + (list(w_ab) if with_q else [])
    widths = ((512, 512, 512) if with_q else ()) + (512, 1024)
    return pl.pallas_call(
        functools.partial(_inproj_cd_kernel, with_q=with_q, mod_row=mod_row),
        grid=(g, r // tm),
        in_specs=[_tok_spec(tm, D_MODEL), MOD_SPEC] + [_const_spec(a.shape) for a in weights],
        out_specs=[_tok_spec(tm, wd) for wd in widths],
        out_shape=[jax.ShapeDtypeStruct((g, r, wd), BF16) for wd in widths],
        compiler_params=_cparams(("parallel", "parallel")),
        name=name,
    )(tok, mod, *weights)


def _dft_weight_kernel(w_ref, cc_ref, sc_ref, wa_ref, wb_ref):
    c_hi, c_mid, _ = _split3(cc_ref[...])
    s_hi, s_mid, _ = _split3(sc_ref[...])
    for gi in range(D_GROUPS):
        sl = slice(gi * D_GROUP_DIM, (gi + 1) * D_GROUP_DIM)
        w_hi, w_mid, _ = _split3(w_ref[:, sl])
        wa_ref[:, sl] = (_dot(w_hi, c_hi) + (_dot(w_mid, c_hi) + _dot(w_hi, c_mid))).astype(BF16)
        wb_ref[:, sl] = (_dot(w_hi, s_hi) + (_dot(w_mid, s_hi) + _dot(w_hi, s_mid))).astype(BF16)


def _dft_weights(w_in, cc, sc):
    width = CD_K - CD_F
    out = pl.BlockSpec((D_MODEL, width), lambda i: (0, 0))
    return pl.pallas_call(
        _dft_weight_kernel,
        grid=(1,),
        in_specs=[pl.BlockSpec((D_MODEL, width), lambda i: (0, CD_F // width)),
                  pl.BlockSpec(cc.shape, lambda i: (0, 0)), pl.BlockSpec(sc.shape, lambda i: (0, 0))],
        out_specs=[out, out],
        out_shape=[jax.ShapeDtypeStruct((D_MODEL, width), BF16)] * 2,
        compiler_params=_cparams(("arbitrary",)),
        name="dft_weights",
    )(w_in, cc, sc)


DFT_ROWS = 512


def _dft_time_kernel(a_ref, b_ref, ct_ref, sn_ref, o_ref):
    a = a_ref[0]
    bm = b_ref[0]
    for r in range(SEQ // DFT_ROWS):
        sl = slice(r * DFT_ROWS, (r + 1) * DFT_ROWS)
        y = _dot(ct_ref[sl, :], a) + _dot(sn_ref[sl, :], bm)
        o_ref[0, sl, :] = y.astype(o_ref.dtype)


def _dft_time(a, bm, ct, sn):
    b = a.shape[0]
    single = pl.Buffered(1)
    tok = pl.BlockSpec((1, SEQ, 512), lambda bb: (bb, 0, 0))
    tab = pl.BlockSpec((SEQ, SEQ), lambda bb: (0, 0), pipeline_mode=single)
    return pl.pallas_call(
        _dft_time_kernel,
        grid=(b,),
        in_specs=[tok, tok, tab, tab],
        out_specs=tok,
        out_shape=jax.ShapeDtypeStruct((b, SEQ, 512), BF16),
        compiler_params=_cparams(("parallel",)),
        name="dft_time",
    )(a, bm, ct, sn)


def _bias_table_kernel(r_ref, o_ref):
    n = GRID_W * LANES
    row = lax.broadcasted_iota(jnp.int32, (2 * 32, n), 0)
    lane = lax.broadcasted_iota(jnp.int32, (2 * 32, n), 1)
    e_row, i_row = row >> 5, row & 31
    c = lane >> 7
    e_lane = (lane >> 6) & 1
    kc = lane & (GRID_W - 1)
    sel = jnp.where((e_row == e_lane) & (kc - c + (WIN_W - 1) == i_row), 1.0, 0.0).astype(BF16)
    hi, mid, lo = _split3(r_ref[...])
    t = _dot(hi, sel) + _dot(mid, sel) + _dot(lo, sel)
    c1 = c[0:1]
    kc1 = kc[0:1]
    cstart = jnp.clip(c1 - WIN_W // 2, 0, GRID_W - WIN_W)
    col_ok = (kc1 >= cstart) & (kc1 < cstart + WIN_W)
    o_ref[...] = jnp.where(col_ok, t * LOG2E, NEG)


def _bias_table(rpb):
    n_dr = 2 * WIN_H - 1
    p = jnp.pad(rpb, ((0, 0), (CB_OFFSET, CB_ENTRIES + 1 - n_dr - CB_OFFSET), (0, 1)))
    r2 = jnp.stack([p[:, 0:CB_ENTRIES], p[:, 1:CB_ENTRIES + 1]], axis=2)
    r2 = r2.reshape(C_HEADS * CB_ENTRIES, 2 * 32)
    t = pl.pallas_call(
        _bias_table_kernel,
        out_shape=jax.ShapeDtypeStruct((C_HEADS * CB_ENTRIES, GRID_W * LANES), F32),
        compiler_params=pltpu.CompilerParams(vmem_limit_bytes=VMEM_LIMIT),
        name="natten_bias_table",
    )(r2)
    return t.reshape(C_HEADS, CB_ENTRIES, GRID_W, LANES)


NAT_BLOCKS = 4


def _natten_block(blk, q_ref, k_ref, v_ref, kc_ref, vc_ref, cb_ref, o_ref, lo, e):
    rows = slice(blk * NAT_Q, (blk + 1) * NAT_Q)
    r0 = NAT_ROWS * (NAT_BLOCKS * pl.program_id(1) + blk)
    ws = jnp.clip(r0 - WIN_H // 2, 0, GRID_H - NAT_WIN_ROWS)
    koff = pl.multiple_of(ws * GRID_W, LANES)

    entry = []
    rmask = []
    for qr in range(NAT_ROWS):
        r = r0 + qr
        lo_r = jnp.clip(r - WIN_H // 2, 0, GRID_H - WIN_H) - r + (WIN_H - 1)
        ent_q, mask_q = [], []
        for j in range(NAT_TILES):
            dr0 = ws + 2 * j - r + (WIN_H - 1)
            ent_q.append(dr0 + CB_OFFSET)
            dr = dr0 + e
            ok = (dr >= lo_r) & (dr < lo_r + WIN_H)
            mask_q.append(jnp.where(ok, 0.0, NEG))
        entry.append(ent_q)
        rmask.append(mask_q)

    for p in range(C_HEADS // 2):
        sl = slice(p * LANES, (p + 1) * LANES)
        sl2 = slice(2 * p * LANES, (2 * p + 2) * LANES)
        q2 = _split_pair(q_ref[0, rows, sl], lo)
        bias = jnp.concatenate(
            [jnp.concatenate([cb_ref[2 * p + hh, entry[qr][j]] + rmask[qr][j] for j in range(NAT_TILES)], axis=1)
             for hh in range(2) for qr in range(NAT_ROWS)], axis=0)
        kv = [(k_ref[0, pl.ds(koff, NAT_WIN), sl], v_ref[0, pl.ds(koff, NAT_WIN), sl2]),
              (kc_ref[0, :, sl], vc_ref[0, :, sl2])]
        o2 = _softmax_pv(q2, kv, bias)
        o_ref[0, rows, sl] = jnp.where(lo, o2[:NAT_Q], o2[NAT_Q:]).astype(o_ref.dtype)


def _natten_kernel(q_ref, k_ref, v_ref, kc_ref, vc_ref, cb_ref, o_ref):
    lane = lax.broadcasted_iota(jnp.int32, (1, LANES), 1)
    lo = lane < HEAD_DIM
    e = jnp.where(lo, 0, 1)
    for blk in range(NAT_BLOCKS):
        _natten_block(blk, q_ref, k_ref, v_ref, kc_ref, vc_ref, cb_ref, o_ref, lo, e)


def _natten(q, k, v, kc, vc, cb):
    b = q.shape[0]
    tq = NAT_BLOCKS * NAT_Q
    qspec = pl.BlockSpec((1, tq, 512), lambda bb, i: (bb, i, 0))
    full = lambda a: pl.BlockSpec((1,) + a.shape[1:], lambda bb, i: (bb, 0, 0))
    return pl.pallas_call(
        _natten_kernel,
        grid=(b, SEQ // tq),
        in_specs=[qspec, full(k), full(v), full(kc), full(vc),
                  pl.BlockSpec(cb.shape, lambda bb, i: (0, 0, 0, 0))],
        out_specs=qspec,
        out_shape=jax.ShapeDtypeStruct((b, SEQ, 512), BF16),
        compiler_params=_cparams(("parallel", "parallel")),
        name="natten",
    )(q, k, v, kc, vc, cb)


def kernel(x, c, ctx, c_ctx, mod_w, mod_b, ln_g, ln_b, ffn_w1, ffn_w2, ab_w_in, ab_w_out, a_q_norm, a_k_norm,
           b_lambda, b_subln, cd_w_in, cd_w_out, c_rpb):
    b = x.shape[0]
    nctx = b * CTX_LEN
    tm = 512
    tmc = min(tm, nctx)

    cs = jnp.concatenate([c, c_ctx[None, :], jnp.zeros((MOD_ROWS - b - 1, D_MODEL), F32)], axis=0)
    mods = _modulation(cs, mod_w, mod_b)
    ctx_row = b

    w1 = [ffn_w1[i].astype(BF16) for i in range(DEPTH)]
    w2 = [ffn_w2[i].astype(BF16) for i in range(DEPTH)]
    cflat = ctx.reshape(1, nctx, D_MODEL)

    perm = np.asarray(QA_PERM)
    w_ab = ab_w_in.reshape(D_MODEL, AB_W).astype(BF16)
    w_out = ab_w_out.reshape(D_MODEL, D_MODEL)
    w_out_ab = jnp.concatenate([w_out[:512].reshape(A_HEADS, HEAD_DIM, D_MODEL)[perm].reshape(512, D_MODEL),
                                w_out[512:]], axis=0).astype(BF16)
    gq = jnp.tile(a_q_norm[0], A_HEADS)[None, :]
    gk = jnp.tile(a_k_norm[0], A_KV_HEADS)[None, :]
    rope_tabs = tuple(jnp.asarray(t) for t in _rope_tables())
    subln = b_subln[0][None, :]
    mod0 = mods[0]

    xq = _inproj_ab(x, mod0, None, w_ab, gq, gk, rope_tabs, tm, "inproj_ab_x")
    cq = _inproj_ab(cflat, mod0, ctx_row, w_ab, gq, gk, None, tmc, "inproj_ab_ctx")
    cq = [a.reshape(b, CTX_LEN, a.shape[2]) for a in cq]
    qa_x, qb_x, ka_x, kb_x, va_x, vb_x = xq
    qa_c, qb_c, ka_c, kb_c, va_c, vb_c = cq
    c_kv = (ka_c, va_c, kb_c, vb_c)
    o_x = _attn_ab((qa_x, qb_x), (ka_x, va_x, kb_x, vb_x), c_kv, b_lambda[0], subln, 512, "attn_ab_x")
    o_c = _attn_ab((qa_c, qb_c), None, c_kv, b_lambda[0], subln, CTX_LEN, "attn_ab_ctx")

    x2 = _post([o_x], [w_out_ab], x, mod0, None, ln_g[0], ln_b[0], w1[0], w2[0], tm, "post0_x")
    c2 = _post([o_c.reshape(1, nctx, D_MODEL)], [w_out_ab], cflat, mod0, ctx_row, ln_g[0], ln_b[0], w1[0], w2[0],
               tmc, "post0_ctx")

    mod1 = mods[1]
    w_in = cd_w_in.reshape(D_MODEL, CD_W)
    ct, sn, cc, sc = _dft_tables()
    w_fab = _dft_weights(w_in, jnp.asarray(cc), jnp.asarray(sc))
    w_cd = w_in.astype(BF16)
    q_n, f_a, f_b, k_n, v_n = _inproj_cd(x2, mod1, None, w_cd, w_fab, tm, "inproj_cd_x")
    kc_n, vc_n = _inproj_cd(c2, mod1, ctx_row, w_cd, None, tmc, "inproj_cd_ctx")
    kc_n = kc_n.reshape(b, CTX_LEN, 512)
    vc_n = vc_n.reshape(b, CTX_LEN, 1024)

    o_d = _dft_time(f_a, f_b, jnp.asarray(ct).astype(BF16), jnp.asarray(sn).astype(BF16))
    cb = _bias_table(c_rpb[0])
    o_n = _natten(q_n, k_n, v_n, kc_n, vc_n, cb)

    w_out = cd_w_out.reshape(D_MODEL, D_MODEL).astype(BF16)
    return _post([o_n, o_d], [w_out[:512], w_out[512:]], x2, mod1, None, ln_g[1], ln_b[1], w1[1], w2[1], tm, "post1_x")
```

```python
import functools
import math

import numpy as np
import jax
import jax.numpy as jnp
from jax import lax
from jax.experimental import pallas as pl
from jax.experimental.pallas import tpu as pltpu

F32 = jnp.float32
BF16 = jnp.bfloat16

D_MODEL = 1024
SEQ = 2048
CTX_LEN = 256
DEPTH = 2
GRID_W = 64
GRID_H = SEQ // GRID_W
HEAD_DIM = 64
A_HEADS = 8
A_KV_HEADS = 2
B_HEADS = 4
C_HEADS = 8
WIN_H = 8
WIN_W = 16
D_GROUPS = 4
D_GROUP_DIM = 128
FFN_DIM = 4 * D_MODEL
ROPE_THETA = 10000.0
ALPHA = (2.0 * DEPTH) ** 0.25
EPS = 1e-6
ATTN_SCALE = HEAD_DIM ** -0.5
LOG2E = math.log2(math.e)
Q_SCALE = ATTN_SCALE * LOG2E
LAM_INIT0 = 0.8 - 0.6 * math.exp(-0.3 * 0)
NEG = -1e30

LANES = 128
MOD_ROWS = 16
VMEM_LIMIT = 56 * 1024 * 1024

QA_PERM = (0, 4, 1, 5, 2, 6, 3, 7)

NAT_ROWS = 4
NAT_Q = NAT_ROWS * GRID_W
NAT_WIN_ROWS = NAT_ROWS + WIN_H
NAT_WIN = NAT_WIN_ROWS * GRID_W
NAT_TILES = NAT_WIN_ROWS // 2
_NAT_DR0 = [min(max(r0 - WIN_H // 2, 0), GRID_H - NAT_WIN_ROWS) + 2 * j - (r0 + qr) + WIN_H - 1
            for r0 in range(0, GRID_H, NAT_ROWS) for qr in range(NAT_ROWS) for j in range(NAT_TILES)]
CB_OFFSET = -min(_NAT_DR0)
CB_ENTRIES = max(_NAT_DR0) + CB_OFFSET + 1


def _cparams(sem):
    return pltpu.CompilerParams(dimension_semantics=sem, vmem_limit_bytes=VMEM_LIMIT)


def _dot(a, b):
    return jnp.dot(a, b, preferred_element_type=F32)


def _dot_nt(a, b):
    return lax.dot_general(a, b, (((1,), (1,)), ((), ())), preferred_element_type=F32)


def _split3(a):
    hi = a.astype(BF16)
    r1 = a - hi.astype(F32)
    mid = r1.astype(BF16)
    lo = (r1 - mid.astype(F32)).astype(BF16)
    return hi, mid, lo


def _layer_norm(z, g, b):
    mu = jnp.mean(z, axis=-1, keepdims=True)
    zc = z - mu
    var = jnp.mean(zc * zc, axis=-1, keepdims=True)
    return zc * lax.rsqrt(var + EPS) * g + b


@functools.lru_cache(maxsize=None)
def _rope_tables():
    pos = np.arange(SEQ)
    row = (pos // GRID_W).astype(np.float64)
    col = (pos % GRID_W).astype(np.float64)
    half = HEAD_DIM // 2
    freqs = np.power(ROPE_THETA, -np.arange(0, half, 2, dtype=np.float64) / half)
    def axis_angles(p):
        a = p[:, None] * freqs[None, :]
        return np.concatenate([a, a], axis=-1)
    ang = np.concatenate([axis_angles(row), axis_angles(col)], axis=-1)
    cos, sin = np.cos(ang), np.sin(ang)
    first = (np.arange(HEAD_DIM) % half) < (half // 2)
    sin_a = np.where(first[None, :], -sin, 0.0)
    sin_b = np.where(first[None, :], 0.0, sin)
    tile = lambda t: np.tile(t, (1, LANES // HEAD_DIM)).astype(np.float32)
    return tile(cos), tile(sin_a), tile(sin_b)


@functools.lru_cache(maxsize=None)
def _group_mean_matrix(width):
    g = np.arange(width) // HEAD_DIM
    return (g[:, None] == g[None, :]).astype(np.float32) / HEAD_DIM


@functools.lru_cache(maxsize=None)
def _dft_tables():
    t = np.arange(SEQ, dtype=np.int64)
    k = (t[:, None] * t[None, :]) % SEQ
    ang = 2.0 * np.pi * k.astype(np.float64) / SEQ
    ct = np.cos(ang).astype(np.float32)
    sn = (-np.sin(ang)).astype(np.float32)
    c = np.arange(D_GROUP_DIM, dtype=np.int64)
    kc = (c[:, None] * c[None, :]) % D_GROUP_DIM
    angc = 2.0 * np.pi * kc.astype(np.float64) / D_GROUP_DIM
    norm = 1.0 / math.sqrt(SEQ * D_GROUP_DIM)
    return ct, sn, (np.cos(angc) * norm).astype(np.float32), (np.sin(angc) * norm).astype(np.float32)


def _mod_kernel(c_ref, w_ref, b_ref, o_ref):
    c = c_ref[...]
    a = c / (1.0 + jnp.exp(-c))
    a_hi, a_mid, _ = _split3(a)
    w = w_ref[0]
    w_hi, w_mid, _ = _split3(w)
    o_ref[0, 0] = _dot(a_hi, w_hi) + (_dot(a_mid, w_hi) + _dot(a_hi, w_mid)) + b_ref[0]


def _modulation(cs, mod_w, mod_b):
    n = 6 * D_MODEL
    return pl.pallas_call(
        _mod_kernel,
        grid=(DEPTH, 6),
        in_specs=[pl.BlockSpec((MOD_ROWS, D_MODEL), lambda l, j: (0, 0)),
                  pl.BlockSpec((1, D_MODEL, D_MODEL), lambda l, j: (l, 0, j)),
                  pl.BlockSpec((1, 1, D_MODEL), lambda l, j: (l, 0, j))],
        out_specs=pl.BlockSpec((1, 1, MOD_ROWS, D_MODEL), lambda l, j: (l, j, 0, 0)),
        out_shape=jax.ShapeDtypeStruct((DEPTH, 6, MOD_ROWS, D_MODEL), F32),
        compiler_params=_cparams(("arbitrary", "arbitrary")),
        name="modulation",
    )(cs, mod_w, mod_b.reshape(DEPTH, 1, n))


def _tok_spec(tm, width):
    return pl.BlockSpec((1, tm, width), lambda g, i: (g, i, 0))


MOD_SPEC = pl.BlockSpec((6, MOD_ROWS, D_MODEL), lambda g, i: (0, 0, 0))


def _mod_vectors(mod_ref, mod_row):
    r = pl.program_id(0) if mod_row is None else mod_row
    return [mod_ref[k, pl.ds(r, 1), :] for k in range(6)]


def _const_spec(shape):
    nd = len(shape)
    return pl.BlockSpec(shape, lambda g, i: (0,) * nd)


AB_QA, AB_QB, AB_KA, AB_VA, AB_KB, AB_VB, AB_W = 0, 512, 1024, 1152, 1280, 1792, 2304


TOKEN_CHAIN_ROWS = 256


def _store_v_ext(o_ref, rows, v):
    ones = jnp.ones((v.shape[0], LANES), o_ref.dtype)
    for j in range(v.shape[1] // LANES):
        o_ref[0, rows, 2 * j * LANES:(2 * j + 1) * LANES] = v[:, j * LANES:(j + 1) * LANES].astype(o_ref.dtype)
        o_ref[0, rows, (2 * j + 1) * LANES:(2 * j + 2) * LANES] = ones


def _rope_chunk(c, cos, sin_a, sin_b):
    return c * cos + pltpu.roll(c, LANES - 16, axis=1) * sin_a + pltpu.roll(c, 16, axis=1) * sin_b


def _inproj_ab_kernel(*refs, rope, mod_row):
    if rope:
        (x_ref, mod_ref, w_ref, gq_ref, gk_ref, g512_ref, g128_ref, cos_ref, sa_ref, sb_ref,
         qa_ref, qb_ref, ka_ref, kb_ref, va_ref, vb_ref) = refs
    else:
        (x_ref, mod_ref, w_ref, gq_ref, gk_ref, g512_ref, g128_ref,
         qa_ref, qb_ref, ka_ref, kb_ref, va_ref, vb_ref) = refs

    def rms(v, g_ref, gm_ref):
        ms = _dot((v * v).astype(BF16), gm_ref[...])
        return v * lax.rsqrt(ms + EPS) * g_ref[...]

    shift, scale = _mod_vectors(mod_ref, mod_row)[0:2]
    lo = lax.broadcasted_iota(jnp.int32, (1, LANES), 1) < HEAD_DIM
    tm = x_ref.shape[1]
    tc = min(TOKEN_CHAIN_ROWS, tm)
    for rb in range(tm // tc):
        rows = slice(rb * tc, (rb + 1) * tc)
        if rope:
            cos, sin_a, sin_b = cos_ref[rows, :], sa_ref[rows, :], sb_ref[rows, :]
        u = (x_ref[0, rows, :] * (1.0 + scale) + shift).astype(BF16)
        h = _dot(u, w_ref[...])

        def chunks(v, q_scale):
            out = []
            for j in range(v.shape[1] // LANES):
                c = v[:, j * LANES:(j + 1) * LANES]
                if rope:
                    c = _rope_chunk(c, cos, sin_a, sin_b)
                out.append(c * q_scale if q_scale != 1.0 else c)
            return out

        def store(o_ref, cs):
            for j, c in enumerate(cs):
                o_ref[0, rows, j * LANES:(j + 1) * LANES] = c.astype(o_ref.dtype)

        qa = chunks(rms(h[:, AB_QA:AB_QB], gq_ref, g512_ref), Q_SCALE)
        qa_perm = []
        for p in range(A_HEADS // 2):
            a, b = qa[p // 2], qa[A_HEADS // 4 + p // 2]
            if p % 2 == 0:
                qa_perm.append(jnp.where(lo, a, pltpu.roll(b, HEAD_DIM, axis=1)))
            else:
                qa_perm.append(jnp.where(lo, pltpu.roll(a, HEAD_DIM, axis=1), b))
        store(qa_ref, qa_perm)
        store(qb_ref, chunks(h[:, AB_QB:AB_KA], Q_SCALE))
        store(ka_ref, chunks(rms(h[:, AB_KA:AB_VA], gk_ref, g128_ref), 1.0))
        store(kb_ref, chunks(h[:, AB_KB:AB_VB], 1.0))
        _store_v_ext(va_ref, rows, h[:, AB_VA:AB_KB])
        _store_v_ext(vb_ref, rows, h[:, AB_VB:AB_W])


def _inproj_ab(tok, mod, mod_row, w, gq, gk, rope_tabs, tm, name):
    g, r, _ = tok.shape
    rope = rope_tabs is not None
    consts = [w, gq, gk, jnp.asarray(_group_mean_matrix(512), BF16), jnp.asarray(_group_mean_matrix(128), BF16)]
    in_specs = [_tok_spec(tm, D_MODEL), MOD_SPEC] + [_const_spec(c.shape) for c in consts]
    args = [tok, mod] + consts
    if rope:
        in_specs += [pl.BlockSpec((tm, LANES), lambda gg, i: (i, 0))] * 3
        args += list(rope_tabs)
    widths = (512, 512, 128, 512, 2 * 128, 2 * 512)
    return pl.pallas_call(
        functools.partial(_inproj_ab_kernel, rope=rope, mod_row=mod_row),
        grid=(g, r // tm),
        in_specs=in_specs,
        out_specs=[_tok_spec(tm, wd) for wd in widths],
        out_shape=[jax.ShapeDtypeStruct((g, r, wd), BF16) for wd in widths],
        compiler_params=_cparams(("parallel", "parallel")),
        name=name,
    )(*args)


def _softmax_pv(q2, kv_pairs, bias=None):
    scores = [_dot_nt(q2, k) for k, _ in kv_pairs]
    if bias is not None:
        scores[0] = scores[0] + bias
    m = scores[0].max(axis=-1, keepdims=True)
    for s in scores[1:]:
        m = jnp.maximum(m, s.max(axis=-1, keepdims=True))
    acc = None
    for s, (_, v) in zip(scores, kv_pairs):
        a = _dot(jnp.exp2(s - m).astype(BF16), v)
        acc = a if acc is None else acc + a
    return acc[:, :LANES] / acc[:, LANES:]


def _split_pair(qp, lo):
    zero = jnp.zeros_like(qp)
    return jnp.concatenate([jnp.where(lo, qp, zero), jnp.where(lo, zero, qp)], axis=0)


ATTN_CHAIN_ROWS = 256


def _attn_ab_kernel(*refs, with_x):
    if with_x:
        (qa_ref, qb_ref, kax_ref, vax_ref, kbx_ref, vbx_ref, kac_ref, vac_ref, kbc_ref, vbc_ref,
         lam_ref, subln_ref, o_ref) = refs
    else:
        (qa_ref, qb_ref, kac_ref, vac_ref, kbc_ref, vbc_ref, lam_ref, subln_ref, o_ref) = refs
    tc = ATTN_CHAIN_ROWS
    lo = lax.broadcasted_iota(jnp.int32, (1, LANES), 1) < HEAD_DIM

    lp = lam_ref[...]
    lam = (jnp.exp(jnp.sum(lp[0:1] * lp[1:2], axis=-1, keepdims=True))
           - jnp.exp(jnp.sum(lp[2:3] * lp[3:4], axis=-1, keepdims=True)) + LAM_INIT0)

    for rb in range(qa_ref.shape[1] // tc):
        rows = slice(rb * tc, (rb + 1) * tc)
        for p in range(A_HEADS // 2):
            sl = slice(p * LANES, (p + 1) * LANES)
            q2 = _split_pair(qa_ref[0, rows, sl], lo)
            kv = [(kac_ref[0], vac_ref[0])]
            if with_x:
                kv.append((kax_ref[0], vax_ref[0]))
            o2 = _softmax_pv(q2, kv)
            o_ref[0, rows, sl] = jnp.where(lo, o2[:tc], o2[tc:]).astype(o_ref.dtype)

        for i in range(B_HEADS):
            sl = slice(i * LANES, (i + 1) * LANES)
            sl2 = slice(2 * i * LANES, (2 * i + 2) * LANES)
            q2 = _split_pair(qb_ref[0, rows, sl], lo)
            kv = [(kbc_ref[0, :, sl], vbc_ref[0, :, sl2])]
            if with_x:
                kv.append((kbx_ref[0, :, sl], vbx_ref[0, :, sl2]))
            o2 = _softmax_pv(q2, kv)
            o = o2[:tc] - lam * o2[tc:]
            ms = jnp.mean(o * o, axis=-1, keepdims=True)
            o = o * lax.rsqrt(ms + EPS) * subln_ref[...] * (1.0 - LAM_INIT0)
            o_ref[0, rows, 512 + i * LANES:512 + (i + 1) * LANES] = o.astype(o_ref.dtype)


def _attn_ab(q_parts, x_kv, c_kv, lam_p, subln, tq, name):
    qa, qb = q_parts
    b, nq, _ = qa.shape
    with_x = x_kv is not None

    def full(a):
        return pl.BlockSpec((1,) + a.shape[1:], lambda bb, i: (bb, 0, 0))

    args = [qa, qb]
    in_specs = [pl.BlockSpec((1, tq, 512), lambda bb, i: (bb, i, 0))] * 2
    if with_x:
        args += list(x_kv)
        in_specs += [full(a) for a in x_kv]
    args += list(c_kv) + [lam_p, subln]
    in_specs += [full(a) for a in c_kv] + [pl.BlockSpec(lam_p.shape, lambda bb, i: (0, 0)),
                                           pl.BlockSpec(subln.shape, lambda bb, i: (0, 0))]
    return pl.pallas_call(
        functools.partial(_attn_ab_kernel, with_x=with_x),
        grid=(b, nq // tq),
        in_specs=in_specs,
        out_specs=pl.BlockSpec((1, tq, D_MODEL), lambda bb, i: (bb, i, 0)),
        out_shape=jax.ShapeDtypeStruct((b, nq, D_MODEL), BF16),
        compiler_params=_cparams(("parallel", "parallel")),
        name=name,
    )(*args)


FFN_CHUNK = 1024


def _post_kernel(*refs, n_parts, mod_row):
    o_refs = refs[:n_parts]
    wo_refs = refs[n_parts:2 * n_parts]
    x_ref, mod_ref, g_ref, b_ref, w1_ref, w2_ref, y_ref = refs[2 * n_parts:]
    _, _, gate1, shift2, scale2, gate2 = _mod_vectors(mod_ref, mod_row)
    y = _dot(o_refs[0][0], wo_refs[0][...])
    for o_ref, w_ref in zip(o_refs[1:], wo_refs[1:]):
        y = y + _dot(o_ref[0], w_ref[...])
    x1 = _layer_norm(ALPHA * x_ref[0] + gate1 * y, g_ref[0:1, :], b_ref[0:1, :])
    u = (x1 * (1.0 + scale2) + shift2).astype(BF16)
    acc = None
    for c in range(FFN_DIM // FFN_CHUNK):
        sl = slice(c * FFN_CHUNK, (c + 1) * FFN_CHUNK)
        h = jnp.maximum(_dot(u, w1_ref[0, :, sl]), 0.0)
        a = _dot((h * h).astype(BF16), w2_ref[0, sl, :])
        acc = a if acc is None else acc + a
    y_ref[0] = _layer_norm(ALPHA * x1 + gate2 * acc, g_ref[1:2, :], b_ref[1:2, :])


def _post(o_parts, wo_parts, resid, mod, mod_row, ln_g, ln_b, w1, w2, layer, tm, name):
    g, r, _ = resid.shape
    n = len(o_parts)
    single = pl.Buffered(1)
    layer_w = lambda a: pl.BlockSpec((1,) + a.shape[1:], lambda gg, i: (layer, 0, 0), pipeline_mode=single)
    const1 = lambda a: pl.BlockSpec(a.shape, lambda gg, i: (0,) * a.ndim, pipeline_mode=single)
    in_specs = ([_tok_spec(tm, o.shape[2]) for o in o_parts] + [const1(w) for w in wo_parts]
                + [_tok_spec(tm, D_MODEL), MOD_SPEC, _const_spec(ln_g.shape), _const_spec(ln_b.shape),
                   layer_w(w1), layer_w(w2)])
    return pl.pallas_call(
        functools.partial(_post_kernel, n_parts=n, mod_row=mod_row),
        grid=(g, r // tm),
        in_specs=in_specs,
        out_specs=_tok_spec(tm, D_MODEL),
        out_shape=jax.ShapeDtypeStruct((g, r, D_MODEL), F32),
        compiler_params=_cparams(("parallel", "parallel")),
        name=name,
    )(*o_parts, *wo_parts, resid, mod, ln_g, ln_b, w1, w2)


CD_Q, CD_F, CD_K, CD_V, CD_W = 0, 512, 1024, 1536, 2048


def _inproj_cd_kernel(*refs, with_q, mod_row):
    if with_q:
        x_ref, mod_ref, w_ref, wa_ref, wb_ref, q_ref, fa_ref, fb_ref, k_ref, v_ref = refs
    else:
        x_ref, mod_ref, w_ref, k_ref, v_ref = refs
    shift, scale = _mod_vectors(mod_ref, mod_row)[0:2]
    u = (x_ref[0] * (1.0 + scale) + shift).astype(BF16)
    if with_q:
        q_ref[0] = (_dot(u, w_ref[:, CD_Q:CD_F]) * Q_SCALE).astype(q_ref.dtype)
        fa_ref[0] = _dot(u, wa_ref[...]).astype(fa_ref.dtype)
        fb_ref[0] = _dot(u, wb_ref[...]).astype(fb_ref.dtype)
    kv = _dot(u, w_ref[:, CD_K:CD_W])
    k_ref[0] = kv[:, :CD_V - CD_K].astype(k_ref.dtype)
    _store_v_ext(v_ref, slice(None), kv[:, CD_V - CD_K:])


def _inproj_cd(tok, mod, mod_row, w, w_ab, tm, name):
    g, r, _ = tok.shape
    with_q = w_ab is not None
    weights = [w] + (list(w_ab) if with_q else [])
    widths = ((512, 512, 512) if with_q else ()) + (512, 1024)
    return pl.pallas_call(
        functools.partial(_inproj_cd_kernel, with_q=with_q, mod_row=mod_row),
        grid=(g, r // tm),
        in_specs=[_tok_spec(tm, D_MODEL), MOD_SPEC] + [_const_spec(a.shape) for a in weights],
        out_specs=[_tok_spec(tm, wd) for wd in widths],
        out_shape=[jax.ShapeDtypeStruct((g, r, wd), BF16) for wd in widths],
        compiler_params=_cparams(("parallel", "parallel")),
        name=name,
    )(tok, mod, *weights)


def _dft_weight_kernel(w_ref, cc_ref, sc_ref, wa_ref, wb_ref):
    c_hi, c_mid, _ = _split3(cc_ref[...])
    s_hi, s_mid, _ = _split3(sc_ref[...])
    for gi in range(D_GROUPS):
        sl = slice(gi * D_GROUP_DIM, (gi + 1) * D_GROUP_DIM)
        w_hi, w_mid, _ = _split3(w_ref[:, sl])
        wa_ref[:, sl] = (_dot(w_hi, c_hi) + (_dot(w_mid, c_hi) + _dot(w_hi, c_mid))).astype(BF16)
        wb_ref[:, sl] = (_dot(w_hi, s_hi) + (_dot(w_mid, s_hi) + _dot(w_hi, s_mid))).astype(BF16)


def _dft_weights(w_in, cc, sc):
    width = CD_K - CD_F
    out = pl.BlockSpec((D_MODEL, width), lambda i: (0, 0))
    return pl.pallas_call(
        _dft_weight_kernel,
        grid=(1,),
        in_specs=[pl.BlockSpec((D_MODEL, width), lambda i: (0, CD_F // width)),
                  pl.BlockSpec(cc.shape, lambda i: (0, 0)), pl.BlockSpec(sc.shape, lambda i: (0, 0))],
        out_specs=[out, out],
        out_shape=[jax.ShapeDtypeStruct((D_MODEL, width), BF16)] * 2,
        compiler_params=_cparams(("arbitrary",)),
        name="dft_weights",
    )(w_in, cc, sc)


def _bias_table_kernel(r_ref, o_ref):
    n = GRID_W * LANES
    row = lax.broadcasted_iota(jnp.int32, (2 * 32, n), 0)
    lane = lax.broadcasted_iota(jnp.int32, (2 * 32, n), 1)
    e_row, i_row = row >> 5, row & 31
    c = lane >> 7
    e_lane = (lane >> 6) & 1
    kc = lane & (GRID_W - 1)
    sel = jnp.where((e_row == e_lane) & (kc - c + (WIN_W - 1) == i_row), 1.0, 0.0).astype(BF16)
    hi, mid, lo = _split3(r_ref[...])
    t = _dot(hi, sel) + _dot(mid, sel) + _dot(lo, sel)
    c1 = c[0:1]
    kc1 = kc[0:1]
    cstart = jnp.clip(c1 - WIN_W // 2, 0, GRID_W - WIN_W)
    col_ok = (kc1 >= cstart) & (kc1 < cstart + WIN_W)
    o_ref[...] = jnp.where(col_ok, t * LOG2E, NEG)


def _bias_table(rpb):
    n_dr = 2 * WIN_H - 1
    p = jnp.pad(rpb, ((0, 0), (CB_OFFSET, CB_ENTRIES + 1 - n_dr - CB_OFFSET), (0, 1)))
    r2 = jnp.stack([p[:, 0:CB_ENTRIES], p[:, 1:CB_ENTRIES + 1]], axis=2)
    r2 = r2.reshape(C_HEADS * CB_ENTRIES, 2 * 32)
    t = pl.pallas_call(
        _bias_table_kernel,
        out_shape=jax.ShapeDtypeStruct((C_HEADS * CB_ENTRIES, GRID_W * LANES), F32),
        compiler_params=pltpu.CompilerParams(vmem_limit_bytes=VMEM_LIMIT),
        name="natten_bias_table",
    )(r2)
    return t.reshape(C_HEADS, CB_ENTRIES, GRID_W, LANES)


NAT_BLOCKS = 2


def _natten_block(blk, q_ref, k_ref, v_ref, kc_ref, vc_ref, cb_ref, o_ref, lo, e):
    rows = slice(blk * NAT_Q, (blk + 1) * NAT_Q)
    r0 = NAT_ROWS * (NAT_BLOCKS * pl.program_id(1) + blk)
    ws = jnp.clip(r0 - WIN_H // 2, 0, GRID_H - NAT_WIN_ROWS)
    koff = pl.multiple_of(ws * GRID_W, LANES)

    entry = []
    rmask = []
    for qr in range(NAT_ROWS):
        r = r0 + qr
        lo_r = jnp.clip(r - WIN_H // 2, 0, GRID_H - WIN_H) - r + (WIN_H - 1)
        ent_q, mask_q = [], []
        for j in range(NAT_TILES):
            dr0 = ws + 2 * j - r + (WIN_H - 1)
            ent_q.append(dr0 + CB_OFFSET)
            dr = dr0 + e
            ok = (dr >= lo_r) & (dr < lo_r + WIN_H)
            mask_q.append(jnp.where(ok, 0.0, NEG))
        entry.append(ent_q)
        rmask.append(mask_q)

    for p in range(C_HEADS // 2):
        sl = slice(p * LANES, (p + 1) * LANES)
        sl2 = slice(2 * p * LANES, (2 * p + 2) * LANES)
        q2 = _split_pair(q_ref[0, rows, sl], lo)
        bias = jnp.concatenate(
            [jnp.concatenate([cb_ref[2 * p + hh, entry[qr][j]] + rmask[qr][j] for j in range(NAT_TILES)], axis=1)
             for hh in range(2) for qr in range(NAT_ROWS)], axis=0)
        kv = [(k_ref[0, pl.ds(koff, NAT_WIN), sl], v_ref[0, pl.ds(koff, NAT_WIN), sl2]),
              (kc_ref[0, :, sl], vc_ref[0, :, sl2])]
        o2 = _softmax_pv(q2, kv, bias)
        o_ref[0, rows, sl] = jnp.where(lo, o2[:NAT_Q], o2[NAT_Q:]).astype(o_ref.dtype)


def _natten_dft_kernel(q_ref, k_ref, v_ref, kc_ref, vc_ref, cb_ref, fa_ref, fb_ref, ct_ref, sn_ref, o_ref, od_ref):
    od_ref[0] = (_dot(ct_ref[...], fa_ref[0]) + _dot(sn_ref[...], fb_ref[0])).astype(od_ref.dtype)
    lane = lax.broadcasted_iota(jnp.int32, (1, LANES), 1)
    lo = lane < HEAD_DIM
    e = jnp.where(lo, 0, 1)
    for blk in range(NAT_BLOCKS):
        _natten_block(blk, q_ref, k_ref, v_ref, kc_ref, vc_ref, cb_ref, o_ref, lo, e)


def _natten_dft(q, k, v, kc, vc, cb, fa, fb, ct, sn):
    b = q.shape[0]
    tq = NAT_BLOCKS * NAT_Q
    qspec = pl.BlockSpec((1, tq, 512), lambda bb, i: (bb, i, 0))
    full = lambda a: pl.BlockSpec((1,) + a.shape[1:], lambda bb, i: (bb, 0, 0))
    table = pl.BlockSpec((tq, SEQ), lambda bb, i: (i, 0))
    return pl.pallas_call(
        _natten_dft_kernel,
        grid=(b, SEQ // tq),
        in_specs=[qspec, full(k), full(v), full(kc), full(vc),
                  pl.BlockSpec(cb.shape, lambda bb, i: (0, 0, 0, 0), pipeline_mode=pl.Buffered(1)),
                  full(fa), full(fb), table, table],
        out_specs=[qspec, qspec],
        out_shape=[jax.ShapeDtypeStruct((b, SEQ, 512), BF16)] * 2,
        compiler_params=_cparams(("parallel", "parallel")),
        name="natten_dft",
    )(q, k, v, kc, vc, cb, fa, fb, ct, sn)


def kernel(x, c, ctx, c_ctx, mod_w, mod_b, ln_g, ln_b, ffn_w1, ffn_w2, ab_w_in, ab_w_out, a_q_norm, a_k_norm,
           b_lambda, b_subln, cd_w_in, cd_w_out, c_rpb):
    b = x.shape[0]
    nctx = b * CTX_LEN
    tm = 512
    tmc = min(tm, nctx)

    cs = jnp.concatenate([c, c_ctx[None, :], jnp.zeros((MOD_ROWS - b - 1, D_MODEL), F32)], axis=0)
    mods = _modulation(cs, mod_w, mod_b)
    ctx_row = b

    w1 = ffn_w1.astype(BF16)
    w2 = ffn_w2.astype(BF16)
    cflat = ctx.reshape(1, nctx, D_MODEL)

    perm = np.asarray(QA_PERM)
    w_ab = ab_w_in.reshape(D_MODEL, AB_W).astype(BF16)
    w_out = ab_w_out.reshape(D_MODEL, D_MODEL)
    w_out_ab = jnp.concatenate([w_out[:512].reshape(A_HEADS, HEAD_DIM, D_MODEL)[perm].reshape(512, D_MODEL),
                                w_out[512:]], axis=0).astype(BF16)
    gq = jnp.tile(a_q_norm[0], A_HEADS)[None, :]
    gk = jnp.tile(a_k_norm[0], A_KV_HEADS)[None, :]
    rope_tabs = tuple(jnp.asarray(t) for t in _rope_tables())
    subln = b_subln[0][None, :]
    mod0 = mods[0]

    xq = _inproj_ab(x, mod0, None, w_ab, gq, gk, rope_tabs, 2 * tm, "inproj_ab_x")
    cq = _inproj_ab(cflat, mod0, ctx_row, w_ab, gq, gk, None, tmc, "inproj_ab_ctx")
    cq = [a.reshape(b, CTX_LEN, a.shape[2]) for a in cq]
    qa_x, qb_x, ka_x, kb_x, va_x, vb_x = xq
    qa_c, qb_c, ka_c, kb_c, va_c, vb_c = cq
    c_kv = (ka_c, va_c, kb_c, vb_c)
    o_x = _attn_ab((qa_x, qb_x), (ka_x, va_x, kb_x, vb_x), c_kv, b_lambda[0], subln, 512, "attn_ab_x")
    o_c = _attn_ab((qa_c, qb_c), None, c_kv, b_lambda[0], subln, CTX_LEN, "attn_ab_ctx")

    x2 = _post([o_x], [w_out_ab], x, mod0, None, ln_g[0], ln_b[0], w1, w2, 0, tm, "post0_x")
    c2 = _post([o_c.reshape(1, nctx, D_MODEL)], [w_out_ab], cflat, mod0, ctx_row, ln_g[0], ln_b[0], w1, w2, 0,
               tmc, "post0_ctx")

    mod1 = mods[1]
    w_in = cd_w_in.reshape(D_MODEL, CD_W)
    ct, sn, cc, sc = _dft_tables()
    w_fab = _dft_weights(w_in, jnp.asarray(cc), jnp.asarray(sc))
    w_cd = w_in.astype(BF16)
    q_n, f_a, f_b, k_n, v_n = _inproj_cd(x2, mod1, None, w_cd, w_fab, 2 * tm, "inproj_cd_x")
    kc_n, vc_n = _inproj_cd(c2, mod1, ctx_row, w_cd, None, tmc, "inproj_cd_ctx")
    kc_n = kc_n.reshape(b, CTX_LEN, 512)
    vc_n = vc_n.reshape(b, CTX_LEN, 1024)

    cb = _bias_table(c_rpb[0])
    o_n, o_d = _natten_dft(q_n, k_n, v_n, kc_n, vc_n, cb, f_a, f_b,
                           jnp.asarray(ct).astype(BF16), jnp.asarray(sn).astype(BF16))

    w_out = cd_w_out.reshape(D_MODEL, D_MODEL).astype(BF16)
    return _post([o_n, o_d], [w_out[:512], w_out[512:]], x2, mod1, None, ln_g[1], ln_b[1], w1, w2, 1, tm, "post1_x")
```

```python
import functools
import math

import numpy as np
import jax
import jax.numpy as jnp
from jax import lax
from jax.experimental import pallas as pl
from jax.experimental.pallas import tpu as pltpu

F32 = jnp.float32
BF16 = jnp.bfloat16

D_MODEL = 1024
SEQ = 2048
CTX_LEN = 256
DEPTH = 2
GRID_W = 64
GRID_H = SEQ // GRID_W
HEAD_DIM = 64
A_HEADS = 8
A_KV_HEADS = 2
B_HEADS = 4
C_HEADS = 8
WIN_H = 8
WIN_W = 16
D_GROUPS = 4
D_GROUP_DIM = 128
FFN_DIM = 4 * D_MODEL
ROPE_THETA = 10000.0
ALPHA = (2.0 * DEPTH) ** 0.25
EPS = 1e-6
ATTN_SCALE = HEAD_DIM ** -0.5
LOG2E = math.log2(math.e)
Q_SCALE = ATTN_SCALE * LOG2E
LAM_INIT0 = 0.8 - 0.6 * math.exp(-0.3 * 0)
NEG = -1e30

LANES = 128
MOD_ROWS = 16
VMEM_LIMIT = 56 * 1024 * 1024

QA_PERM = (0, 4, 1, 5, 2, 6, 3, 7)

NAT_ROWS = 4
NAT_Q = NAT_ROWS * GRID_W
NAT_WIN_ROWS = NAT_ROWS + WIN_H
NAT_WIN = NAT_WIN_ROWS * GRID_W
NAT_TILES = NAT_WIN_ROWS // 2
_NAT_DR0 = [min(max(r0 - WIN_H // 2, 0), GRID_H - NAT_WIN_ROWS) + 2 * j - (r0 + qr) + WIN_H - 1
            for r0 in range(0, GRID_H, NAT_ROWS) for qr in range(NAT_ROWS) for j in range(NAT_TILES)]
CB_OFFSET = -min(_NAT_DR0)
CB_ENTRIES = max(_NAT_DR0) + CB_OFFSET + 1


def _cparams(sem):
    return pltpu.CompilerParams(dimension_semantics=sem, vmem_limit_bytes=VMEM_LIMIT)


def _dot(a, b):
    return jnp.dot(a, b, preferred_element_type=F32)


def _dot_nt(a, b):
    return lax.dot_general(a, b, (((1,), (1,)), ((), ())), preferred_element_type=F32)


def _split3(a):
    hi = a.astype(BF16)
    r1 = a - hi.astype(F32)
    mid = r1.astype(BF16)
    lo = (r1 - mid.astype(F32)).astype(BF16)
    return hi, mid, lo


def _layer_norm(z, g, b):
    mu = jnp.mean(z, axis=-1, keepdims=True)
    zc = z - mu
    var = jnp.mean(zc * zc, axis=-1, keepdims=True)
    return zc * lax.rsqrt(var + EPS) * g + b


@functools.lru_cache(maxsize=None)
def _rope_tables():
    pos = np.arange(SEQ)
    row = (pos // GRID_W).astype(np.float64)
    col = (pos % GRID_W).astype(np.float64)
    half = HEAD_DIM // 2
    freqs = np.power(ROPE_THETA, -np.arange(0, half, 2, dtype=np.float64) / half)
    def axis_angles(p):
        a = p[:, None] * freqs[None, :]
        return np.concatenate([a, a], axis=-1)
    ang = np.concatenate([axis_angles(row), axis_angles(col)], axis=-1)
    cos, sin = np.cos(ang), np.sin(ang)
    first = (np.arange(HEAD_DIM) % half) < (half // 2)
    sin_a = np.where(first[None, :], -sin, 0.0)
    sin_b = np.where(first[None, :], 0.0, sin)
    tile = lambda t: np.tile(t, (1, LANES // HEAD_DIM)).astype(np.float32)
    return tile(cos), tile(sin_a), tile(sin_b)


@functools.lru_cache(maxsize=None)
def _group_mean_matrix(width):
    g = np.arange(width) // HEAD_DIM
    return (g[:, None] == g[None, :]).astype(np.float32) / HEAD_DIM


@functools.lru_cache(maxsize=None)
def _dft_tables():
    t = np.arange(SEQ, dtype=np.int64)
    k = (t[:, None] * t[None, :]) % SEQ
    ang = 2.0 * np.pi * k.astype(np.float64) / SEQ
    ct = np.cos(ang).astype(np.float32)
    sn = (-np.sin(ang)).astype(np.float32)
    c = np.arange(D_GROUP_DIM, dtype=np.int64)
    kc = (c[:, None] * c[None, :]) % D_GROUP_DIM
    angc = 2.0 * np.pi * kc.astype(np.float64) / D_GROUP_DIM
    norm = 1.0 / math.sqrt(SEQ * D_GROUP_DIM)
    return ct, sn, (np.cos(angc) * norm).astype(np.float32), (np.sin(angc) * norm).astype(np.float32)


def _mod_kernel(c_ref, w_ref, b_ref, o_ref):
    c = c_ref[...]
    a = c / (1.0 + jnp.exp(-c))
    a_hi, a_mid, _ = _split3(a)
    y = _dot(jnp.concatenate([a_hi, a_mid], axis=0), w_ref[0].astype(BF16))
    o_ref[0, 0] = y[:MOD_ROWS] + y[MOD_ROWS:] + b_ref[0]


def _modulation(cs, mod_w, mod_b):
    n = 6 * D_MODEL
    return pl.pallas_call(
        _mod_kernel,
        grid=(DEPTH, 6),
        in_specs=[pl.BlockSpec((MOD_ROWS, D_MODEL), lambda l, j: (0, 0)),
                  pl.BlockSpec((1, D_MODEL, D_MODEL), lambda l, j: (l, 0, j)),
                  pl.BlockSpec((1, 1, D_MODEL), lambda l, j: (l, 0, j))],
        out_specs=pl.BlockSpec((1, 1, MOD_ROWS, D_MODEL), lambda l, j: (l, j, 0, 0)),
        out_shape=jax.ShapeDtypeStruct((DEPTH, 6, MOD_ROWS, D_MODEL), F32),
        compiler_params=_cparams(("arbitrary", "arbitrary")),
        name="modulation",
    )(cs, mod_w, mod_b.reshape(DEPTH, 1, n))


def _tok_spec(tm, width):
    return pl.BlockSpec((1, tm, width), lambda g, i: (g, i, 0))


MOD_SPEC = pl.BlockSpec((6, MOD_ROWS, D_MODEL), lambda g, i: (0, 0, 0))


def _mod_vectors(mod_ref, mod_row):
    r = pl.program_id(0) if mod_row is None else mod_row
    return [mod_ref[k, pl.ds(r, 1), :] for k in range(6)]


def _const_spec(shape):
    nd = len(shape)
    return pl.BlockSpec(shape, lambda g, i: (0,) * nd)


AB_QA, AB_QB, AB_KA, AB_VA, AB_KB, AB_VB, AB_W = 0, 512, 1024, 1152, 1280, 1792, 2304


TOKEN_CHAIN_ROWS = 256


def _store_v_ext(o_ref, rows, v):
    ones = jnp.ones((v.shape[0], LANES), o_ref.dtype)
    for j in range(v.shape[1] // LANES):
        o_ref[0, rows, 2 * j * LANES:(2 * j + 1) * LANES] = v[:, j * LANES:(j + 1) * LANES].astype(o_ref.dtype)
        o_ref[0, rows, (2 * j + 1) * LANES:(2 * j + 2) * LANES] = ones


def _rope_chunk(c, cos, sin_a, sin_b):
    return c * cos + pltpu.roll(c, LANES - 16, axis=1) * sin_a + pltpu.roll(c, 16, axis=1) * sin_b


def _inproj_ab_kernel(*refs, rope, mod_row):
    if rope:
        (x_ref, mod_ref, w_ref, gq_ref, gk_ref, g512_ref, g128_ref, cos_ref, sa_ref, sb_ref,
         qa_ref, qb_ref, ka_ref, kb_ref, va_ref, vb_ref) = refs
    else:
        (x_ref, mod_ref, w_ref, gq_ref, gk_ref, g512_ref, g128_ref,
         qa_ref, qb_ref, ka_ref, kb_ref, va_ref, vb_ref) = refs

    def rms(v, g_ref, gm_ref):
        ms = _dot((v * v).astype(BF16), gm_ref[...])
        return v * lax.rsqrt(ms + EPS) * g_ref[...]

    shift, scale = _mod_vectors(mod_ref, mod_row)[0:2]
    lo = lax.broadcasted_iota(jnp.int32, (1, LANES), 1) < HEAD_DIM
    tm = x_ref.shape[1]
    tc = min(TOKEN_CHAIN_ROWS, tm)
    for rb in range(tm // tc):
        rows = slice(rb * tc, (rb + 1) * tc)
        if rope:
            cos, sin_a, sin_b = cos_ref[rows, :], sa_ref[rows, :], sb_ref[rows, :]
        u = (x_ref[0, rows, :] * (1.0 + scale) + shift).astype(BF16)
        h = _dot(u, w_ref[...])

        def chunks(v, q_scale):
            out = []
            for j in range(v.shape[1] // LANES):
                c = v[:, j * LANES:(j + 1) * LANES]
                if rope:
                    c = _rope_chunk(c, cos, sin_a, sin_b)
                out.append(c * q_scale if q_scale != 1.0 else c)
            return out

        def store(o_ref, cs):
            for j, c in enumerate(cs):
                o_ref[0, rows, j * LANES:(j + 1) * LANES] = c.astype(o_ref.dtype)

        qa = chunks(rms(h[:, AB_QA:AB_QB], gq_ref, g512_ref), Q_SCALE)
        qa_perm = []
        for p in range(A_HEADS // 2):
            a, b = qa[p // 2], qa[A_HEADS // 4 + p // 2]
            if p % 2 == 0:
                qa_perm.append(jnp.where(lo, a, pltpu.roll(b, HEAD_DIM, axis=1)))
            else:
                qa_perm.append(jnp.where(lo, pltpu.roll(a, HEAD_DIM, axis=1), b))
        store(qa_ref, qa_perm)
        store(qb_ref, chunks(h[:, AB_QB:AB_KA], Q_SCALE))
        store(ka_ref, chunks(rms(h[:, AB_KA:AB_VA], gk_ref, g128_ref), 1.0))
        store(kb_ref, chunks(h[:, AB_KB:AB_VB], 1.0))
        _store_v_ext(va_ref, rows, h[:, AB_VA:AB_KB])
        _store_v_ext(vb_ref, rows, h[:, AB_VB:AB_W])


def _inproj_ab(tok, mod, mod_row, w, gq, gk, rope_tabs, tm, name):
    g, r, _ = tok.shape
    rope = rope_tabs is not None
    consts = [w, gq, gk, jnp.asarray(_group_mean_matrix(512), BF16), jnp.asarray(_group_mean_matrix(128), BF16)]
    in_specs = [_tok_spec(tm, D_MODEL), MOD_SPEC] + [_const_spec(c.shape) for c in consts]
    args = [tok, mod] + consts
    if rope:
        in_specs += [pl.BlockSpec((tm, LANES), lambda gg, i: (i, 0))] * 3
        args += list(rope_tabs)
    widths = (512, 512, 128, 512, 2 * 128, 2 * 512)
    return pl.pallas_call(
        functools.partial(_inproj_ab_kernel, rope=rope, mod_row=mod_row),
        grid=(g, r // tm),
        in_specs=in_specs,
        out_specs=[_tok_spec(tm, wd) for wd in widths],
        out_shape=[jax.ShapeDtypeStruct((g, r, wd), BF16) for wd in widths],
        compiler_params=_cparams(("parallel", "parallel")),
        name=name,
    )(*args)


def _softmax_pv(q2, kv_pairs, bias=None):
    scores = [_dot_nt(q2, k) for k, _ in kv_pairs]
    if bias is not None:
        scores[0] = scores[0] + bias
    m = scores[0].max(axis=-1, keepdims=True)
    for s in scores[1:]:
        m = jnp.maximum(m, s.max(axis=-1, keepdims=True))
    acc = None
    for s, (_, v) in zip(scores, kv_pairs):
        a = _dot(jnp.exp2(s - m).astype(BF16), v)
        acc = a if acc is None else acc + a
    return acc[:, :LANES] / acc[:, LANES:]


def _split_pair(qp, lo):
    zero = jnp.zeros_like(qp)
    return jnp.concatenate([jnp.where(lo, qp, zero), jnp.where(lo, zero, qp)], axis=0)


ATTN_CHAIN_ROWS = 256


def _chain_schedule(rows, n_pairs, chain_rows):
    blocks = [(r0, min(chain_rows, rows - r0)) for r0 in range(0, rows, chain_rows)]
    sched = [(p, r0, n) for r0, n in blocks for p in range(n_pairs)]
    if len(blocks) < 2:
        return sched
    (p, r0, n), rest = sched[0], sched[1:]
    sched = [(p, r0, n // 2), (p, r0 + n // 2, n - n // 2)] + rest
    (p, r0, n), rest = sched[-1], sched[:-1]
    return rest + [(p, r0, n - n // 2), (p, r0 + n - n // 2, n // 2)]


def _attn_ab_kernel(*refs, with_x):
    if with_x:
        (qa_ref, qb_ref, kax_ref, vax_ref, kbx_ref, vbx_ref, kac_ref, vac_ref, kbc_ref, vbc_ref,
         lam_ref, subln_ref, o_ref) = refs
    else:
        (qa_ref, qb_ref, kac_ref, vac_ref, kbc_ref, vbc_ref, lam_ref, subln_ref, o_ref) = refs
    lo = lax.broadcasted_iota(jnp.int32, (1, LANES), 1) < HEAD_DIM

    lp = lam_ref[...]
    lam = (jnp.exp(jnp.sum(lp[0:1] * lp[1:2], axis=-1, keepdims=True))
           - jnp.exp(jnp.sum(lp[2:3] * lp[3:4], axis=-1, keepdims=True)) + LAM_INIT0)

    def gqa_chain(p, r0, n):
        rows = slice(r0, r0 + n)
        sl = slice(p * LANES, (p + 1) * LANES)
        q2 = _split_pair(qa_ref[0, rows, sl], lo)
        kv = [(kac_ref[0], vac_ref[0])]
        if with_x:
            kv.append((kax_ref[0], vax_ref[0]))
        o2 = _softmax_pv(q2, kv)
        o_ref[0, rows, sl] = jnp.where(lo, o2[:n], o2[n:]).astype(o_ref.dtype)

    def diff_chain(i, r0, n):
        rows = slice(r0, r0 + n)
        sl = slice(i * LANES, (i + 1) * LANES)
        sl2 = slice(2 * i * LANES, (2 * i + 2) * LANES)
        q2 = _split_pair(qb_ref[0, rows, sl], lo)
        kv = [(kbc_ref[0, :, sl], vbc_ref[0, :, sl2])]
        if with_x:
            kv.append((kbx_ref[0, :, sl], vbx_ref[0, :, sl2]))
        o2 = _softmax_pv(q2, kv)
        o = o2[:n] - lam * o2[n:]
        ms = jnp.mean(o * o, axis=-1, keepdims=True)
        o = o * lax.rsqrt(ms + EPS) * subln_ref[...] * (1.0 - LAM_INIT0)
        o_ref[0, rows, 512 + i * LANES:512 + (i + 1) * LANES] = o.astype(o_ref.dtype)

    for pair, r0, n in _chain_schedule(qa_ref.shape[1], A_HEADS // 2 + B_HEADS, ATTN_CHAIN_ROWS):
        if pair < A_HEADS // 2:
            gqa_chain(pair, r0, n)
        else:
            diff_chain(pair - A_HEADS // 2, r0, n)


def _attn_ab(q_parts, x_kv, c_kv, lam_p, subln, tq, name):
    qa, qb = q_parts
    b, nq, _ = qa.shape
    with_x = x_kv is not None

    def full(a):
        return pl.BlockSpec((1,) + a.shape[1:], lambda bb, i: (bb, 0, 0))

    args = [qa, qb]
    in_specs = [pl.BlockSpec((1, tq, 512), lambda bb, i: (bb, i, 0))] * 2
    if with_x:
        args += list(x_kv)
        in_specs += [full(a) for a in x_kv]
    args += list(c_kv) + [lam_p, subln]
    in_specs += [full(a) for a in c_kv] + [pl.BlockSpec(lam_p.shape, lambda bb, i: (0, 0)),
                                           pl.BlockSpec(subln.shape, lambda bb, i: (0, 0))]
    return pl.pallas_call(
        functools.partial(_attn_ab_kernel, with_x=with_x),
        grid=(b, nq // tq),
        in_specs=in_specs,
        out_specs=pl.BlockSpec((1, tq, D_MODEL), lambda bb, i: (bb, i, 0)),
        out_shape=jax.ShapeDtypeStruct((b, nq, D_MODEL), BF16),
        compiler_params=_cparams(("parallel", "parallel")),
        name=name,
    )(*args)


FFN_CHUNK = 1024


def _post_kernel(*refs, n_parts, mod_row):
    o_refs = refs[:n_parts]
    wo_refs = refs[n_parts:2 * n_parts]
    x_ref, mod_ref, g_ref, b_ref, w1_ref, w2_ref, y_ref = refs[2 * n_parts:]
    _, _, gate1, shift2, scale2, gate2 = _mod_vectors(mod_ref, mod_row)
    y = _dot(o_refs[0][0], wo_refs[0][...])
    for o_ref, w_ref in zip(o_refs[1:], wo_refs[1:]):
        y = y + _dot(o_ref[0], w_ref[...])
    x1 = _layer_norm(ALPHA * x_ref[0] + gate1 * y, g_ref[0:1, :], b_ref[0:1, :])
    u = (x1 * (1.0 + scale2) + shift2).astype(BF16)
    acc = None
    for c in range(FFN_DIM // FFN_CHUNK):
        sl = slice(c * FFN_CHUNK, (c + 1) * FFN_CHUNK)
        h = jnp.maximum(_dot(u, w1_ref[0, :, sl]), 0.0)
        a = _dot((h * h).astype(BF16), w2_ref[0, sl, :])
        acc = a if acc is None else acc + a
    y_ref[0] = _layer_norm(ALPHA * x1 + gate2 * acc, g_ref[1:2, :], b_ref[1:2, :])


def _post(o_parts, wo_parts, resid, mod, mod_row, ln_g, ln_b, w1, w2, layer, tm, name):
    g, r, _ = resid.shape
    n = len(o_parts)
    single = pl.Buffered(1)
    layer_w = lambda a: pl.BlockSpec((1,) + a.shape[1:], lambda gg, i: (layer, 0, 0), pipeline_mode=single)
    const1 = lambda a: pl.BlockSpec(a.shape, lambda gg, i: (0,) * a.ndim, pipeline_mode=single)
    in_specs = ([_tok_spec(tm, o.shape[2]) for o in o_parts] + [const1(w) for w in wo_parts]
                + [_tok_spec(tm, D_MODEL), MOD_SPEC, _const_spec(ln_g.shape), _const_spec(ln_b.shape),
                   layer_w(w1), layer_w(w2)])
    return pl.pallas_call(
        functools.partial(_post_kernel, n_parts=n, mod_row=mod_row),
        grid=(g, r // tm),
        in_specs=in_specs,
        out_specs=_tok_spec(tm, D_MODEL),
        out_shape=jax.ShapeDtypeStruct((g, r, D_MODEL), F32),
        compiler_params=_cparams(("parallel", "parallel")),
        name=name,
    )(*o_parts, *wo_parts, resid, mod, ln_g, ln_b, w1, w2)


CD_Q, CD_F, CD_K, CD_V, CD_W = 0, 512, 1024, 1536, 2048


def _inproj_cd_kernel(*refs, with_q, mod_row):
    if with_q:
        x_ref, mod_ref, w_ref, wa_ref, wb_ref, q_ref, fa_ref, fb_ref, k_ref, v_ref = refs
    else:
        x_ref, mod_ref, w_ref, k_ref, v_ref = refs
    shift, scale = _mod_vectors(mod_ref, mod_row)[0:2]
    u = (x_ref[0] * (1.0 + scale) + shift).astype(BF16)
    if with_q:
        q_ref[0] = (_dot(u, w_ref[:, CD_Q:CD_F]) * Q_SCALE).astype(q_ref.dtype)
        fa_ref[0] = _dot(u, wa_ref[...]).astype(fa_ref.dtype)
        fb_ref[0] = _dot(u, wb_ref[...]).astype(fb_ref.dtype)
    kv = _dot(u, w_ref[:, CD_K:CD_W])
    k_ref[0] = kv[:, :CD_V - CD_K].astype(k_ref.dtype)
    _store_v_ext(v_ref, slice(None), kv[:, CD_V - CD_K:])


def _inproj_cd(tok, mod, mod_row, w, w_ab, tm, name):
    g, r, _ = tok.shape
    with_q = w_ab is not None
    weights = [w] + (list(w_ab) if with_q else [])
    widths = ((512, 512, 512) if with_q else ()) + (512, 1024)
    return pl.pallas_call(
        functools.partial(_inproj_cd_kernel, with_q=with_q, mod_row=mod_row),
        grid=(g, r // tm),
        in_specs=[_tok_spec(tm, D_MODEL), MOD_SPEC] + [_const_spec(a.shape) for a in weights],
        out_specs=[_tok_spec(tm, wd) for wd in widths],
        out_shape=[jax.ShapeDtypeStruct((g, r, wd), BF16) for wd in widths],
        compiler_params=_cparams(("parallel", "parallel")),
        name=name,
    )(tok, mod, *weights)


def _dft_weight_kernel(w_ref, cc_ref, sc_ref, wa_ref, wb_ref):
    c_hi, c_mid, _ = _split3(cc_ref[...])
    s_hi, s_mid, _ = _split3(sc_ref[...])
    for gi in range(D_GROUPS):
        sl = slice(gi * D_GROUP_DIM, (gi + 1) * D_GROUP_DIM)
        w_hi, w_mid, _ = _split3(w_ref[:, sl])
        wa_ref[:, sl] = (_dot(w_hi, c_hi) + (_dot(w_mid, c_hi) + _dot(w_hi, c_mid))).astype(BF16)
        wb_ref[:, sl] = (_dot(w_hi, s_hi) + (_dot(w_mid, s_hi) + _dot(w_hi, s_mid))).astype(BF16)


def _dft_weights(w_in, cc, sc):
    width = CD_K - CD_F
    out = pl.BlockSpec((D_MODEL, width), lambda i: (0, 0))
    return pl.pallas_call(
        _dft_weight_kernel,
        grid=(1,),
        in_specs=[pl.BlockSpec((D_MODEL, width), lambda i: (0, CD_F // width)),
                  pl.BlockSpec(cc.shape, lambda i: (0, 0)), pl.BlockSpec(sc.shape, lambda i: (0, 0))],
        out_specs=[out, out],
        out_shape=[jax.ShapeDtypeStruct((D_MODEL, width), BF16)] * 2,
        compiler_params=_cparams(("arbitrary",)),
        name="dft_weights",
    )(w_in, cc, sc)


DFT_ROWS = 512


def _dft_time_kernel(a_ref, b_ref, ct_ref, sn_ref, o_ref):
    a = a_ref[0]
    bm = b_ref[0]
    for r in range(SEQ // DFT_ROWS):
        sl = slice(r * DFT_ROWS, (r + 1) * DFT_ROWS)
        y = _dot(ct_ref[sl, :], a) + _dot(sn_ref[sl, :], bm)
        o_ref[0, sl, :] = y.astype(o_ref.dtype)


def _dft_time(a, bm, ct, sn):
    b = a.shape[0]
    single = pl.Buffered(1)
    tok = pl.BlockSpec((1, SEQ, 512), lambda bb: (bb, 0, 0))
    tab = pl.BlockSpec((SEQ, SEQ), lambda bb: (0, 0), pipeline_mode=single)
    return pl.pallas_call(
        _dft_time_kernel,
        grid=(b,),
        in_specs=[tok, tok, tab, tab],
        out_specs=tok,
        out_shape=jax.ShapeDtypeStruct((b, SEQ, 512), BF16),
        compiler_params=_cparams(("parallel",)),
        name="dft_time",
    )(a, bm, ct, sn)


def _bias_table_kernel(r_ref, o_ref):
    n = GRID_W * LANES
    row = lax.broadcasted_iota(jnp.int32, (2 * 32, n), 0)
    lane = lax.broadcasted_iota(jnp.int32, (2 * 32, n), 1)
    e_row, i_row = row >> 5, row & 31
    c = lane >> 7
    e_lane = (lane >> 6) & 1
    kc = lane & (GRID_W - 1)
    sel = jnp.where((e_row == e_lane) & (kc - c + (WIN_W - 1) == i_row), 1.0, 0.0).astype(BF16)
    hi, mid, lo = _split3(r_ref[...])
    t = _dot(hi, sel) + _dot(mid, sel) + _dot(lo, sel)
    c1 = c[0:1]
    kc1 = kc[0:1]
    cstart = jnp.clip(c1 - WIN_W // 2, 0, GRID_W - WIN_W)
    col_ok = (kc1 >= cstart) & (kc1 < cstart + WIN_W)
    o_ref[...] = jnp.where(col_ok, t * LOG2E, NEG)


def _bias_table(rpb):
    n_dr = 2 * WIN_H - 1
    p = jnp.pad(rpb, ((0, 0), (CB_OFFSET, CB_ENTRIES + 1 - n_dr - CB_OFFSET), (0, 1)))
    r2 = jnp.stack([p[:, 0:CB_ENTRIES], p[:, 1:CB_ENTRIES + 1]], axis=2)
    r2 = r2.reshape(C_HEADS * CB_ENTRIES, 2 * 32)
    t = pl.pallas_call(
        _bias_table_kernel,
        out_shape=jax.ShapeDtypeStruct((C_HEADS * CB_ENTRIES, GRID_W * LANES), F32),
        compiler_params=pltpu.CompilerParams(vmem_limit_bytes=VMEM_LIMIT),
        name="natten_bias_table",
    )(r2)
    return t.reshape(C_HEADS, CB_ENTRIES, GRID_W, LANES)


NAT_BLOCKS = 4


def _natten_block(blk, q_ref, k_ref, v_ref, kc_ref, vc_ref, cb_ref, o_ref, lo, e):
    rows = slice(blk * NAT_Q, (blk + 1) * NAT_Q)
    r0 = NAT_ROWS * (NAT_BLOCKS * pl.program_id(1) + blk)
    ws = jnp.clip(r0 - WIN_H // 2, 0, GRID_H - NAT_WIN_ROWS)
    koff = pl.multiple_of(ws * GRID_W, LANES)

    entry = []
    rmask = []
    for qr in range(NAT_ROWS):
        r = r0 + qr
        lo_r = jnp.clip(r - WIN_H // 2, 0, GRID_H - WIN_H) - r + (WIN_H - 1)
        ent_q, mask_q = [], []
        for j in range(NAT_TILES):
            dr0 = ws + 2 * j - r + (WIN_H - 1)
            ent_q.append(dr0 + CB_OFFSET)
            dr = dr0 + e
            ok = (dr >= lo_r) & (dr < lo_r + WIN_H)
            mask_q.append(jnp.where(ok, 0.0, NEG))
        entry.append(ent_q)
        rmask.append(mask_q)

    for p in range(C_HEADS // 2):
        sl = slice(p * LANES, (p + 1) * LANES)
        sl2 = slice(2 * p * LANES, (2 * p + 2) * LANES)
        q2 = _split_pair(q_ref[0, rows, sl], lo)
        bias = jnp.concatenate(
            [jnp.concatenate([cb_ref[2 * p + hh, entry[qr][j]] + rmask[qr][j] for j in range(NAT_TILES)], axis=1)
             for hh in range(2) for qr in range(NAT_ROWS)], axis=0)
        kv = [(k_ref[0, pl.ds(koff, NAT_WIN), sl], v_ref[0, pl.ds(koff, NAT_WIN), sl2]),
              (kc_ref[0, :, sl], vc_ref[0, :, sl2])]
        o2 = _softmax_pv(q2, kv, bias)
        o_ref[0, rows, sl] = jnp.where(lo, o2[:NAT_Q], o2[NAT_Q:]).astype(o_ref.dtype)


def _natten_kernel(q_ref, k_ref, v_ref, kc_ref, vc_ref, cb_ref, o_ref):
    lane = lax.broadcasted_iota(jnp.int32, (1, LANES), 1)
    lo = lane < HEAD_DIM
    e = jnp.where(lo, 0, 1)
    for blk in range(NAT_BLOCKS):
        _natten_block(blk, q_ref, k_ref, v_ref, kc_ref, vc_ref, cb_ref, o_ref, lo, e)


def _natten(q, k, v, kc, vc, cb):
    b = q.shape[0]
    tq = NAT_BLOCKS * NAT_Q
    qspec = pl.BlockSpec((1, tq, 512), lambda bb, i: (bb, i, 0))
    full = lambda a: pl.BlockSpec((1,) + a.shape[1:], lambda bb, i: (bb, 0, 0))
    return pl.pallas_call(
        _natten_kernel,
        grid=(b, SEQ // tq),
        in_specs=[qspec, full(k), full(v), full(kc), full(vc),
                  pl.BlockSpec(cb.shape, lambda bb, i: (0, 0, 0, 0))],
        out_specs=qspec,
        out_shape=jax.ShapeDtypeStruct((b, SEQ, 512), BF16),
        compiler_params=_cparams(("parallel", "parallel")),
        name="natten",
    )(q, k, v, kc, vc, cb)


def kernel(x, c, ctx, c_ctx, mod_w, mod_b, ln_g, ln_b, ffn_w1, ffn_w2, ab_w_in, ab_w_out, a_q_norm, a_k_norm,
           b_lambda, b_subln, cd_w_in, cd_w_out, c_rpb):
    b = x.shape[0]
    nctx = b * CTX_LEN
    tm = 512
    tmc = min(tm, nctx)

    cs = jnp.concatenate([c, c_ctx[None, :], jnp.zeros((MOD_ROWS - b - 1, D_MODEL), F32)], axis=0)
    mods = _modulation(cs, mod_w, mod_b)
    ctx_row = b

    w1 = ffn_w1.astype(BF16)
    w2 = ffn_w2.astype(BF16)
    cflat = ctx.reshape(1, nctx, D_MODEL)

    perm = np.asarray(QA_PERM)
    w_ab = ab_w_in.reshape(D_MODEL, AB_W).astype(BF16)
    w_out = ab_w_out.reshape(D_MODEL, D_MODEL)
    w_out_ab = jnp.concatenate([w_out[:512].reshape(A_HEADS, HEAD_DIM, D_MODEL)[perm].reshape(512, D_MODEL),
                                w_out[512:]], axis=0).astype(BF16)
    gq = jnp.tile(a_q_norm[0], A_HEADS)[None, :]
    gk = jnp.tile(a_k_norm[0], A_KV_HEADS)[None, :]
    rope_tabs = tuple(jnp.asarray(t) for t in _rope_tables())
    subln = b_subln[0][None, :]
    mod0 = mods[0]

    xq = _inproj_ab(x, mod0, None, w_ab, gq, gk, rope_tabs, 2 * tm, "inproj_ab_x")
    cq = _inproj_ab(cflat, mod0, ctx_row, w_ab, gq, gk, None, tmc, "inproj_ab_ctx")
    cq = [a.reshape(b, CTX_LEN, a.shape[2]) for a in cq]
    qa_x, qb_x, ka_x, kb_x, va_x, vb_x = xq
    qa_c, qb_c, ka_c, kb_c, va_c, vb_c = cq
    c_kv = (ka_c, va_c, kb_c, vb_c)
    o_x = _attn_ab((qa_x, qb_x), (ka_x, va_x, kb_x, vb_x), c_kv, b_lambda[0], subln, 512, "attn_ab_x")
    o_c = _attn_ab((qa_c, qb_c), None, c_kv, b_lambda[0], subln, CTX_LEN, "attn_ab_ctx")

    x2 = _post([o_x], [w_out_ab], x, mod0, None, ln_g[0], ln_b[0], w1, w2, 0, tm, "post0_x")
    c2 = _post([o_c.reshape(1, nctx, D_MODEL)], [w_out_ab], cflat, mod0, ctx_row, ln_g[0], ln_b[0], w1, w2, 0,
               tmc, "post0_ctx")

    mod1 = mods[1]
    w_in = cd_w_in.reshape(D_MODEL, CD_W)
    ct, sn, cc, sc = _dft_tables()
    w_fab = _dft_weights(w_in, jnp.asarray(cc), jnp.asarray(sc))
    w_cd = w_in.astype(BF16)
    q_n, f_a, f_b, k_n, v_n = _inproj_cd(x2, mod1, None, w_cd, w_fab, 2 * tm, "inproj_cd_x")
    kc_n, vc_n = _inproj_cd(c2, mod1, ctx_row, w_cd, None, tmc, "inproj_cd_ctx")
    kc_n = kc_n.reshape(b, CTX_LEN, 512)
    vc_n = vc_n.reshape(b, CTX_LEN, 1024)

    o_d = _dft_time(f_a, f_b, jnp.asarray(ct).astype(BF16), jnp.asarray(sn).astype(BF16))
    cb = _bias_table(c_rpb[0])
    o_n = _natten(q_n, k_n, v_n, kc_n, vc_n, cb)

    w_out = cd_w_out.reshape(D_MODEL, D_MODEL).astype(BF16)
    return _post([o_n, o_d], [w_out[:512], w_out[512:]], x2, mod1, None, ln_g[1], ln_b[1], w1, w2, 1, tm, "post1_x")
```

```python
import functools
import math

import numpy as np
import jax
import jax.numpy as jnp
from jax import lax
from jax.experimental import pallas as pl
from jax.experimental.pallas import tpu as pltpu

F32 = jnp.float32
BF16 = jnp.bfloat16

D_MODEL = 1024
SEQ = 2048
CTX_LEN = 256
DEPTH = 2
GRID_W = 64
GRID_H = SEQ // GRID_W
HEAD_DIM = 64
A_HEADS = 8
A_KV_HEADS = 2
B_HEADS = 4
C_HEADS = 8
WIN_H = 8
WIN_W = 16
D_GROUPS = 4
D_GROUP_DIM = 128
FFN_DIM = 4 * D_MODEL
ROPE_THETA = 10000.0
ALPHA = (2.0 * DEPTH) ** 0.25
EPS = 1e-6
ATTN_SCALE = HEAD_DIM ** -0.5
LOG2E = math.log2(math.e)
Q_SCALE = ATTN_SCALE * LOG2E
LAM_INIT0 = 0.8 - 0.6 * math.exp(-0.3 * 0)
NEG = -1e30

LANES = 128
MOD_ROWS = 16
VMEM_LIMIT = 56 * 1024 * 1024

QA_PERM = (0, 4, 1, 5, 2, 6, 3, 7)

NAT_ROWS = 4
NAT_Q = NAT_ROWS * GRID_W
NAT_WIN_ROWS = NAT_ROWS + WIN_H
NAT_WIN = NAT_WIN_ROWS * GRID_W
NAT_TILES = NAT_WIN_ROWS // 2
_NAT_DR0 = [min(max(r0 - WIN_H // 2, 0), GRID_H - NAT_WIN_ROWS) + 2 * j - (r0 + qr) + WIN_H - 1
            for r0 in range(0, GRID_H, NAT_ROWS) for qr in range(NAT_ROWS) for j in range(NAT_TILES)]
CB_OFFSET = -min(_NAT_DR0)
CB_ENTRIES = max(_NAT_DR0) + CB_OFFSET + 1


def _cparams(sem, flags=None):
    return pltpu.CompilerParams(dimension_semantics=sem, vmem_limit_bytes=VMEM_LIMIT, flags=flags)


def _dot(a, b):
    return jnp.dot(a, b, preferred_element_type=F32)


def _dot_nt(a, b):
    return lax.dot_general(a, b, (((1,), (1,)), ((), ())), preferred_element_type=F32)


def _split3(a):
    hi = a.astype(BF16)
    r1 = a - hi.astype(F32)
    mid = r1.astype(BF16)
    lo = (r1 - mid.astype(F32)).astype(BF16)
    return hi, mid, lo


def _layer_norm(z, g, b):
    mu = jnp.mean(z, axis=-1, keepdims=True)
    zc = z - mu
    var = jnp.mean(zc * zc, axis=-1, keepdims=True)
    return zc * lax.rsqrt(var + EPS) * g + b


@functools.lru_cache(maxsize=None)
def _rope_tables():
    pos = np.arange(SEQ)
    row = (pos // GRID_W).astype(np.float64)
    col = (pos % GRID_W).astype(np.float64)
    half = HEAD_DIM // 2
    freqs = np.power(ROPE_THETA, -np.arange(0, half, 2, dtype=np.float64) / half)
    def axis_angles(p):
        a = p[:, None] * freqs[None, :]
        return np.concatenate([a, a], axis=-1)
    ang = np.concatenate([axis_angles(row), axis_angles(col)], axis=-1)
    cos, sin = np.cos(ang), np.sin(ang)
    first = (np.arange(HEAD_DIM) % half) < (half // 2)
    sin_a = np.where(first[None, :], -sin, 0.0)
    sin_b = np.where(first[None, :], 0.0, sin)
    tile = lambda t: np.tile(t, (1, LANES // HEAD_DIM)).astype(np.float32)
    return tile(cos), tile(sin_a), tile(sin_b)


@functools.lru_cache(maxsize=None)
def _group_mean_matrix(width):
    g = np.arange(width) // HEAD_DIM
    return (g[:, None] == g[None, :]).astype(np.float32) / HEAD_DIM


@functools.lru_cache(maxsize=None)
def _dft_tables():
    t = np.arange(SEQ, dtype=np.int64)
    k = (t[:, None] * t[None, :]) % SEQ
    ang = 2.0 * np.pi * k.astype(np.float64) / SEQ
    ct = np.cos(ang).astype(np.float32)
    sn = (-np.sin(ang)).astype(np.float32)
    c = np.arange(D_GROUP_DIM, dtype=np.int64)
    kc = (c[:, None] * c[None, :]) % D_GROUP_DIM
    angc = 2.0 * np.pi * kc.astype(np.float64) / D_GROUP_DIM
    norm = 1.0 / math.sqrt(SEQ * D_GROUP_DIM)
    return ct, sn, (np.cos(angc) * norm).astype(np.float32), (np.sin(angc) * norm).astype(np.float32)


def _mod_kernel(c_ref, w_ref, b_ref, o_ref):
    c = c_ref[...]
    a = c / (1.0 + jnp.exp(-c))
    a_hi, a_mid, _ = _split3(a)
    y = _dot(jnp.concatenate([a_hi, a_mid], axis=0), w_ref[0].astype(BF16))
    o_ref[0, 0] = y[:MOD_ROWS] + y[MOD_ROWS:] + b_ref[0]


def _modulation(cs, mod_w, mod_b):
    n = 6 * D_MODEL
    return pl.pallas_call(
        _mod_kernel,
        grid=(DEPTH, 6),
        in_specs=[pl.BlockSpec((MOD_ROWS, D_MODEL), lambda l, j: (0, 0)),
                  pl.BlockSpec((1, D_MODEL, D_MODEL), lambda l, j: (l, 0, j)),
                  pl.BlockSpec((1, 1, D_MODEL), lambda l, j: (l, 0, j))],
        out_specs=pl.BlockSpec((1, 1, MOD_ROWS, D_MODEL), lambda l, j: (l, j, 0, 0)),
        out_shape=jax.ShapeDtypeStruct((DEPTH, 6, MOD_ROWS, D_MODEL), F32),
        compiler_params=_cparams(("arbitrary", "arbitrary")),
        name="modulation",
    )(cs, mod_w, mod_b.reshape(DEPTH, 1, n))


def _tok_spec(tm, width):
    return pl.BlockSpec((1, tm, width), lambda g, i: (g, i, 0))


MOD_SPEC = pl.BlockSpec((6, MOD_ROWS, D_MODEL), lambda g, i: (0, 0, 0))


def _mod_vectors(mod_ref, mod_row):
    r = pl.program_id(0) if mod_row is None else mod_row
    return [mod_ref[k, pl.ds(r, 1), :] for k in range(6)]


def _const_spec(shape):
    nd = len(shape)
    return pl.BlockSpec(shape, lambda g, i: (0,) * nd)


AB_QA, AB_QB, AB_KA, AB_VA, AB_KB, AB_VB, AB_W = 0, 512, 1024, 1152, 1280, 1792, 2304


TOKEN_CHAIN_ROWS = 256


def _store_v_ext(o_ref, rows, v):
    ones = jnp.ones((v.shape[0], LANES), o_ref.dtype)
    for j in range(v.shape[1] // LANES):
        o_ref[0, rows, 2 * j * LANES:(2 * j + 1) * LANES] = v[:, j * LANES:(j + 1) * LANES].astype(o_ref.dtype)
        o_ref[0, rows, (2 * j + 1) * LANES:(2 * j + 2) * LANES] = ones


def _rope_chunk(c, cos, sin_a, sin_b):
    return c * cos + pltpu.roll(c, LANES - 16, axis=1) * sin_a + pltpu.roll(c, 16, axis=1) * sin_b


def _inproj_ab_kernel(*refs, rope, mod_row):
    if rope:
        (x_ref, mod_ref, w_ref, gq_ref, gk_ref, g512_ref, g128_ref, cos_ref, sa_ref, sb_ref,
         qa_ref, qb_ref, ka_ref, kb_ref, va_ref, vb_ref) = refs
    else:
        (x_ref, mod_ref, w_ref, gq_ref, gk_ref, g512_ref, g128_ref,
         qa_ref, qb_ref, ka_ref, kb_ref, va_ref, vb_ref) = refs

    def rms(v, g_ref, gm_ref):
        ms = _dot((v * v).astype(BF16), gm_ref[...])
        return v * lax.rsqrt(ms + EPS) * g_ref[...]

    shift, scale = _mod_vectors(mod_ref, mod_row)[0:2]
    lo = lax.broadcasted_iota(jnp.int32, (1, LANES), 1) < HEAD_DIM
    tm = x_ref.shape[1]
    tc = min(TOKEN_CHAIN_ROWS, tm)
    for rb in range(tm // tc):
        rows = slice(rb * tc, (rb + 1) * tc)
        if rope:
            cos, sin_a, sin_b = cos_ref[rows, :], sa_ref[rows, :], sb_ref[rows, :]
        u = (x_ref[0, rows, :] * (1.0 + scale) + shift).astype(BF16)
        h = _dot(u, w_ref[...])

        def chunks(v, q_scale):
            out = []
            for j in range(v.shape[1] // LANES):
                c = v[:, j * LANES:(j + 1) * LANES]
                if rope:
                    c = _rope_chunk(c, cos, sin_a, sin_b)
                out.append(c * q_scale if q_scale != 1.0 else c)
            return out

        def store(o_ref, cs):
            for j, c in enumerate(cs):
                o_ref[0, rows, j * LANES:(j + 1) * LANES] = c.astype(o_ref.dtype)

        qa = chunks(rms(h[:, AB_QA:AB_QB], gq_ref, g512_ref), Q_SCALE)
        qa_perm = []
        for p in range(A_HEADS // 2):
            a, b = qa[p // 2], qa[A_HEADS // 4 + p // 2]
            if p % 2 == 0:
                qa_perm.append(jnp.where(lo, a, pltpu.roll(b, HEAD_DIM, axis=1)))
            else:
                qa_perm.append(jnp.where(lo, pltpu.roll(a, HEAD_DIM, axis=1), b))
        store(qa_ref, qa_perm)
        store(qb_ref, chunks(h[:, AB_QB:AB_KA], Q_SCALE))
        store(ka_ref, chunks(rms(h[:, AB_KA:AB_VA], gk_ref, g128_ref), 1.0))
        store(kb_ref, chunks(h[:, AB_KB:AB_VB], 1.0))
        _store_v_ext(va_ref, rows, h[:, AB_VA:AB_KB])
        _store_v_ext(vb_ref, rows, h[:, AB_VB:AB_W])


def _inproj_ab(tok, mod, mod_row, w, gq, gk, rope_tabs, tm, name):
    g, r, _ = tok.shape
    rope = rope_tabs is not None
    consts = [w, gq, gk, jnp.asarray(_group_mean_matrix(512), BF16), jnp.asarray(_group_mean_matrix(128), BF16)]
    in_specs = [_tok_spec(tm, D_MODEL), MOD_SPEC] + [_const_spec(c.shape) for c in consts]
    args = [tok, mod] + consts
    if rope:
        in_specs += [pl.BlockSpec((tm, LANES), lambda gg, i: (i, 0))] * 3
        args += list(rope_tabs)
    widths = (512, 512, 128, 512, 2 * 128, 2 * 512)
    return pl.pallas_call(
        functools.partial(_inproj_ab_kernel, rope=rope, mod_row=mod_row),
        grid=(g, r // tm),
        in_specs=in_specs,
        out_specs=[_tok_spec(tm, wd) for wd in widths],
        out_shape=[jax.ShapeDtypeStruct((g, r, wd), BF16) for wd in widths],
        compiler_params=_cparams(("parallel", "parallel")),
        name=name,
    )(*args)


def _softmax_pv(q2, kv_pairs, bias=None):
    scores = [_dot_nt(q2, k) for k, _ in kv_pairs]
    if bias is not None:
        scores[0] = scores[0] + bias
    m = scores[0].max(axis=-1, keepdims=True)
    for s in scores[1:]:
        m = jnp.maximum(m, s.max(axis=-1, keepdims=True))
    acc = None
    for s, (_, v) in zip(scores, kv_pairs):
        a = _dot(jnp.exp2(s - m).astype(BF16), v)
        acc = a if acc is None else acc + a
    return acc[:, :LANES] / acc[:, LANES:]


def _split_pair(qp, lo):
    zero = jnp.zeros_like(qp)
    return jnp.concatenate([jnp.where(lo, qp, zero), jnp.where(lo, zero, qp)], axis=0)


ATTN_CHAIN_ROWS = 256


def _chain_schedule(rows, n_pairs, chain_rows):
    blocks = [(r0, min(chain_rows, rows - r0)) for r0 in range(0, rows, chain_rows)]
    sched = [(p, r0, n) for r0, n in blocks for p in range(n_pairs)]
    if len(blocks) < 2:
        return sched
    (p, r0, n), rest = sched[0], sched[1:]
    sched = [(p, r0, n // 2), (p, r0 + n // 2, n - n // 2)] + rest
    (p, r0, n), rest = sched[-1], sched[:-1]
    return rest + [(p, r0, n - n // 2), (p, r0 + n - n // 2, n // 2)]


def _attn_ab_kernel(*refs, with_x, n_casts):
    if n_casts:
        cast_in = refs[len(refs) - 2 * n_casts - 1:len(refs) - n_casts - 1]
        cast_out = refs[len(refs) - n_casts:]
        refs = refs[:len(refs) - 2 * n_casts - 1] + (refs[len(refs) - n_casts - 1],)
        for src, dst in zip(cast_in, cast_out):
            dst[...] = src[...].astype(dst.dtype)
    if with_x:
        (qa_ref, qb_ref, kax_ref, vax_ref, kbx_ref, vbx_ref, kac_ref, vac_ref, kbc_ref, vbc_ref,
         lam_ref, subln_ref, o_ref) = refs
    else:
        (qa_ref, qb_ref, kac_ref, vac_ref, kbc_ref, vbc_ref, lam_ref, subln_ref, o_ref) = refs
    lo = lax.broadcasted_iota(jnp.int32, (1, LANES), 1) < HEAD_DIM

    lp = lam_ref[...]
    lam = (jnp.exp(jnp.sum(lp[0:1] * lp[1:2], axis=-1, keepdims=True))
           - jnp.exp(jnp.sum(lp[2:3] * lp[3:4], axis=-1, keepdims=True)) + LAM_INIT0)

    def gqa_chain(p, r0, n):
        rows = slice(r0, r0 + n)
        sl = slice(p * LANES, (p + 1) * LANES)
        q2 = _split_pair(qa_ref[0, rows, sl], lo)
        kv = [(kac_ref[0], vac_ref[0])]
        if with_x:
            kv.append((kax_ref[0], vax_ref[0]))
        o2 = _softmax_pv(q2, kv)
        o_ref[0, rows, sl] = jnp.where(lo, o2[:n], o2[n:]).astype(o_ref.dtype)

    def diff_chain(i, r0, n):
        rows = slice(r0, r0 + n)
        sl = slice(i * LANES, (i + 1) * LANES)
        sl2 = slice(2 * i * LANES, (2 * i + 2) * LANES)
        q2 = _split_pair(qb_ref[0, rows, sl], lo)
        kv = [(kbc_ref[0, :, sl], vbc_ref[0, :, sl2])]
        if with_x:
            kv.append((kbx_ref[0, :, sl], vbx_ref[0, :, sl2]))
        o2 = _softmax_pv(q2, kv)
        o = o2[:n] - lam * o2[n:]
        ms = jnp.mean(o * o, axis=-1, keepdims=True)
        o = o * lax.rsqrt(ms + EPS) * subln_ref[...] * (1.0 - LAM_INIT0)
        o_ref[0, rows, 512 + i * LANES:512 + (i + 1) * LANES] = o.astype(o_ref.dtype)

    for pair, r0, n in _chain_schedule(qa_ref.shape[1], A_HEADS // 2 + B_HEADS, ATTN_CHAIN_ROWS):
        if pair < A_HEADS // 2:
            gqa_chain(pair, r0, n)
        else:
            diff_chain(pair - A_HEADS // 2, r0, n)


def _attn_ab(q_parts, x_kv, c_kv, lam_p, subln, tq, name, casts=()):
    qa, qb = q_parts
    b, nq, _ = qa.shape
    with_x = x_kv is not None
    steps = nq // tq

    def full(a):
        return pl.BlockSpec((1,) + a.shape[1:], lambda bb, i: (bb, 0, 0))

    args = [qa, qb]
    in_specs = [pl.BlockSpec((1, tq, 512), lambda bb, i: (bb, i, 0))] * 2
    if with_x:
        args += list(x_kv)
        in_specs += [full(a) for a in x_kv]
    args += list(c_kv) + [lam_p, subln]
    in_specs += [full(a) for a in c_kv] + [pl.BlockSpec(lam_p.shape, lambda bb, i: (0, 0)),
                                           pl.BlockSpec(subln.shape, lambda bb, i: (0, 0))]
    cast_specs = [pl.BlockSpec((a.shape[0] // (b * steps), a.shape[1]), lambda bb, i: (bb * steps + i, 0))
                  for a in casts]
    outs = pl.pallas_call(
        functools.partial(_attn_ab_kernel, with_x=with_x, n_casts=len(casts)),
        grid=(b, steps),
        in_specs=in_specs + cast_specs,
        out_specs=[pl.BlockSpec((1, tq, D_MODEL), lambda bb, i: (bb, i, 0))] + cast_specs,
        out_shape=[jax.ShapeDtypeStruct((b, nq, D_MODEL), BF16)]
                  + [jax.ShapeDtypeStruct(a.shape, BF16) for a in casts],
        compiler_params=_cparams(("parallel", "parallel")),
        name=name,
    )(*args, *casts)
    return outs[0] if not casts else outs


FFN_CHUNK = 1024


def _post_kernel(*refs, n_parts, mod_row):
    o_refs = refs[:n_parts]
    wo_refs = refs[n_parts:2 * n_parts]
    x_ref, mod_ref, g_ref, b_ref, w1_ref, w2_ref, y_ref = refs[2 * n_parts:]
    _, _, gate1, shift2, scale2, gate2 = _mod_vectors(mod_ref, mod_row)
    y = _dot(o_refs[0][0], wo_refs[0][...])
    for o_ref, w_ref in zip(o_refs[1:], wo_refs[1:]):
        y = y + _dot(o_ref[0], w_ref[...])
    x1 = _layer_norm(ALPHA * x_ref[0] + gate1 * y, g_ref[0:1, :], b_ref[0:1, :])
    u = (x1 * (1.0 + scale2) + shift2).astype(BF16)
    acc = None
    for c in range(FFN_DIM // FFN_CHUNK):
        sl = slice(c * FFN_CHUNK, (c + 1) * FFN_CHUNK)
        h = jnp.maximum(_dot(u, w1_ref[0, :, sl]), 0.0)
        a = _dot((h * h).astype(BF16), w2_ref[0, sl, :])
        acc = a if acc is None else acc + a
    y_ref[0] = _layer_norm(ALPHA * x1 + gate2 * acc, g_ref[1:2, :], b_ref[1:2, :])


def _post(o_parts, wo_parts, resid, mod, mod_row, ln_g, ln_b, w1, w2, layer, tm, name):
    g, r, _ = resid.shape
    n = len(o_parts)
    single = pl.Buffered(1)
    layer_w = lambda a: pl.BlockSpec((1,) + a.shape[1:], lambda gg, i: (layer, 0, 0), pipeline_mode=single)
    const1 = lambda a: pl.BlockSpec(a.shape, lambda gg, i: (0,) * a.ndim, pipeline_mode=single)
    in_specs = ([_tok_spec(tm, o.shape[2]) for o in o_parts] + [const1(w) for w in wo_parts]
                + [_tok_spec(tm, D_MODEL), MOD_SPEC, _const_spec(ln_g.shape), _const_spec(ln_b.shape),
                   layer_w(w1), layer_w(w2)])
    return pl.pallas_call(
        functools.partial(_post_kernel, n_parts=n, mod_row=mod_row),
        grid=(g, r // tm),
        in_specs=in_specs,
        out_specs=_tok_spec(tm, D_MODEL),
        out_shape=jax.ShapeDtypeStruct((g, r, D_MODEL), F32),
        compiler_params=_cparams(("parallel", "parallel")),
        name=name,
    )(*o_parts, *wo_parts, resid, mod, ln_g, ln_b, w1, w2)


CD_Q, CD_F, CD_K, CD_V, CD_W = 0, 512, 1024, 1536, 2048


def _inproj_cd_kernel(*refs, with_q, mod_row):
    if with_q:
        x_ref, mod_ref, w_ref, wa_ref, wb_ref, q_ref, fa_ref, fb_ref, k_ref, v_ref = refs
    else:
        x_ref, mod_ref, w_ref, k_ref, v_ref = refs
    shift, scale = _mod_vectors(mod_ref, mod_row)[0:2]
    u = (x_ref[0] * (1.0 + scale) + shift).astype(BF16)
    if with_q:
        q_ref[0] = (_dot(u, w_ref[:, CD_Q:CD_F]) * Q_SCALE).astype(q_ref.dtype)
        fa_ref[0] = _dot(u, wa_ref[...]).astype(fa_ref.dtype)
        fb_ref[0] = _dot(u, wb_ref[...]).astype(fb_ref.dtype)
    kv = _dot(u, w_ref[:, CD_K:CD_W])
    k_ref[0] = kv[:, :CD_V - CD_K].astype(k_ref.dtype)
    _store_v_ext(v_ref, slice(None), kv[:, CD_V - CD_K:])


def _inproj_cd(tok, mod, mod_row, w, w_ab, tm, name):
    g, r, _ = tok.shape
    with_q = w_ab is not None
    weights = [w] + (list(w_ab) if with_q else [])
    widths = ((512, 512, 512) if with_q else ()) + (512, 1024)
    return pl.pallas_call(
        functools.partial(_inproj_cd_kernel, with_q=with_q, mod_row=mod_row),
        grid=(g, r // tm),
        in_specs=[_tok_spec(tm, D_MODEL), MOD_SPEC] + [_const_spec(a.shape) for a in weights],
        out_specs=[_tok_spec(tm, wd) for wd in widths],
        out_shape=[jax.ShapeDtypeStruct((g, r, wd), BF16) for wd in widths],
        compiler_params=_cparams(("parallel", "parallel")),
        name=name,
    )(tok, mod, *weights)


def _dft_weight_kernel(w_ref, cc_ref, sc_ref, wa_ref, wb_ref):
    c_hi, c_mid, _ = _split3(cc_ref[...])
    s_hi, s_mid, _ = _split3(sc_ref[...])
    for gi in range(D_GROUPS):
        sl = slice(gi * D_GROUP_DIM, (gi + 1) * D_GROUP_DIM)
        w_hi, w_mid, _ = _split3(w_ref[:, sl])
        wa_ref[:, sl] = (_dot(w_hi, c_hi) + (_dot(w_mid, c_hi) + _dot(w_hi, c_mid))).astype(BF16)
        wb_ref[:, sl] = (_dot(w_hi, s_hi) + (_dot(w_mid, s_hi) + _dot(w_hi, s_mid))).astype(BF16)


def _dft_weights(w_in, cc, sc):
    width = CD_K - CD_F
    out = pl.BlockSpec((D_MODEL, width), lambda i: (0, 0))
    return pl.pallas_call(
        _dft_weight_kernel,
        grid=(1,),
        in_specs=[pl.BlockSpec((D_MODEL, width), lambda i: (0, CD_F // width)),
                  pl.BlockSpec(cc.shape, lambda i: (0, 0)), pl.BlockSpec(sc.shape, lambda i: (0, 0))],
        out_specs=[out, out],
        out_shape=[jax.ShapeDtypeStruct((D_MODEL, width), BF16)] * 2,
        compiler_params=_cparams(("arbitrary",)),
        name="dft_weights",
    )(w_in, cc, sc)


DFT_ROWS = 512


def _dft_time_kernel(a_ref, b_ref, ct_ref, sn_ref, o_ref):
    a = a_ref[0]
    bm = b_ref[0]
    for r in range(SEQ // DFT_ROWS):
        sl = slice(r * DFT_ROWS, (r + 1) * DFT_ROWS)
        y = _dot(ct_ref[sl, :], a) + _dot(sn_ref[sl, :], bm)
        o_ref[0, sl, :] = y.astype(o_ref.dtype)


def _dft_time(a, bm, ct, sn):
    b = a.shape[0]
    single = pl.Buffered(1)
    tok = pl.BlockSpec((1, SEQ, 512), lambda bb: (bb, 0, 0))
    tab = pl.BlockSpec((SEQ, SEQ), lambda bb: (0, 0), pipeline_mode=single)
    return pl.pallas_call(
        _dft_time_kernel,
        grid=(b,),
        in_specs=[tok, tok, tab, tab],
        out_specs=tok,
        out_shape=jax.ShapeDtypeStruct((b, SEQ, 512), BF16),
        compiler_params=_cparams(("parallel",)),
        name="dft_time",
    )(a, bm, ct, sn)


def _bias_table_kernel(r_ref, o_ref):
    n = GRID_W * LANES
    row = lax.broadcasted_iota(jnp.int32, (2 * 32, n), 0)
    lane = lax.broadcasted_iota(jnp.int32, (2 * 32, n), 1)
    e_row, i_row = row >> 5, row & 31
    c = lane >> 7
    e_lane = (lane >> 6) & 1
    kc = lane & (GRID_W - 1)
    sel = jnp.where((e_row == e_lane) & (kc - c + (WIN_W - 1) == i_row), 1.0, 0.0).astype(BF16)
    hi, mid, lo = _split3(r_ref[...])
    t = _dot(hi, sel) + _dot(mid, sel) + _dot(lo, sel)
    c1 = c[0:1]
    kc1 = kc[0:1]
    cstart = jnp.clip(c1 - WIN_W // 2, 0, GRID_W - WIN_W)
    col_ok = (kc1 >= cstart) & (kc1 < cstart + WIN_W)
    o_ref[...] = jnp.where(col_ok, t * LOG2E, NEG)


def _bias_table(rpb):
    n_dr = 2 * WIN_H - 1
    p = jnp.pad(rpb, ((0, 0), (CB_OFFSET, CB_ENTRIES + 1 - n_dr - CB_OFFSET), (0, 1)))
    r2 = jnp.stack([p[:, 0:CB_ENTRIES], p[:, 1:CB_ENTRIES + 1]], axis=2)
    r2 = r2.reshape(C_HEADS * CB_ENTRIES, 2 * 32)
    t = pl.pallas_call(
        _bias_table_kernel,
        out_shape=jax.ShapeDtypeStruct((C_HEADS * CB_ENTRIES, GRID_W * LANES), F32),
        compiler_params=pltpu.CompilerParams(vmem_limit_bytes=VMEM_LIMIT),
        name="natten_bias_table",
    )(r2)
    return t.reshape(C_HEADS, CB_ENTRIES, GRID_W, LANES)


NAT_BLOCKS = 4


def _natten_block(blk, q_ref, k_ref, v_ref, kc_ref, vc_ref, cb_ref, o_ref, lo, e):
    rows = slice(blk * NAT_Q, (blk + 1) * NAT_Q)
    r0 = NAT_ROWS * (NAT_BLOCKS * pl.program_id(1) + blk)
    ws = jnp.clip(r0 - WIN_H // 2, 0, GRID_H - NAT_WIN_ROWS)
    koff = pl.multiple_of(ws * GRID_W, LANES)

    entry = []
    rmask = []
    for qr in range(NAT_ROWS):
        r = r0 + qr
        lo_r = jnp.clip(r - WIN_H // 2, 0, GRID_H - WIN_H) - r + (WIN_H - 1)
        ent_q, mask_q = [], []
        for j in range(NAT_TILES):
            dr0 = ws + 2 * j - r + (WIN_H - 1)
            ent_q.append(dr0 + CB_OFFSET)
            dr = dr0 + e
            ok = (dr >= lo_r) & (dr < lo_r + WIN_H)
            mask_q.append(jnp.where(ok, 0.0, NEG))
        entry.append(ent_q)
        rmask.append(mask_q)

    for p in range(C_HEADS // 2):
        sl = slice(p * LANES, (p + 1) * LANES)
        sl2 = slice(2 * p * LANES, (2 * p + 2) * LANES)
        q2 = _split_pair(q_ref[0, rows, sl], lo)
        bias = jnp.concatenate(
            [jnp.concatenate([cb_ref[2 * p + hh, entry[qr][j]] + rmask[qr][j] for j in range(NAT_TILES)], axis=1)
             for hh in range(2) for qr in range(NAT_ROWS)], axis=0)
        kv = [(k_ref[0, pl.ds(koff, NAT_WIN), sl], v_ref[0, pl.ds(koff, NAT_WIN), sl2]),
              (kc_ref[0, :, sl], vc_ref[0, :, sl2])]
        o2 = _softmax_pv(q2, kv, bias)
        o_ref[0, rows, sl] = jnp.where(lo, o2[:NAT_Q], o2[NAT_Q:]).astype(o_ref.dtype)


def _natten_kernel(q_ref, k_ref, v_ref, kc_ref, vc_ref, cb_ref, o_ref):
    lane = lax.broadcasted_iota(jnp.int32, (1, LANES), 1)
    lo = lane < HEAD_DIM
    e = jnp.where(lo, 0, 1)
    for blk in range(NAT_BLOCKS):
        _natten_block(blk, q_ref, k_ref, v_ref, kc_ref, vc_ref, cb_ref, o_ref, lo, e)


def _natten(q, k, v, kc, vc, cb):
    b = q.shape[0]
    tq = NAT_BLOCKS * NAT_Q
    qspec = pl.BlockSpec((1, tq, 512), lambda bb, i: (bb, i, 0))
    full = lambda a: pl.BlockSpec((1,) + a.shape[1:], lambda bb, i: (bb, 0, 0))
    return pl.pallas_call(
        _natten_kernel,
        grid=(b, SEQ // tq),
        in_specs=[qspec, full(k), full(v), full(kc), full(vc),
                  pl.BlockSpec(cb.shape, lambda bb, i: (0, 0, 0, 0))],
        out_specs=qspec,
        out_shape=jax.ShapeDtypeStruct((b, SEQ, 512), BF16),
        compiler_params=_cparams(("parallel", "parallel")),
        name="natten",
    )(q, k, v, kc, vc, cb)


def kernel(x, c, ctx, c_ctx, mod_w, mod_b, ln_g, ln_b, ffn_w1, ffn_w2, ab_w_in, ab_w_out, a_q_norm, a_k_norm,
           b_lambda, b_subln, cd_w_in, cd_w_out, c_rpb):
    b = x.shape[0]
    nctx = b * CTX_LEN
    tm = 512
    tmc = min(tm, nctx)

    cs = jnp.concatenate([c, c_ctx[None, :], jnp.zeros((MOD_ROWS - b - 1, D_MODEL), F32)], axis=0)
    mods = _modulation(cs, mod_w, mod_b)
    ctx_row = b

    cflat = ctx.reshape(1, nctx, D_MODEL)

    perm = np.asarray(QA_PERM)
    w_ab = ab_w_in.reshape(D_MODEL, AB_W).astype(BF16)
    w_out = ab_w_out.reshape(D_MODEL, D_MODEL)
    w_out_ab = jnp.concatenate([w_out[:512].reshape(A_HEADS, HEAD_DIM, D_MODEL)[perm].reshape(512, D_MODEL),
                                w_out[512:]], axis=0).astype(BF16)
    gq = jnp.tile(a_q_norm[0], A_HEADS)[None, :]
    gk = jnp.tile(a_k_norm[0], A_KV_HEADS)[None, :]
    rope_tabs = tuple(jnp.asarray(t) for t in _rope_tables())
    subln = b_subln[0][None, :]
    mod0 = mods[0]

    xq = _inproj_ab(x, mod0, None, w_ab, gq, gk, rope_tabs, 2 * tm, "inproj_ab_x")
    cq = _inproj_ab(cflat, mod0, ctx_row, w_ab, gq, gk, None, tmc, "inproj_ab_ctx")
    cq = [a.reshape(b, CTX_LEN, a.shape[2]) for a in cq]
    qa_x, qb_x, ka_x, kb_x, va_x, vb_x = xq
    qa_c, qb_c, ka_c, kb_c, va_c, vb_c = cq
    c_kv = (ka_c, va_c, kb_c, vb_c)
    o_x, w1, w2, w_cd, w_out_cd = _attn_ab(
        (qa_x, qb_x), (ka_x, va_x, kb_x, vb_x), c_kv, b_lambda[0], subln, 512, "attn_ab_x",
        casts=(ffn_w1.reshape(DEPTH * D_MODEL, FFN_DIM), ffn_w2.reshape(DEPTH * FFN_DIM, D_MODEL),
               cd_w_in.reshape(D_MODEL, CD_W), cd_w_out.reshape(D_MODEL, D_MODEL)))
    w1 = w1.reshape(DEPTH, D_MODEL, FFN_DIM)
    w2 = w2.reshape(DEPTH, FFN_DIM, D_MODEL)
    o_c = _attn_ab((qa_c, qb_c), None, c_kv, b_lambda[0], subln, CTX_LEN, "attn_ab_ctx")

    x2 = _post([o_x], [w_out_ab], x, mod0, None, ln_g[0], ln_b[0], w1, w2, 0, tm, "post0_x")
    c2 = _post([o_c.reshape(1, nctx, D_MODEL)], [w_out_ab], cflat, mod0, ctx_row, ln_g[0], ln_b[0], w1, w2, 0,
               tmc, "post0_ctx")

    mod1 = mods[1]
    w_in = cd_w_in.reshape(D_MODEL, CD_W)
    ct, sn, cc, sc = _dft_tables()
    w_fab = _dft_weights(w_in, jnp.asarray(cc), jnp.asarray(sc))
    q_n, f_a, f_b, k_n, v_n = _inproj_cd(x2, mod1, None, w_cd, w_fab, 2 * tm, "inproj_cd_x")
    kc_n, vc_n = _inproj_cd(c2, mod1, ctx_row, w_cd, None, tmc, "inproj_cd_ctx")
    kc_n = kc_n.reshape(b, CTX_LEN, 512)
    vc_n = vc_n.reshape(b, CTX_LEN, 1024)

    o_d = _dft_time(f_a, f_b, jnp.asarray(ct).astype(BF16), jnp.asarray(sn).astype(BF16))
    cb = _bias_table(c_rpb[0])
    o_n = _natten(q_n, k_n, v_n, kc_n, vc_n, cb)

    return _post([o_n, o_d], [w_out_cd[:512], w_out_cd[512:]], x2, mod1, None, ln_g[1], ln_b[1], w1, w2, 1, tm, "post1_x")
```

```python
import functools
import math

import numpy as np
import jax
import jax.numpy as jnp
from jax import lax
from jax.experimental import pallas as pl
from jax.experimental.pallas import tpu as pltpu

F32 = jnp.float32
BF16 = jnp.bfloat16

D_MODEL = 1024
SEQ = 2048
CTX_LEN = 256
DEPTH = 2
GRID_W = 64
GRID_H = SEQ // GRID_W
HEAD_DIM = 64
A_HEADS = 8
A_KV_HEADS = 2
B_HEADS = 4
C_HEADS = 8
WIN_H = 8
WIN_W = 16
D_GROUPS = 4
D_GROUP_DIM = 128
FFN_DIM = 4 * D_MODEL
ROPE_THETA = 10000.0
ALPHA = (2.0 * DEPTH) ** 0.25
EPS = 1e-6
ATTN_SCALE = HEAD_DIM ** -0.5
LOG2E = math.log2(math.e)
Q_SCALE = ATTN_SCALE * LOG2E
LAM_INIT0 = 0.8 - 0.6 * math.exp(-0.3 * 0)
NEG = -1e30

LANES = 128
MOD_ROWS = 16
VMEM_LIMIT = 56 * 1024 * 1024

QA_PERM = (0, 4, 1, 5, 2, 6, 3, 7)

NAT_ROWS = 4
NAT_Q = NAT_ROWS * GRID_W
NAT_WIN_ROWS = NAT_ROWS + WIN_H
NAT_WIN = NAT_WIN_ROWS * GRID_W
NAT_TILES = NAT_WIN_ROWS // 2
_NAT_DR0 = [min(max(r0 - WIN_H // 2, 0), GRID_H - NAT_WIN_ROWS) + 2 * j - (r0 + qr) + WIN_H - 1
            for r0 in range(0, GRID_H, NAT_ROWS) for qr in range(NAT_ROWS) for j in range(NAT_TILES)]
CB_OFFSET = -min(_NAT_DR0)
CB_ENTRIES = max(_NAT_DR0) + CB_OFFSET + 1


def _cparams(sem, flags=None):
    return pltpu.CompilerParams(dimension_semantics=sem, vmem_limit_bytes=VMEM_LIMIT, flags=flags)


def _dot(a, b):
    return jnp.dot(a, b, preferred_element_type=F32)


def _dot_nt(a, b):
    return lax.dot_general(a, b, (((1,), (1,)), ((), ())), preferred_element_type=F32)


def _split3(a):
    hi = a.astype(BF16)
    r1 = a - hi.astype(F32)
    mid = r1.astype(BF16)
    lo = (r1 - mid.astype(F32)).astype(BF16)
    return hi, mid, lo


def _layer_norm(z, g, b):
    mu = jnp.mean(z, axis=-1, keepdims=True)
    zc = z - mu
    var = jnp.mean(zc * zc, axis=-1, keepdims=True)
    return zc * lax.rsqrt(var + EPS) * g + b


@functools.lru_cache(maxsize=None)
def _rope_tables():
    pos = np.arange(SEQ)
    row = (pos // GRID_W).astype(np.float64)
    col = (pos % GRID_W).astype(np.float64)
    half = HEAD_DIM // 2
    freqs = np.power(ROPE_THETA, -np.arange(0, half, 2, dtype=np.float64) / half)
    def axis_angles(p):
        a = p[:, None] * freqs[None, :]
        return np.concatenate([a, a], axis=-1)
    ang = np.concatenate([axis_angles(row), axis_angles(col)], axis=-1)
    cos, sin = np.cos(ang), np.sin(ang)
    first = (np.arange(HEAD_DIM) % half) < (half // 2)
    sin_a = np.where(first[None, :], -sin, 0.0)
    sin_b = np.where(first[None, :], 0.0, sin)
    tile = lambda t: np.tile(t, (1, LANES // HEAD_DIM)).astype(np.float32)
    return tile(cos), tile(sin_a), tile(sin_b)


@functools.lru_cache(maxsize=None)
def _group_mean_matrix(width):
    g = np.arange(width) // HEAD_DIM
    return (g[:, None] == g[None, :]).astype(np.float32) / HEAD_DIM


@functools.lru_cache(maxsize=None)
def _dft_tables():
    t = np.arange(SEQ, dtype=np.int64)
    k = (t[:, None] * t[None, :]) % SEQ
    ang = 2.0 * np.pi * k.astype(np.float64) / SEQ
    ct = np.cos(ang).astype(np.float32)
    sn = (-np.sin(ang)).astype(np.float32)
    c = np.arange(D_GROUP_DIM, dtype=np.int64)
    kc = (c[:, None] * c[None, :]) % D_GROUP_DIM
    angc = 2.0 * np.pi * kc.astype(np.float64) / D_GROUP_DIM
    norm = 1.0 / math.sqrt(SEQ * D_GROUP_DIM)
    return ct, sn, (np.cos(angc) * norm).astype(np.float32), (np.sin(angc) * norm).astype(np.float32)


def _mod_kernel(c_ref, w_ref, b_ref, o_ref):
    c = c_ref[...]
    a = c / (1.0 + jnp.exp(-c))
    a_hi, a_mid, _ = _split3(a)
    y = _dot(jnp.concatenate([a_hi, a_mid], axis=0), w_ref[0].astype(BF16))
    o_ref[0, 0] = y[:MOD_ROWS] + y[MOD_ROWS:] + b_ref[0]


def _modulation(cs, mod_w, mod_b):
    n = 6 * D_MODEL
    return pl.pallas_call(
        _mod_kernel,
        grid=(DEPTH, 6),
        in_specs=[pl.BlockSpec((MOD_ROWS, D_MODEL), lambda l, j: (0, 0)),
                  pl.BlockSpec((1, D_MODEL, D_MODEL), lambda l, j: (l, 0, j)),
                  pl.BlockSpec((1, 1, D_MODEL), lambda l, j: (l, 0, j))],
        out_specs=pl.BlockSpec((1, 1, MOD_ROWS, D_MODEL), lambda l, j: (l, j, 0, 0)),
        out_shape=jax.ShapeDtypeStruct((DEPTH, 6, MOD_ROWS, D_MODEL), F32),
        compiler_params=_cparams(("arbitrary", "arbitrary")),
        name="modulation",
    )(cs, mod_w, mod_b.reshape(DEPTH, 1, n))


def _tok_spec(tm, width):
    return pl.BlockSpec((1, tm, width), lambda g, i: (g, i, 0))


MOD_SPEC = pl.BlockSpec((6, MOD_ROWS, D_MODEL), lambda g, i: (0, 0, 0))


def _mod_vectors(mod_ref, mod_row):
    r = pl.program_id(0) if mod_row is None else mod_row
    return [mod_ref[k, pl.ds(r, 1), :] for k in range(6)]


def _const_spec(shape):
    nd = len(shape)
    return pl.BlockSpec(shape, lambda g, i: (0,) * nd)


AB_QA, AB_QB, AB_KA, AB_VA, AB_KB, AB_VB, AB_W = 0, 512, 1024, 1152, 1280, 1792, 2304


TOKEN_CHAIN_ROWS = 256


def _store_v_ext(o_ref, rows, v):
    ones = jnp.ones((v.shape[0], LANES), o_ref.dtype)
    for j in range(v.shape[1] // LANES):
        o_ref[0, rows, 2 * j * LANES:(2 * j + 1) * LANES] = v[:, j * LANES:(j + 1) * LANES].astype(o_ref.dtype)
        o_ref[0, rows, (2 * j + 1) * LANES:(2 * j + 2) * LANES] = ones


def _rope_chunk(c, cos, sin_a, sin_b):
    return c * cos + pltpu.roll(c, LANES - 16, axis=1) * sin_a + pltpu.roll(c, 16, axis=1) * sin_b


def _inproj_ab_kernel(*refs, rope, mod_row):
    if rope:
        (x_ref, mod_ref, w_ref, gq_ref, gk_ref, g512_ref, g128_ref, cos_ref, sa_ref, sb_ref,
         qa_ref, qb_ref, ka_ref, kb_ref, va_ref, vb_ref) = refs
    else:
        (x_ref, mod_ref, w_ref, gq_ref, gk_ref, g512_ref, g128_ref,
         qa_ref, qb_ref, ka_ref, kb_ref, va_ref, vb_ref) = refs

    def rms(v, g_ref, gm_ref):
        ms = _dot((v * v).astype(BF16), gm_ref[...])
        return v * lax.rsqrt(ms + EPS) * g_ref[...]

    shift, scale = _mod_vectors(mod_ref, mod_row)[0:2]
    lo = lax.broadcasted_iota(jnp.int32, (1, LANES), 1) < HEAD_DIM
    tm = x_ref.shape[1]
    tc = min(TOKEN_CHAIN_ROWS, tm)

    def project(rb):
        rows = slice(rb * tc, (rb + 1) * tc)
        u = (x_ref[0, rows, :] * (1.0 + scale) + shift).astype(BF16)
        return _dot(u, w_ref[...])

    def epilogue(rb, h):
        rows = slice(rb * tc, (rb + 1) * tc)
        if rope:
            cos, sin_a, sin_b = cos_ref[rows, :], sa_ref[rows, :], sb_ref[rows, :]

        def chunks(v, q_scale):
            out = []
            for j in range(v.shape[1] // LANES):
                c = v[:, j * LANES:(j + 1) * LANES]
                if rope:
                    c = _rope_chunk(c, cos, sin_a, sin_b)
                out.append(c * q_scale if q_scale != 1.0 else c)
            return out

        def store(o_ref, cs):
            for j, c in enumerate(cs):
                o_ref[0, rows, j * LANES:(j + 1) * LANES] = c.astype(o_ref.dtype)

        qa = chunks(rms(h[:, AB_QA:AB_QB], gq_ref, g512_ref), Q_SCALE)
        qa_perm = []
        for p in range(A_HEADS // 2):
            a, b = qa[p // 2], qa[A_HEADS // 4 + p // 2]
            if p % 2 == 0:
                qa_perm.append(jnp.where(lo, a, pltpu.roll(b, HEAD_DIM, axis=1)))
            else:
                qa_perm.append(jnp.where(lo, pltpu.roll(a, HEAD_DIM, axis=1), b))
        store(qa_ref, qa_perm)
        store(qb_ref, chunks(h[:, AB_QB:AB_KA], Q_SCALE))
        store(ka_ref, chunks(rms(h[:, AB_KA:AB_VA], gk_ref, g128_ref), 1.0))
        store(kb_ref, chunks(h[:, AB_KB:AB_VB], 1.0))
        _store_v_ext(va_ref, rows, h[:, AB_VA:AB_KB])
        _store_v_ext(vb_ref, rows, h[:, AB_VB:AB_W])

    h = project(0)
    for rb in range(1, tm // tc):
        h_next = project(rb)
        epilogue(rb - 1, h)
        h = h_next
    epilogue(tm // tc - 1, h)


def _inproj_ab(tok, mod, mod_row, w, gq, gk, rope_tabs, tm, name):
    g, r, _ = tok.shape
    rope = rope_tabs is not None
    consts = [w, gq, gk, jnp.asarray(_group_mean_matrix(512), BF16), jnp.asarray(_group_mean_matrix(128), BF16)]
    in_specs = [_tok_spec(tm, D_MODEL), MOD_SPEC] + [_const_spec(c.shape) for c in consts]
    args = [tok, mod] + consts
    if rope:
        in_specs += [pl.BlockSpec((tm, LANES), lambda gg, i: (i, 0))] * 3
        args += list(rope_tabs)
    widths = (512, 512, 128, 512, 2 * 128, 2 * 512)
    return pl.pallas_call(
        functools.partial(_inproj_ab_kernel, rope=rope, mod_row=mod_row),
        grid=(g, r // tm),
        in_specs=in_specs,
        out_specs=[_tok_spec(tm, wd) for wd in widths],
        out_shape=[jax.ShapeDtypeStruct((g, r, wd), BF16) for wd in widths],
        compiler_params=_cparams(("parallel", "parallel")),
        name=name,
    )(*args)


def _softmax_pv(q2, kv_pairs, bias=None):
    scores = [_dot_nt(q2, k) for k, _ in kv_pairs]
    if bias is not None:
        scores[0] = scores[0] + bias
    m = scores[0].max(axis=-1, keepdims=True)
    for s in scores[1:]:
        m = jnp.maximum(m, s.max(axis=-1, keepdims=True))
    acc = None
    for s, (_, v) in zip(scores, kv_pairs):
        a = _dot(jnp.exp2(s - m).astype(BF16), v)
        acc = a if acc is None else acc + a
    return acc[:, :LANES] / acc[:, LANES:]


def _split_pair(qp, lo):
    zero = jnp.zeros_like(qp)
    return jnp.concatenate([jnp.where(lo, qp, zero), jnp.where(lo, zero, qp)], axis=0)


ATTN_CHAIN_ROWS = 256


def _chain_schedule(rows, n_pairs, chain_rows):
    blocks = [(r0, min(chain_rows, rows - r0)) for r0 in range(0, rows, chain_rows)]
    sched = [(p, r0, n) for r0, n in blocks for p in range(n_pairs)]
    if len(blocks) < 2:
        return sched
    (p, r0, n), rest = sched[0], sched[1:]
    sched = [(p, r0, n // 2), (p, r0 + n // 2, n - n // 2)] + rest
    (p, r0, n), rest = sched[-1], sched[:-1]
    return rest + [(p, r0, n - n // 2), (p, r0 + n - n // 2, n // 2)]


def _attn_ab_kernel(*refs, with_x, n_casts):
    if n_casts:
        cast_in = refs[len(refs) - 2 * n_casts - 1:len(refs) - n_casts - 1]
        cast_out = refs[len(refs) - n_casts:]
        refs = refs[:len(refs) - 2 * n_casts - 1] + (refs[len(refs) - n_casts - 1],)
        for src, dst in zip(cast_in, cast_out):
            dst[...] = src[...].astype(dst.dtype)
    if with_x:
        (qa_ref, qb_ref, kax_ref, vax_ref, kbx_ref, vbx_ref, kac_ref, vac_ref, kbc_ref, vbc_ref,
         lam_ref, subln_ref, o_ref) = refs
    else:
        (qa_ref, qb_ref, kac_ref, vac_ref, kbc_ref, vbc_ref, lam_ref, subln_ref, o_ref) = refs
    lo = lax.broadcasted_iota(jnp.int32, (1, LANES), 1) < HEAD_DIM

    lp = lam_ref[...]
    lam = (jnp.exp(jnp.sum(lp[0:1] * lp[1:2], axis=-1, keepdims=True))
           - jnp.exp(jnp.sum(lp[2:3] * lp[3:4], axis=-1, keepdims=True)) + LAM_INIT0)

    def gqa_chain(p, r0, n):
        rows = slice(r0, r0 + n)
        sl = slice(p * LANES, (p + 1) * LANES)
        q2 = _split_pair(qa_ref[0, rows, sl], lo)
        kv = [(kac_ref[0], vac_ref[0])]
        if with_x:
            kv.append((kax_ref[0], vax_ref[0]))
        o2 = _softmax_pv(q2, kv)
        o_ref[0, rows, sl] = jnp.where(lo, o2[:n], o2[n:]).astype(o_ref.dtype)

    def diff_chain(i, r0, n):
        rows = slice(r0, r0 + n)
        sl = slice(i * LANES, (i + 1) * LANES)
        sl2 = slice(2 * i * LANES, (2 * i + 2) * LANES)
        q2 = _split_pair(qb_ref[0, rows, sl], lo)
        kv = [(kbc_ref[0, :, sl], vbc_ref[0, :, sl2])]
        if with_x:
            kv.append((kbx_ref[0, :, sl], vbx_ref[0, :, sl2]))
        o2 = _softmax_pv(q2, kv)
        o = o2[:n] - lam * o2[n:]
        ms = jnp.mean(o * o, axis=-1, keepdims=True)
        o = o * lax.rsqrt(ms + EPS) * subln_ref[...] * (1.0 - LAM_INIT0)
        o_ref[0, rows, 512 + i * LANES:512 + (i + 1) * LANES] = o.astype(o_ref.dtype)

    for pair, r0, n in _chain_schedule(qa_ref.shape[1], A_HEADS // 2 + B_HEADS, ATTN_CHAIN_ROWS):
        if pair < A_HEADS // 2:
            gqa_chain(pair, r0, n)
        else:
            diff_chain(pair - A_HEADS // 2, r0, n)


def _attn_ab(q_parts, x_kv, c_kv, lam_p, subln, tq, name, casts=()):
    qa, qb = q_parts
    b, nq, _ = qa.shape
    with_x = x_kv is not None
    steps = nq // tq

    def full(a):
        return pl.BlockSpec((1,) + a.shape[1:], lambda bb, i: (bb, 0, 0))

    args = [qa, qb]
    in_specs = [pl.BlockSpec((1, tq, 512), lambda bb, i: (bb, i, 0))] * 2
    if with_x:
        args += list(x_kv)
        in_specs += [full(a) for a in x_kv]
    args += list(c_kv) + [lam_p, subln]
    in_specs += [full(a) for a in c_kv] + [pl.BlockSpec(lam_p.shape, lambda bb, i: (0, 0)),
                                           pl.BlockSpec(subln.shape, lambda bb, i: (0, 0))]
    cast_specs = [pl.BlockSpec((a.shape[0] // (b * steps), a.shape[1]), lambda bb, i: (bb * steps + i, 0))
                  for a in casts]
    outs = pl.pallas_call(
        functools.partial(_attn_ab_kernel, with_x=with_x, n_casts=len(casts)),
        grid=(b, steps),
        in_specs=in_specs + cast_specs,
        out_specs=[pl.BlockSpec((1, tq, D_MODEL), lambda bb, i: (bb, i, 0))] + cast_specs,
        out_shape=[jax.ShapeDtypeStruct((b, nq, D_MODEL), BF16)]
                  + [jax.ShapeDtypeStruct(a.shape, BF16) for a in casts],
        compiler_params=_cparams(("parallel", "parallel")),
        name=name,
    )(*args, *casts)
    return outs[0] if not casts else outs


FFN_CHUNK = 1024
POST_EDGE_PIECES = 2


def _post_kernel(*refs, n_parts, mod_row):
    o_refs = refs[:n_parts]
    wo_refs = refs[n_parts:2 * n_parts]
    x_ref, mod_ref, g_ref, b_ref, w1_ref, w2_ref, y_ref = refs[2 * n_parts:]
    _, _, gate1, shift2, scale2, gate2 = _mod_vectors(mod_ref, mod_row)
    tm = x_ref.shape[1]
    n_chunks = FFN_DIM // FFN_CHUNK
    pieces = [slice(r, r + tm // POST_EDGE_PIECES) for r in range(0, tm, tm // POST_EDGE_PIECES)]
    sq_relu = lambda t: jnp.square(jnp.maximum(t, 0.0)).astype(BF16)

    ys = []
    for rows in pieces:
        y = _dot(o_refs[0][0, rows, :], wo_refs[0][...])
        for o_ref, w_ref in zip(o_refs[1:], wo_refs[1:]):
            y = y + _dot(o_ref[0, rows, :], w_ref[...])
        ys.append(y)
    x1s, us, h0s = [], [], []
    for k, (rows, y) in enumerate(zip(pieces, ys)):
        x1s.append(_layer_norm(ALPHA * x_ref[0, rows, :] + gate1 * y, g_ref[0:1, :], b_ref[0:1, :]))
        us.append((x1s[-1] * (1.0 + scale2) + shift2).astype(BF16))
        if k % 2 == 1:
            h0s.append(sq_relu(_dot(jnp.concatenate(us[k - 1:k + 1], axis=0), w1_ref[0, :, 0:FFN_CHUNK])))
    u = jnp.concatenate(us, axis=0)
    acc = _dot(jnp.concatenate(h0s, axis=0), w2_ref[0, 0:FFN_CHUNK, :])
    for c in range(1, n_chunks - 1):
        sl = slice(c * FFN_CHUNK, (c + 1) * FFN_CHUNK)
        acc = acc + _dot(sq_relu(_dot(u, w1_ref[0, :, sl])), w2_ref[0, sl, :])
    sl = slice((n_chunks - 1) * FFN_CHUNK, n_chunks * FFN_CHUNK)
    h_last = sq_relu(_dot(u, w1_ref[0, :, sl]))
    fs = [acc[rows] + _dot(h_last[rows], w2_ref[0, sl, :]) for rows in pieces]
    for rows, x1, f in zip(pieces, x1s, fs):
        y_ref[0, rows, :] = _layer_norm(ALPHA * x1 + gate2 * f, g_ref[1:2, :], b_ref[1:2, :])


def _post(o_parts, wo_parts, resid, mod, mod_row, ln_g, ln_b, w1, w2, layer, tm, name):
    g, r, _ = resid.shape
    n = len(o_parts)
    single = pl.Buffered(1)
    layer_w = lambda a: pl.BlockSpec((1,) + a.shape[1:], lambda gg, i: (layer, 0, 0), pipeline_mode=single)
    const1 = lambda a: pl.BlockSpec(a.shape, lambda gg, i: (0,) * a.ndim, pipeline_mode=single)
    in_specs = ([_tok_spec(tm, o.shape[2]) for o in o_parts] + [const1(w) for w in wo_parts]
                + [_tok_spec(tm, D_MODEL), MOD_SPEC, _const_spec(ln_g.shape), _const_spec(ln_b.shape),
                   layer_w(w1), layer_w(w2)])
    return pl.pallas_call(
        functools.partial(_post_kernel, n_parts=n, mod_row=mod_row),
        grid=(g, r // tm),
        in_specs=in_specs,
        out_specs=_tok_spec(tm, D_MODEL),
        out_shape=jax.ShapeDtypeStruct((g, r, D_MODEL), F32),
        compiler_params=_cparams(("parallel", "parallel")),
        name=name,
    )(*o_parts, *wo_parts, resid, mod, ln_g, ln_b, w1, w2)


CD_Q, CD_F, CD_K, CD_V, CD_W = 0, 512, 1024, 1536, 2048


def _inproj_cd_kernel(*refs, with_q, mod_row):
    if with_q:
        x_ref, mod_ref, w_ref, wa_ref, wb_ref, q_ref, fa_ref, fb_ref, k_ref, v_ref = refs
    else:
        x_ref, mod_ref, w_ref, k_ref, v_ref = refs
    shift, scale = _mod_vectors(mod_ref, mod_row)[0:2]
    u = (x_ref[0] * (1.0 + scale) + shift).astype(BF16)
    if with_q:
        q_ref[0] = (_dot(u, w_ref[:, CD_Q:CD_F]) * Q_SCALE).astype(q_ref.dtype)
        fa_ref[0] = _dot(u, wa_ref[...]).astype(fa_ref.dtype)
        fb_ref[0] = _dot(u, wb_ref[...]).astype(fb_ref.dtype)
    kv = _dot(u, w_ref[:, CD_K:CD_W])
    k_ref[0] = kv[:, :CD_V - CD_K].astype(k_ref.dtype)
    _store_v_ext(v_ref, slice(None), kv[:, CD_V - CD_K:])


def _inproj_cd(tok, mod, mod_row, w, w_ab, tm, name):
    g, r, _ = tok.shape
    with_q = w_ab is not None
    weights = [w] + (list(w_ab) if with_q else [])
    widths = ((512, 512, 512) if with_q else ()) + (512, 1024)
    return pl.pallas_call(
        functools.partial(_inproj_cd_kernel, with_q=with_q, mod_row=mod_row),
        grid=(g, r // tm),
        in_specs=[_tok_spec(tm, D_MODEL), MOD_SPEC] + [_const_spec(a.shape) for a in weights],
        out_specs=[_tok_spec(tm, wd) for wd in widths],
        out_shape=[jax.ShapeDtypeStruct((g, r, wd), BF16) for wd in widths],
        compiler_params=_cparams(("parallel", "parallel")),
        name=name,
    )(tok, mod, *weights)


def _dft_weight_kernel(w_ref, cc_ref, sc_ref, wa_ref, wb_ref):
    c_hi, c_mid, _ = _split3(cc_ref[...])
    s_hi, s_mid, _ = _split3(sc_ref[...])
    for gi in range(D_GROUPS):
        sl = slice(gi * D_GROUP_DIM, (gi + 1) * D_GROUP_DIM)
        w_hi, w_mid, _ = _split3(w_ref[:, sl])
        wa_ref[:, sl] = (_dot(w_hi, c_hi) + (_dot(w_mid, c_hi) + _dot(w_hi, c_mid))).astype(BF16)
        wb_ref[:, sl] = (_dot(w_hi, s_hi) + (_dot(w_mid, s_hi) + _dot(w_hi, s_mid))).astype(BF16)


def _dft_weights(w_in, cc, sc):
    width = CD_K - CD_F
    out = pl.BlockSpec((D_MODEL, width), lambda i: (0, 0))
    return pl.pallas_call(
        _dft_weight_kernel,
        grid=(1,),
        in_specs=[pl.BlockSpec((D_MODEL, width), lambda i: (0, CD_F // width)),
                  pl.BlockSpec(cc.shape, lambda i: (0, 0)), pl.BlockSpec(sc.shape, lambda i: (0, 0))],
        out_specs=[out, out],
        out_shape=[jax.ShapeDtypeStruct((D_MODEL, width), BF16)] * 2,
        compiler_params=_cparams(("arbitrary",)),
        name="dft_weights",
    )(w_in, cc, sc)


DFT_ROWS = 512


def _dft_time_kernel(a_ref, b_ref, ct_ref, sn_ref, o_ref):
    a = a_ref[0]
    bm = b_ref[0]
    for r in range(SEQ // DFT_ROWS):
        sl = slice(r * DFT_ROWS, (r + 1) * DFT_ROWS)
        y = _dot(ct_ref[sl, :], a) + _dot(sn_ref[sl, :], bm)
        o_ref[0, sl, :] = y.astype(o_ref.dtype)


def _dft_time(a, bm, ct, sn):
    b = a.shape[0]
    single = pl.Buffered(1)
    tok = pl.BlockSpec((1, SEQ, 512), lambda bb: (bb, 0, 0))
    tab = pl.BlockSpec((SEQ, SEQ), lambda bb: (0, 0), pipeline_mode=single)
    return pl.pallas_call(
        _dft_time_kernel,
        grid=(b,),
        in_specs=[tok, tok, tab, tab],
        out_specs=tok,
        out_shape=jax.ShapeDtypeStruct((b, SEQ, 512), BF16),
        compiler_params=_cparams(("parallel",)),
        name="dft_time",
    )(a, bm, ct, sn)


def _bias_table_kernel(r_ref, o_ref):
    n = GRID_W * LANES
    row = lax.broadcasted_iota(jnp.int32, (2 * 32, n), 0)
    lane = lax.broadcasted_iota(jnp.int32, (2 * 32, n), 1)
    e_row, i_row = row >> 5, row & 31
    c = lane >> 7
    e_lane = (lane >> 6) & 1
    kc = lane & (GRID_W - 1)
    sel = jnp.where((e_row == e_lane) & (kc - c + (WIN_W - 1) == i_row), 1.0, 0.0).astype(BF16)
    hi, mid, lo = _split3(r_ref[...])
    t = _dot(hi, sel) + _dot(mid, sel) + _dot(lo, sel)
    c1 = c[0:1]
    kc1 = kc[0:1]
    cstart = jnp.clip(c1 - WIN_W // 2, 0, GRID_W - WIN_W)
    col_ok = (kc1 >= cstart) & (kc1 < cstart + WIN_W)
    o_ref[...] = jnp.where(col_ok, t * LOG2E, NEG)


def _bias_table(rpb):
    n_dr = 2 * WIN_H - 1
    p = jnp.pad(rpb, ((0, 0), (CB_OFFSET, CB_ENTRIES + 1 - n_dr - CB_OFFSET), (0, 1)))
    r2 = jnp.stack([p[:, 0:CB_ENTRIES], p[:, 1:CB_ENTRIES + 1]], axis=2)
    r2 = r2.reshape(C_HEADS * CB_ENTRIES, 2 * 32)
    t = pl.pallas_call(
        _bias_table_kernel,
        out_shape=jax.ShapeDtypeStruct((C_HEADS * CB_ENTRIES, GRID_W * LANES), F32),
        compiler_params=pltpu.CompilerParams(vmem_limit_bytes=VMEM_LIMIT),
        name="natten_bias_table",
    )(r2)
    return t.reshape(C_HEADS, CB_ENTRIES, GRID_W, LANES)


NAT_BLOCKS = 4


def _natten_block(blk, q_ref, k_ref, v_ref, kc_ref, vc_ref, cb_ref, o_ref, lo, e):
    rows = slice(blk * NAT_Q, (blk + 1) * NAT_Q)
    r0 = NAT_ROWS * (NAT_BLOCKS * pl.program_id(1) + blk)
    ws = jnp.clip(r0 - WIN_H // 2, 0, GRID_H - NAT_WIN_ROWS)
    koff = pl.multiple_of(ws * GRID_W, LANES)

    entry = []
    rmask = []
    for qr in range(NAT_ROWS):
        r = r0 + qr
        lo_r = jnp.clip(r - WIN_H // 2, 0, GRID_H - WIN_H) - r + (WIN_H - 1)
        ent_q, mask_q = [], []
        for j in range(NAT_TILES):
            dr0 = ws + 2 * j - r + (WIN_H - 1)
            ent_q.append(dr0 + CB_OFFSET)
            dr = dr0 + e
            ok = (dr >= lo_r) & (dr < lo_r + WIN_H)
            mask_q.append(jnp.where(ok, 0.0, NEG))
        entry.append(ent_q)
        rmask.append(mask_q)

    for p in range(C_HEADS // 2):
        sl = slice(p * LANES, (p + 1) * LANES)
        sl2 = slice(2 * p * LANES, (2 * p + 2) * LANES)
        q2 = _split_pair(q_ref[0, rows, sl], lo)
        bias = jnp.concatenate(
            [jnp.concatenate([cb_ref[2 * p + hh, entry[qr][j]] + rmask[qr][j] for j in range(NAT_TILES)], axis=1)
             for hh in range(2) for qr in range(NAT_ROWS)], axis=0)
        kv = [(k_ref[0, pl.ds(koff, NAT_WIN), sl], v_ref[0, pl.ds(koff, NAT_WIN), sl2]),
              (kc_ref[0, :, sl], vc_ref[0, :, sl2])]
        o2 = _softmax_pv(q2, kv, bias)
        o_ref[0, rows, sl] = jnp.where(lo, o2[:NAT_Q], o2[NAT_Q:]).astype(o_ref.dtype)


def _natten_kernel(q_ref, k_ref, v_ref, kc_ref, vc_ref, cb_ref, o_ref):
    lane = lax.broadcasted_iota(jnp.int32, (1, LANES), 1)
    lo = lane < HEAD_DIM
    e = jnp.where(lo, 0, 1)
    for blk in range(NAT_BLOCKS):
        _natten_block(blk, q_ref, k_ref, v_ref, kc_ref, vc_ref, cb_ref, o_ref, lo, e)


def _natten(q, k, v, kc, vc, cb):
    b = q.shape[0]
    tq = NAT_BLOCKS * NAT_Q
    qspec = pl.BlockSpec((1, tq, 512), lambda bb, i: (bb, i, 0))
    full = lambda a: pl.BlockSpec((1,) + a.shape[1:], lambda bb, i: (bb, 0, 0))
    return pl.pallas_call(
        _natten_kernel,
        grid=(b, SEQ // tq),
        in_specs=[qspec, full(k), full(v), full(kc), full(vc),
                  pl.BlockSpec(cb.shape, lambda bb, i: (0, 0, 0, 0))],
        out_specs=qspec,
        out_shape=jax.ShapeDtypeStruct((b, SEQ, 512), BF16),
        compiler_params=_cparams(("parallel", "parallel")),
        name="natten",
    )(q, k, v, kc, vc, cb)


def kernel(x, c, ctx, c_ctx, mod_w, mod_b, ln_g, ln_b, ffn_w1, ffn_w2, ab_w_in, ab_w_out, a_q_norm, a_k_norm,
           b_lambda, b_subln, cd_w_in, cd_w_out, c_rpb):
    b = x.shape[0]
    nctx = b * CTX_LEN
    tm = 512
    tmc = min(tm, nctx)

    cs = jnp.concatenate([c, c_ctx[None, :], jnp.zeros((MOD_ROWS - b - 1, D_MODEL), F32)], axis=0)
    mods = _modulation(cs, mod_w, mod_b)
    ctx_row = b

    cflat = ctx.reshape(1, nctx, D_MODEL)

    perm = np.asarray(QA_PERM)
    w_ab = ab_w_in.reshape(D_MODEL, AB_W).astype(BF16)
    w_out = ab_w_out.reshape(D_MODEL, D_MODEL)
    w_out_ab = jnp.concatenate([w_out[:512].reshape(A_HEADS, HEAD_DIM, D_MODEL)[perm].reshape(512, D_MODEL),
                                w_out[512:]], axis=0).astype(BF16)
    gq = jnp.tile(a_q_norm[0], A_HEADS)[None, :]
    gk = jnp.tile(a_k_norm[0], A_KV_HEADS)[None, :]
    rope_tabs = tuple(jnp.asarray(t) for t in _rope_tables())
    subln = b_subln[0][None, :]
    mod0 = mods[0]

    xq = _inproj_ab(x, mod0, None, w_ab, gq, gk, rope_tabs, 2 * tm, "inproj_ab_x")
    cq = _inproj_ab(cflat, mod0, ctx_row, w_ab, gq, gk, None, tmc, "inproj_ab_ctx")
    cq = [a.reshape(b, CTX_LEN, a.shape[2]) for a in cq]
    qa_x, qb_x, ka_x, kb_x, va_x, vb_x = xq
    qa_c, qb_c, ka_c, kb_c, va_c, vb_c = cq
    c_kv = (ka_c, va_c, kb_c, vb_c)
    o_x, w1, w2, w_cd, w_out_cd = _attn_ab(
        (qa_x, qb_x), (ka_x, va_x, kb_x, vb_x), c_kv, b_lambda[0], subln, 512, "attn_ab_x",
        casts=(ffn_w1.reshape(DEPTH * D_MODEL, FFN_DIM), ffn_w2.reshape(DEPTH * FFN_DIM, D_MODEL),
               cd_w_in.reshape(D_MODEL, CD_W), cd_w_out.reshape(D_MODEL, D_MODEL)))
    w1 = w1.reshape(DEPTH, D_MODEL, FFN_DIM)
    w2 = w2.reshape(DEPTH, FFN_DIM, D_MODEL)
    o_c = _attn_ab((qa_c, qb_c), None, c_kv, b_lambda[0], subln, CTX_LEN, "attn_ab_ctx")

    x2 = _post([o_x], [w_out_ab], x, mod0, None, ln_g[0], ln_b[0], w1, w2, 0, tm, "post0_x")
    c2 = _post([o_c.reshape(1, nctx, D_MODEL)], [w_out_ab], cflat, mod0, ctx_row, ln_g[0], ln_b[0], w1, w2, 0,
               tmc, "post0_ctx")

    mod1 = mods[1]
    w_in = cd_w_in.reshape(D_MODEL, CD_W)
    ct, sn, cc, sc = _dft_tables()
    w_fab = _dft_weights(w_in, jnp.asarray(cc), jnp.asarray(sc))
    q_n, f_a, f_b, k_n, v_n = _inproj_cd(x2, mod1, None, w_cd, w_fab, 2 * tm, "inproj_cd_x")
    kc_n, vc_n = _inproj_cd(c2, mod1, ctx_row, w_cd, None, tmc, "inproj_cd_ctx")
    kc_n = kc_n.reshape(b, CTX_LEN, 512)
    vc_n = vc_n.reshape(b, CTX_LEN, 1024)

    o_d = _dft_time(f_a, f_b, jnp.asarray(ct).astype(BF16), jnp.asarray(sn).astype(BF16))
    cb = _bias_table(c_rpb[0])
    o_n = _natten(q_n, k_n, v_n, kc_n, vc_n, cb)

    return _post([o_n, o_d], [w_out_cd[:512], w_out_cd[512:]], x2, mod1, None, ln_g[1], ln_b[1], w1, w2, 1, tm, "post1_x")
```

```python
import functools
import math

import numpy as np
import jax
import jax.numpy as jnp
from jax import lax
from jax.experimental import pallas as pl
from jax.experimental.pallas import tpu as pltpu

F32 = jnp.float32
BF16 = jnp.bfloat16

D_MODEL = 1024
SEQ = 2048
CTX_LEN = 256
DEPTH = 2
GRID_W = 64
GRID_H = SEQ // GRID_W
HEAD_DIM = 64
A_HEADS = 8
A_KV_HEADS = 2
B_HEADS = 4
C_HEADS = 8
WIN_H = 8
WIN_W = 16
D_GROUPS = 4
D_GROUP_DIM = 128
FFN_DIM = 4 * D_MODEL
ROPE_THETA = 10000.0
ALPHA = (2.0 * DEPTH) ** 0.25
EPS = 1e-6
ATTN_SCALE = HEAD_DIM ** -0.5
LOG2E = math.log2(math.e)
Q_SCALE = ATTN_SCALE * LOG2E
LAM_INIT0 = 0.8 - 0.6 * math.exp(-0.3 * 0)
NEG = -1e30

LANES = 128
MOD_ROWS = 16
VMEM_LIMIT = 56 * 1024 * 1024

QA_PERM = (0, 4, 1, 5, 2, 6, 3, 7)

NAT_ROWS = 4
NAT_Q = NAT_ROWS * GRID_W
NAT_WIN_ROWS = NAT_ROWS + WIN_H
NAT_WIN = NAT_WIN_ROWS * GRID_W
NAT_TILES = NAT_WIN_ROWS // 2
_NAT_DR0 = [min(max(r0 - WIN_H // 2, 0), GRID_H - NAT_WIN_ROWS) + 2 * j - (r0 + qr) + WIN_H - 1
            for r0 in range(0, GRID_H, NAT_ROWS) for qr in range(NAT_ROWS) for j in range(NAT_TILES)]
CB_OFFSET = -min(_NAT_DR0)
CB_ENTRIES = max(_NAT_DR0) + CB_OFFSET + 1


def _cparams(sem, flags=None):
    return pltpu.CompilerParams(dimension_semantics=sem, vmem_limit_bytes=VMEM_LIMIT, flags=flags)


def _dot(a, b):
    return jnp.dot(a, b, preferred_element_type=F32)


def _dot_nt(a, b):
    return lax.dot_general(a, b, (((1,), (1,)), ((), ())), preferred_element_type=F32)


def _split3(a):
    hi = a.astype(BF16)
    r1 = a - hi.astype(F32)
    mid = r1.astype(BF16)
    lo = (r1 - mid.astype(F32)).astype(BF16)
    return hi, mid, lo


def _layer_norm(z, g, b):
    mu = jnp.mean(z, axis=-1, keepdims=True)
    zc = z - mu
    var = jnp.mean(zc * zc, axis=-1, keepdims=True)
    return zc * lax.rsqrt(var + EPS) * g + b


@functools.lru_cache(maxsize=None)
def _rope_tables():
    pos = np.arange(SEQ)
    row = (pos // GRID_W).astype(np.float64)
    col = (pos % GRID_W).astype(np.float64)
    half = HEAD_DIM // 2
    freqs = np.power(ROPE_THETA, -np.arange(0, half, 2, dtype=np.float64) / half)
    def axis_angles(p):
        a = p[:, None] * freqs[None, :]
        return np.concatenate([a, a], axis=-1)
    ang = np.concatenate([axis_angles(row), axis_angles(col)], axis=-1)
    cos, sin = np.cos(ang), np.sin(ang)
    first = (np.arange(HEAD_DIM) % half) < (half // 2)
    sin_a = np.where(first[None, :], -sin, 0.0)
    sin_b = np.where(first[None, :], 0.0, sin)
    tile = lambda t: np.tile(t, (1, LANES // HEAD_DIM)).astype(np.float32)
    return tile(cos), tile(sin_a), tile(sin_b)


@functools.lru_cache(maxsize=None)
def _group_mean_matrix(width):
    g = np.arange(width) // HEAD_DIM
    return (g[:, None] == g[None, :]).astype(np.float32) / HEAD_DIM


@functools.lru_cache(maxsize=None)
def _dft_tables():
    t = np.arange(SEQ, dtype=np.int64)
    k = (t[:, None] * t[None, :]) % SEQ
    ang = 2.0 * np.pi * k.astype(np.float64) / SEQ
    ct = np.cos(ang).astype(np.float32)
    sn = (-np.sin(ang)).astype(np.float32)
    c = np.arange(D_GROUP_DIM, dtype=np.int64)
    kc = (c[:, None] * c[None, :]) % D_GROUP_DIM
    angc = 2.0 * np.pi * kc.astype(np.float64) / D_GROUP_DIM
    norm = 1.0 / math.sqrt(SEQ * D_GROUP_DIM)
    return ct, sn, (np.cos(angc) * norm).astype(np.float32), (np.sin(angc) * norm).astype(np.float32)


def _mod_kernel(c_ref, w_ref, b_ref, o_ref):
    c = c_ref[...]
    a = c / (1.0 + jnp.exp(-c))
    a_hi, a_mid, _ = _split3(a)
    y = _dot(jnp.concatenate([a_hi, a_mid], axis=0), w_ref[0].astype(BF16))
    o_ref[0, 0] = y[:MOD_ROWS] + y[MOD_ROWS:] + b_ref[0]


def _modulation(cs, mod_w, mod_b):
    n = 6 * D_MODEL
    return pl.pallas_call(
        _mod_kernel,
        grid=(DEPTH, 6),
        in_specs=[pl.BlockSpec((MOD_ROWS, D_MODEL), lambda l, j: (0, 0)),
                  pl.BlockSpec((1, D_MODEL, D_MODEL), lambda l, j: (l, 0, j)),
                  pl.BlockSpec((1, 1, D_MODEL), lambda l, j: (l, 0, j))],
        out_specs=pl.BlockSpec((1, 1, MOD_ROWS, D_MODEL), lambda l, j: (l, j, 0, 0)),
        out_shape=jax.ShapeDtypeStruct((DEPTH, 6, MOD_ROWS, D_MODEL), F32),
        compiler_params=_cparams(("arbitrary", "arbitrary")),
        name="modulation",
    )(cs, mod_w, mod_b.reshape(DEPTH, 1, n))


def _tok_spec(tm, width):
    return pl.BlockSpec((1, tm, width), lambda g, i: (g, i, 0))


MOD_SPEC = pl.BlockSpec((6, MOD_ROWS, D_MODEL), lambda g, i: (0, 0, 0))


def _mod_vectors(mod_ref, mod_row):
    r = pl.program_id(0) if mod_row is None else mod_row
    return [mod_ref[k, pl.ds(r, 1), :] for k in range(6)]


def _const_spec(shape):
    nd = len(shape)
    return pl.BlockSpec(shape, lambda g, i: (0,) * nd)


AB_QA, AB_QB, AB_KA, AB_VA, AB_KB, AB_VB, AB_W = 0, 512, 1024, 1152, 1280, 1792, 2304


TOKEN_CHAIN_ROWS = 256


def _store_v_ext(o_ref, rows, v):
    ones = jnp.ones((v.shape[0], LANES), o_ref.dtype)
    for j in range(v.shape[1] // LANES):
        o_ref[0, rows, 2 * j * LANES:(2 * j + 1) * LANES] = v[:, j * LANES:(j + 1) * LANES].astype(o_ref.dtype)
        o_ref[0, rows, (2 * j + 1) * LANES:(2 * j + 2) * LANES] = ones


def _rope_chunk(c, cos, sin_a, sin_b):
    return c * cos + pltpu.roll(c, LANES - 16, axis=1) * sin_a + pltpu.roll(c, 16, axis=1) * sin_b


def _inproj_ab_kernel(*refs, rope, mod_row):
    if rope:
        (x_ref, mod_ref, w_ref, gq_ref, gk_ref, g512_ref, g128_ref, cos_ref, sa_ref, sb_ref,
         qa_ref, qb_ref, ka_ref, kb_ref, va_ref, vb_ref) = refs
    else:
        (x_ref, mod_ref, w_ref, gq_ref, gk_ref, g512_ref, g128_ref,
         qa_ref, qb_ref, ka_ref, kb_ref, va_ref, vb_ref) = refs

    def rms(v, g_ref, gm_ref):
        ms = _dot((v * v).astype(BF16), gm_ref[...])
        return v * lax.rsqrt(ms + EPS) * g_ref[...]

    shift, scale = _mod_vectors(mod_ref, mod_row)[0:2]
    lo = lax.broadcasted_iota(jnp.int32, (1, LANES), 1) < HEAD_DIM
    tm = x_ref.shape[1]
    tc = min(TOKEN_CHAIN_ROWS, tm)

    def project(rb):
        rows = slice(rb * tc, (rb + 1) * tc)
        u = (x_ref[0, rows, :] * (1.0 + scale) + shift).astype(BF16)
        return _dot(u, w_ref[...])

    def epilogue(rb, h):
        rows = slice(rb * tc, (rb + 1) * tc)
        if rope:
            cos, sin_a, sin_b = cos_ref[rows, :], sa_ref[rows, :], sb_ref[rows, :]

        def chunks(v, q_scale):
            out = []
            for j in range(v.shape[1] // LANES):
                c = v[:, j * LANES:(j + 1) * LANES]
                if rope:
                    c = _rope_chunk(c, cos, sin_a, sin_b)
                out.append(c * q_scale if q_scale != 1.0 else c)
            return out

        def store(o_ref, cs):
            for j, c in enumerate(cs):
                o_ref[0, rows, j * LANES:(j + 1) * LANES] = c.astype(o_ref.dtype)

        qa = chunks(rms(h[:, AB_QA:AB_QB], gq_ref, g512_ref), Q_SCALE)
        qa_perm = []
        for p in range(A_HEADS // 2):
            a, b = qa[p // 2], qa[A_HEADS // 4 + p // 2]
            if p % 2 == 0:
                qa_perm.append(jnp.where(lo, a, pltpu.roll(b, HEAD_DIM, axis=1)))
            else:
                qa_perm.append(jnp.where(lo, pltpu.roll(a, HEAD_DIM, axis=1), b))
        store(qa_ref, qa_perm)
        store(qb_ref, chunks(h[:, AB_QB:AB_KA], Q_SCALE))
        store(ka_ref, chunks(rms(h[:, AB_KA:AB_VA], gk_ref, g128_ref), 1.0))
        store(kb_ref, chunks(h[:, AB_KB:AB_VB], 1.0))
        _store_v_ext(va_ref, rows, h[:, AB_VA:AB_KB])
        _store_v_ext(vb_ref, rows, h[:, AB_VB:AB_W])

    for rb in range(tm // tc):
        epilogue(rb, project(rb))


def _inproj_ab(tok, mod, mod_row, w, gq, gk, rope_tabs, tm, name):
    g, r, _ = tok.shape
    rope = rope_tabs is not None
    consts = [w, gq, gk, jnp.asarray(_group_mean_matrix(512), BF16), jnp.asarray(_group_mean_matrix(128), BF16)]
    in_specs = [_tok_spec(tm, D_MODEL), MOD_SPEC] + [_const_spec(c.shape) for c in consts]
    args = [tok, mod] + consts
    if rope:
        in_specs += [pl.BlockSpec((tm, LANES), lambda gg, i: (i, 0))] * 3
        args += list(rope_tabs)
    widths = (512, 512, 128, 512, 2 * 128, 2 * 512)
    return pl.pallas_call(
        functools.partial(_inproj_ab_kernel, rope=rope, mod_row=mod_row),
        grid=(g, r // tm),
        in_specs=in_specs,
        out_specs=[_tok_spec(tm, wd) for wd in widths],
        out_shape=[jax.ShapeDtypeStruct((g, r, wd), BF16) for wd in widths],
        compiler_params=_cparams(("parallel", "parallel")),
        name=name,
    )(*args)


def _softmax_pv(q2, kv_pairs, bias=None):
    scores = [_dot_nt(q2, k) for k, _ in kv_pairs]
    if bias is not None:
        scores[0] = scores[0] + bias
    m = scores[0].max(axis=-1, keepdims=True)
    for s in scores[1:]:
        m = jnp.maximum(m, s.max(axis=-1, keepdims=True))
    acc = None
    for s, (_, v) in zip(scores, kv_pairs):
        a = _dot(jnp.exp2(s - m).astype(BF16), v)
        acc = a if acc is None else acc + a
    return acc[:, :LANES] / acc[:, LANES:]


def _split_pair(qp, lo):
    zero = jnp.zeros_like(qp)
    return jnp.concatenate([jnp.where(lo, qp, zero), jnp.where(lo, zero, qp)], axis=0)


ATTN_CHAIN_ROWS = 256


def _chain_schedule(rows, n_pairs, chain_rows):
    blocks = [(r0, min(chain_rows, rows - r0)) for r0 in range(0, rows, chain_rows)]
    sched = [(p, r0, n) for r0, n in blocks for p in range(n_pairs)]
    if len(blocks) < 2:
        return sched
    (p, r0, n), rest = sched[0], sched[1:]
    sched = [(p, r0, n // 2), (p, r0 + n // 2, n - n // 2)] + rest
    (p, r0, n), rest = sched[-1], sched[:-1]
    return rest + [(p, r0, n - n // 2), (p, r0 + n - n // 2, n // 2)]


def _attn_ab_kernel(*refs, with_x, n_casts):
    if n_casts:
        cast_in = refs[len(refs) - 2 * n_casts - 1:len(refs) - n_casts - 1]
        cast_out = refs[len(refs) - n_casts:]
        refs = refs[:len(refs) - 2 * n_casts - 1] + (refs[len(refs) - n_casts - 1],)
        for src, dst in zip(cast_in, cast_out):
            dst[...] = src[...].astype(dst.dtype)
    if with_x:
        (qa_ref, qb_ref, kax_ref, vax_ref, kbx_ref, vbx_ref, kac_ref, vac_ref, kbc_ref, vbc_ref,
         lam_ref, subln_ref, o_ref) = refs
    else:
        (qa_ref, qb_ref, kac_ref, vac_ref, kbc_ref, vbc_ref, lam_ref, subln_ref, o_ref) = refs
    lo = lax.broadcasted_iota(jnp.int32, (1, LANES), 1) < HEAD_DIM

    lp = lam_ref[...]
    lam = (jnp.exp(jnp.sum(lp[0:1] * lp[1:2], axis=-1, keepdims=True))
           - jnp.exp(jnp.sum(lp[2:3] * lp[3:4], axis=-1, keepdims=True)) + LAM_INIT0)

    def gqa_chain(p, r0, n):
        rows = slice(r0, r0 + n)
        sl = slice(p * LANES, (p + 1) * LANES)
        q2 = _split_pair(qa_ref[0, rows, sl], lo)
        kv = [(kac_ref[0], vac_ref[0])]
        if with_x:
            kv.append((kax_ref[0], vax_ref[0]))
        o2 = _softmax_pv(q2, kv)
        o_ref[0, rows, sl] = jnp.where(lo, o2[:n], o2[n:]).astype(o_ref.dtype)

    def diff_chain(i, r0, n):
        rows = slice(r0, r0 + n)
        sl = slice(i * LANES, (i + 1) * LANES)
        sl2 = slice(2 * i * LANES, (2 * i + 2) * LANES)
        q2 = _split_pair(qb_ref[0, rows, sl], lo)
        kv = [(kbc_ref[0, :, sl], vbc_ref[0, :, sl2])]
        if with_x:
            kv.append((kbx_ref[0, :, sl], vbx_ref[0, :, sl2]))
        o2 = _softmax_pv(q2, kv)
        o = o2[:n] - lam * o2[n:]
        ms = jnp.mean(o * o, axis=-1, keepdims=True)
        o = o * lax.rsqrt(ms + EPS) * subln_ref[...] * (1.0 - LAM_INIT0)
        o_ref[0, rows, 512 + i * LANES:512 + (i + 1) * LANES] = o.astype(o_ref.dtype)

    for pair, r0, n in _chain_schedule(qa_ref.shape[1], A_HEADS // 2 + B_HEADS, ATTN_CHAIN_ROWS):
        if pair < A_HEADS // 2:
            gqa_chain(pair, r0, n)
        else:
            diff_chain(pair - A_HEADS // 2, r0, n)


def _attn_ab(q_parts, x_kv, c_kv, lam_p, subln, tq, name, casts=()):
    qa, qb = q_parts
    b, nq, _ = qa.shape
    with_x = x_kv is not None
    steps = nq // tq

    def full(a):
        return pl.BlockSpec((1,) + a.shape[1:], lambda bb, i: (bb, 0, 0))

    args = [qa, qb]
    in_specs = [pl.BlockSpec((1, tq, 512), lambda bb, i: (bb, i, 0))] * 2
    if with_x:
        args += list(x_kv)
        in_specs += [full(a) for a in x_kv]
    args += list(c_kv) + [lam_p, subln]
    in_specs += [full(a) for a in c_kv] + [pl.BlockSpec(lam_p.shape, lambda bb, i: (0, 0)),
                                           pl.BlockSpec(subln.shape, lambda bb, i: (0, 0))]
    cast_specs = [pl.BlockSpec((a.shape[0] // (b * steps), a.shape[1]), lambda bb, i: (bb * steps + i, 0))
                  for a in casts]
    outs = pl.pallas_call(
        functools.partial(_attn_ab_kernel, with_x=with_x, n_casts=len(casts)),
        grid=(b, steps),
        in_specs=in_specs + cast_specs,
        out_specs=[pl.BlockSpec((1, tq, D_MODEL), lambda bb, i: (bb, i, 0))] + cast_specs,
        out_shape=[jax.ShapeDtypeStruct((b, nq, D_MODEL), BF16)]
                  + [jax.ShapeDtypeStruct(a.shape, BF16) for a in casts],
        compiler_params=_cparams(("parallel", "parallel")),
        name=name,
    )(*args, *casts)
    return outs[0] if not casts else outs


FFN_CHUNK = 1024
POST_EDGE_PIECES = 2


def _post_kernel(*refs, n_parts, mod_row):
    o_refs = refs[:n_parts]
    wo_refs = refs[n_parts:2 * n_parts]
    x_ref, mod_ref, g_ref, b_ref, w1_ref, w2_ref, y_ref = refs[2 * n_parts:]
    _, _, gate1, shift2, scale2, gate2 = _mod_vectors(mod_ref, mod_row)
    tm = x_ref.shape[1]
    n_chunks = FFN_DIM // FFN_CHUNK
    pieces = [slice(r, r + tm // POST_EDGE_PIECES) for r in range(0, tm, tm // POST_EDGE_PIECES)]
    sq_relu = lambda t: jnp.square(jnp.maximum(t, 0.0)).astype(BF16)

    ys = []
    for rows in pieces:
        y = _dot(o_refs[0][0, rows, :], wo_refs[0][...])
        for o_ref, w_ref in zip(o_refs[1:], wo_refs[1:]):
            y = y + _dot(o_ref[0, rows, :], w_ref[...])
        ys.append(y)
    x1s, us, h0s = [], [], []
    for rows, y in zip(pieces, ys):
        x1s.append(_layer_norm(ALPHA * x_ref[0, rows, :] + gate1 * y, g_ref[0:1, :], b_ref[0:1, :]))
        us.append((x1s[-1] * (1.0 + scale2) + shift2).astype(BF16))
        h0s.append(sq_relu(_dot(us[-1], w1_ref[0, :, 0:FFN_CHUNK])))
    u = jnp.concatenate(us, axis=0)
    acc = _dot(jnp.concatenate(h0s, axis=0), w2_ref[0, 0:FFN_CHUNK, :])
    for c in range(1, n_chunks - 1):
        sl = slice(c * FFN_CHUNK, (c + 1) * FFN_CHUNK)
        acc = acc + _dot(sq_relu(_dot(u, w1_ref[0, :, sl])), w2_ref[0, sl, :])
    sl = slice((n_chunks - 1) * FFN_CHUNK, n_chunks * FFN_CHUNK)
    h_last = sq_relu(_dot(u, w1_ref[0, :, sl]))
    fs = [acc[rows] + _dot(h_last[rows], w2_ref[0, sl, :]) for rows in pieces]
    for rows, x1, f in zip(pieces, x1s, fs):
        y_ref[0, rows, :] = _layer_norm(ALPHA * x1 + gate2 * f, g_ref[1:2, :], b_ref[1:2, :])


def _post(o_parts, wo_parts, resid, mod, mod_row, ln_g, ln_b, w1, w2, layer, tm, name):
    g, r, _ = resid.shape
    n = len(o_parts)
    single = pl.Buffered(1)
    layer_w = lambda a: pl.BlockSpec((1,) + a.shape[1:], lambda gg, i: (layer, 0, 0), pipeline_mode=single)
    const1 = lambda a: pl.BlockSpec(a.shape, lambda gg, i: (0,) * a.ndim, pipeline_mode=single)
    in_specs = ([_tok_spec(tm, o.shape[2]) for o in o_parts] + [const1(w) for w in wo_parts]
                + [_tok_spec(tm, D_MODEL), MOD_SPEC, _const_spec(ln_g.shape), _const_spec(ln_b.shape),
                   layer_w(w1), layer_w(w2)])
    return pl.pallas_call(
        functools.partial(_post_kernel, n_parts=n, mod_row=mod_row),
        grid=(g, r // tm),
        in_specs=in_specs,
        out_specs=_tok_spec(tm, D_MODEL),
        out_shape=jax.ShapeDtypeStruct((g, r, D_MODEL), F32),
        compiler_params=_cparams(("parallel", "parallel")),
        name=name,
    )(*o_parts, *wo_parts, resid, mod, ln_g, ln_b, w1, w2)


CD_Q, CD_F, CD_K, CD_V, CD_W = 0, 512, 1024, 1536, 2048


def _inproj_cd_kernel(*refs, with_q, mod_row):
    if with_q:
        x_ref, mod_ref, w_ref, wa_ref, wb_ref, q_ref, fa_ref, fb_ref, k_ref, v_ref = refs
    else:
        x_ref, mod_ref, w_ref, k_ref, v_ref = refs
    shift, scale = _mod_vectors(mod_ref, mod_row)[0:2]
    u = (x_ref[0] * (1.0 + scale) + shift).astype(BF16)
    if with_q:
        q_ref[0] = (_dot(u, w_ref[:, CD_Q:CD_F]) * Q_SCALE).astype(q_ref.dtype)
        fa_ref[0] = _dot(u, wa_ref[...]).astype(fa_ref.dtype)
        fb_ref[0] = _dot(u, wb_ref[...]).astype(fb_ref.dtype)
    kv = _dot(u, w_ref[:, CD_K:CD_W])
    k_ref[0] = kv[:, :CD_V - CD_K].astype(k_ref.dtype)
    _store_v_ext(v_ref, slice(None), kv[:, CD_V - CD_K:])


def _inproj_cd(tok, mod, mod_row, w, w_ab, tm, name):
    g, r, _ = tok.shape
    with_q = w_ab is not None
    weights = [w] + (list(w_ab) if with_q else [])
    widths = ((512, 512, 512) if with_q else ()) + (512, 1024)
    return pl.pallas_call(
        functools.partial(_inproj_cd_kernel, with_q=with_q, mod_row=mod_row),
        grid=(g, r // tm),
        in_specs=[_tok_spec(tm, D_MODEL), MOD_SPEC] + [_const_spec(a.shape) for a in weights],
        out_specs=[_tok_spec(tm, wd) for wd in widths],
        out_shape=[jax.ShapeDtypeStruct((g, r, wd), BF16) for wd in widths],
        compiler_params=_cparams(("parallel", "parallel")),
        name=name,
    )(tok, mod, *weights)


def _dft_weight_kernel(w_ref, cc_ref, sc_ref, wa_ref, wb_ref):
    c_hi, c_mid, _ = _split3(cc_ref[...])
    s_hi, s_mid, _ = _split3(sc_ref[...])
    for gi in range(D_GROUPS):
        sl = slice(gi * D_GROUP_DIM, (gi + 1) * D_GROUP_DIM)
        w_hi, w_mid, _ = _split3(w_ref[:, sl])
        wa_ref[:, sl] = (_dot(w_hi, c_hi) + (_dot(w_mid, c_hi) + _dot(w_hi, c_mid))).astype(BF16)
        wb_ref[:, sl] = (_dot(w_hi, s_hi) + (_dot(w_mid, s_hi) + _dot(w_hi, s_mid))).astype(BF16)


def _dft_weights(w_in, cc, sc):
    width = CD_K - CD_F
    out = pl.BlockSpec((D_MODEL, width), lambda i: (0, 0))
    return pl.pallas_call(
        _dft_weight_kernel,
        grid=(1,),
        in_specs=[pl.BlockSpec((D_MODEL, width), lambda i: (0, CD_F // width)),
                  pl.BlockSpec(cc.shape, lambda i: (0, 0)), pl.BlockSpec(sc.shape, lambda i: (0, 0))],
        out_specs=[out, out],
        out_shape=[jax.ShapeDtypeStruct((D_MODEL, width), BF16)] * 2,
        compiler_params=_cparams(("arbitrary",)),
        name="dft_weights",
    )(w_in, cc, sc)


DFT_ROWS = 512


def _dft_time_kernel(a_ref, b_ref, ct_ref, sn_ref, o_ref):
    a = a_ref[0]
    bm = b_ref[0]
    for r in range(SEQ // DFT_ROWS):
        sl = slice(r * DFT_ROWS, (r + 1) * DFT_ROWS)
        y = _dot(ct_ref[sl, :], a) + _dot(sn_ref[sl, :], bm)
        o_ref[0, sl, :] = y.astype(o_ref.dtype)


def _dft_time(a, bm, ct, sn):
    b = a.shape[0]
    single = pl.Buffered(1)
    tok = pl.BlockSpec((1, SEQ, 512), lambda bb: (bb, 0, 0))
    tab = pl.BlockSpec((SEQ, SEQ), lambda bb: (0, 0), pipeline_mode=single)
    return pl.pallas_call(
        _dft_time_kernel,
        grid=(b,),
        in_specs=[tok, tok, tab, tab],
        out_specs=tok,
        out_shape=jax.ShapeDtypeStruct((b, SEQ, 512), BF16),
        compiler_params=_cparams(("parallel",)),
        name="dft_time",
    )(a, bm, ct, sn)


def _bias_table_kernel(r_ref, o_ref):
    n = GRID_W * LANES
    row = lax.broadcasted_iota(jnp.int32, (2 * 32, n), 0)
    lane = lax.broadcasted_iota(jnp.int32, (2 * 32, n), 1)
    e_row, i_row = row >> 5, row & 31
    c = lane >> 7
    e_lane = (lane >> 6) & 1
    kc = lane & (GRID_W - 1)
    sel = jnp.where((e_row == e_lane) & (kc - c + (WIN_W - 1) == i_row), 1.0, 0.0).astype(BF16)
    hi, mid, lo = _split3(r_ref[...])
    t = _dot(hi, sel) + _dot(mid, sel) + _dot(lo, sel)
    c1 = c[0:1]
    kc1 = kc[0:1]
    cstart = jnp.clip(c1 - WIN_W // 2, 0, GRID_W - WIN_W)
    col_ok = (kc1 >= cstart) & (kc1 < cstart + WIN_W)
    o_ref[...] = jnp.where(col_ok, t * LOG2E, NEG)


def _bias_table(rpb):
    n_dr = 2 * WIN_H - 1
    p = jnp.pad(rpb, ((0, 0), (CB_OFFSET, CB_ENTRIES + 1 - n_dr - CB_OFFSET), (0, 1)))
    r2 = jnp.stack([p[:, 0:CB_ENTRIES], p[:, 1:CB_ENTRIES + 1]], axis=2)
    r2 = r2.reshape(C_HEADS * CB_ENTRIES, 2 * 32)
    t = pl.pallas_call(
        _bias_table_kernel,
        out_shape=jax.ShapeDtypeStruct((C_HEADS * CB_ENTRIES, GRID_W * LANES), F32),
        compiler_params=pltpu.CompilerParams(vmem_limit_bytes=VMEM_LIMIT),
        name="natten_bias_table",
    )(r2)
    return t.reshape(C_HEADS, CB_ENTRIES, GRID_W, LANES)


NAT_BLOCKS = 4


def _natten_block(blk, q_ref, k_ref, v_ref, kc_ref, vc_ref, cb_ref, o_ref, lo, e):
    rows = slice(blk * NAT_Q, (blk + 1) * NAT_Q)
    r0 = NAT_ROWS * (NAT_BLOCKS * pl.program_id(1) + blk)
    ws = jnp.clip(r0 - WIN_H // 2, 0, GRID_H - NAT_WIN_ROWS)
    koff = pl.multiple_of(ws * GRID_W, LANES)

    entry = []
    rmask = []
    for qr in range(NAT_ROWS):
        r = r0 + qr
        lo_r = jnp.clip(r - WIN_H // 2, 0, GRID_H - WIN_H) - r + (WIN_H - 1)
        ent_q, mask_q = [], []
        for j in range(NAT_TILES):
            dr0 = ws + 2 * j - r + (WIN_H - 1)
            ent_q.append(dr0 + CB_OFFSET)
            dr = dr0 + e
            ok = (dr >= lo_r) & (dr < lo_r + WIN_H)
            mask_q.append(jnp.where(ok, 0.0, NEG))
        entry.append(ent_q)
        rmask.append(mask_q)

    for p in range(C_HEADS // 2):
        sl = slice(p * LANES, (p + 1) * LANES)
        sl2 = slice(2 * p * LANES, (2 * p + 2) * LANES)
        q2 = _split_pair(q_ref[0, rows, sl], lo)
        bias = jnp.concatenate(
            [jnp.concatenate([cb_ref[2 * p + hh, entry[qr][j]] + rmask[qr][j] for j in range(NAT_TILES)], axis=1)
             for hh in range(2) for qr in range(NAT_ROWS)], axis=0)
        kv = [(k_ref[0, pl.ds(koff, NAT_WIN), sl], v_ref[0, pl.ds(koff, NAT_WIN), sl2]),
              (kc_ref[0, :, sl], vc_ref[0, :, sl2])]
        o2 = _softmax_pv(q2, kv, bias)
        o_ref[0, rows, sl] = jnp.where(lo, o2[:NAT_Q], o2[NAT_Q:]).astype(o_ref.dtype)


def _natten_kernel(q_ref, k_ref, v_ref, kc_ref, vc_ref, cb_ref, o_ref):
    lane = lax.broadcasted_iota(jnp.int32, (1, LANES), 1)
    lo = lane < HEAD_DIM
    e = jnp.where(lo, 0, 1)
    for blk in range(NAT_BLOCKS):
        _natten_block(blk, q_ref, k_ref, v_ref, kc_ref, vc_ref, cb_ref, o_ref, lo, e)


def _natten(q, k, v, kc, vc, cb):
    b = q.shape[0]
    tq = NAT_BLOCKS * NAT_Q
    qspec = pl.BlockSpec((1, tq, 512), lambda bb, i: (bb, i, 0))
    full = lambda a: pl.BlockSpec((1,) + a.shape[1:], lambda bb, i: (bb, 0, 0))
    return pl.pallas_call(
        _natten_kernel,
        grid=(b, SEQ // tq),
        in_specs=[qspec, full(k), full(v), full(kc), full(vc),
                  pl.BlockSpec(cb.shape, lambda bb, i: (0, 0, 0, 0))],
        out_specs=qspec,
        out_shape=jax.ShapeDtypeStruct((b, SEQ, 512), BF16),
        compiler_params=_cparams(("parallel", "parallel")),
        name="natten",
    )(q, k, v, kc, vc, cb)


def kernel(x, c, ctx, c_ctx, mod_w, mod_b, ln_g, ln_b, ffn_w1, ffn_w2, ab_w_in, ab_w_out, a_q_norm, a_k_norm,
           b_lambda, b_subln, cd_w_in, cd_w_out, c_rpb):
    b = x.shape[0]
    nctx = b * CTX_LEN
    tm = 512
    tmc = min(tm, nctx)

    cs = jnp.concatenate([c, c_ctx[None, :], jnp.zeros((MOD_ROWS - b - 1, D_MODEL), F32)], axis=0)
    mods = _modulation(cs, mod_w, mod_b)
    ctx_row = b

    cflat = ctx.reshape(1, nctx, D_MODEL)

    perm = np.asarray(QA_PERM)
    w_ab = ab_w_in.reshape(D_MODEL, AB_W).astype(BF16)
    w_out = ab_w_out.reshape(D_MODEL, D_MODEL)
    w_out_ab = jnp.concatenate([w_out[:512].reshape(A_HEADS, HEAD_DIM, D_MODEL)[perm].reshape(512, D_MODEL),
                                w_out[512:]], axis=0).astype(BF16)
    gq = jnp.tile(a_q_norm[0], A_HEADS)[None, :]
    gk = jnp.tile(a_k_norm[0], A_KV_HEADS)[None, :]
    rope_tabs = tuple(jnp.asarray(t) for t in _rope_tables())
    subln = b_subln[0][None, :]
    mod0 = mods[0]

    xq = _inproj_ab(x, mod0, None, w_ab, gq, gk, rope_tabs, 2 * tm, "inproj_ab_x")
    cq = _inproj_ab(cflat, mod0, ctx_row, w_ab, gq, gk, None, tmc, "inproj_ab_ctx")
    cq = [a.reshape(b, CTX_LEN, a.shape[2]) for a in cq]
    qa_x, qb_x, ka_x, kb_x, va_x, vb_x = xq
    qa_c, qb_c, ka_c, kb_c, va_c, vb_c = cq
    c_kv = (ka_c, va_c, kb_c, vb_c)
    o_x, w1, w2, w_cd, w_out_cd = _attn_ab(
        (qa_x, qb_x), (ka_x, va_x, kb_x, vb_x), c_kv, b_lambda[0], subln, 512, "attn_ab_x",
        casts=(ffn_w1.reshape(DEPTH * D_MODEL, FFN_DIM), ffn_w2.reshape(DEPTH * FFN_DIM, D_MODEL),
               cd_w_in.reshape(D_MODEL, CD_W), cd_w_out.reshape(D_MODEL, D_MODEL)))
    w1 = w1.reshape(DEPTH, D_MODEL, FFN_DIM)
    w2 = w2.reshape(DEPTH, FFN_DIM, D_MODEL)
    o_c = _attn_ab((qa_c, qb_c), None, c_kv, b_lambda[0], subln, CTX_LEN, "attn_ab_ctx")

    x2 = _post([o_x], [w_out_ab], x, mod0, None, ln_g[0], ln_b[0], w1, w2, 0, tm, "post0_x")
    c2 = _post([o_c.reshape(1, nctx, D_MODEL)], [w_out_ab], cflat, mod0, ctx_row, ln_g[0], ln_b[0], w1, w2, 0,
               tmc, "post0_ctx")

    mod1 = mods[1]
    w_in = cd_w_in.reshape(D_MODEL, CD_W)
    ct, sn, cc, sc = _dft_tables()
    w_fab = _dft_weights(w_in, jnp.asarray(cc), jnp.asarray(sc))
    q_n, f_a, f_b, k_n, v_n = _inproj_cd(x2, mod1, None, w_cd, w_fab, 2 * tm, "inproj_cd_x")
    kc_n, vc_n = _inproj_cd(c2, mod1, ctx_row, w_cd, None, tmc, "inproj_cd_ctx")
    kc_n = kc_n.reshape(b, CTX_LEN, 512)
    vc_n = vc_n.reshape(b, CTX_LEN, 1024)

    o_d = _dft_time(f_a, f_b, jnp.asarray(ct).astype(BF16), jnp.asarray(sn).astype(BF16))
    cb = _bias_table(c_rpb[0])
    o_n = _natten(q_n, k_n, v_n, kc_n, vc_n, cb)

    return _post([o_n, o_d], [w_out_cd[:512], w_out_cd[512:]], x2, mod1, None, ln_g[1], ln_b[1], w1, w2, 1, tm, "post1_x")
```

```python
import functools
import math

import numpy as np
import jax
import jax.numpy as jnp
from jax import lax
from jax.experimental import pallas as pl
from jax.experimental.pallas import tpu as pltpu

F32 = jnp.float32
BF16 = jnp.bfloat16

D_MODEL = 1024
SEQ = 2048
CTX_LEN = 256
DEPTH = 2
GRID_W = 64
GRID_H = SEQ // GRID_W
HEAD_DIM = 64
A_HEADS = 8
A_KV_HEADS = 2
B_HEADS = 4
C_HEADS = 8
WIN_H = 8
WIN_W = 16
D_GROUPS = 4
D_GROUP_DIM = 128
FFN_DIM = 4 * D_MODEL
ROPE_THETA = 10000.0
ALPHA = (2.0 * DEPTH) ** 0.25
EPS = 1e-6
ATTN_SCALE = HEAD_DIM ** -0.5
LOG2E = math.log2(math.e)
Q_SCALE = ATTN_SCALE * LOG2E
LAM_INIT0 = 0.8 - 0.6 * math.exp(-0.3 * 0)
NEG = -1e30

LANES = 128
MOD_ROWS = 16
VMEM_LIMIT = 56 * 1024 * 1024

QA_PERM = (0, 4, 1, 5, 2, 6, 3, 7)

NAT_ROWS = 4
NAT_Q = NAT_ROWS * GRID_W
NAT_WIN_ROWS = NAT_ROWS + WIN_H
NAT_WIN = NAT_WIN_ROWS * GRID_W
NAT_TILES = NAT_WIN_ROWS // 2
_NAT_DR0 = [min(max(r0 - WIN_H // 2, 0), GRID_H - NAT_WIN_ROWS) + 2 * j - (r0 + qr) + WIN_H - 1
            for r0 in range(0, GRID_H, NAT_ROWS) for qr in range(NAT_ROWS) for j in range(NAT_TILES)]
CB_OFFSET = -min(_NAT_DR0)
CB_ENTRIES = max(_NAT_DR0) + CB_OFFSET + 1


def _cparams(sem, flags=None):
    return pltpu.CompilerParams(dimension_semantics=sem, vmem_limit_bytes=VMEM_LIMIT, flags=flags)


def _dot(a, b):
    return jnp.dot(a, b, preferred_element_type=F32)


def _dot_nt(a, b):
    return lax.dot_general(a, b, (((1,), (1,)), ((), ())), preferred_element_type=F32)


def _split3(a):
    hi = a.astype(BF16)
    r1 = a - hi.astype(F32)
    mid = r1.astype(BF16)
    lo = (r1 - mid.astype(F32)).astype(BF16)
    return hi, mid, lo


def _layer_norm(z, g, b):
    mu = jnp.mean(z, axis=-1, keepdims=True)
    zc = z - mu
    var = jnp.mean(zc * zc, axis=-1, keepdims=True)
    return zc * lax.rsqrt(var + EPS) * g + b


@functools.lru_cache(maxsize=None)
def _rope_tables():
    pos = np.arange(SEQ)
    row = (pos // GRID_W).astype(np.float64)
    col = (pos % GRID_W).astype(np.float64)
    half = HEAD_DIM // 2
    freqs = np.power(ROPE_THETA, -np.arange(0, half, 2, dtype=np.float64) / half)
    def axis_angles(p):
        a = p[:, None] * freqs[None, :]
        return np.concatenate([a, a], axis=-1)
    ang = np.concatenate([axis_angles(row), axis_angles(col)], axis=-1)
    cos, sin = np.cos(ang), np.sin(ang)
    first = (np.arange(HEAD_DIM) % half) < (half // 2)
    sin_a = np.where(first[None, :], -sin, 0.0)
    sin_b = np.where(first[None, :], 0.0, sin)
    tile = lambda t: np.tile(t, (1, LANES // HEAD_DIM)).astype(np.float32)
    return tile(cos), tile(sin_a), tile(sin_b)


@functools.lru_cache(maxsize=None)
def _group_mean_matrix(width):
    g = np.arange(width) // HEAD_DIM
    return (g[:, None] == g[None, :]).astype(np.float32) / HEAD_DIM


@functools.lru_cache(maxsize=None)
def _dft_tables():
    t = np.arange(SEQ, dtype=np.int64)
    k = (t[:, None] * t[None, :]) % SEQ
    ang = 2.0 * np.pi * k.astype(np.float64) / SEQ
    ct = np.cos(ang).astype(np.float32)
    sn = (-np.sin(ang)).astype(np.float32)
    c = np.arange(D_GROUP_DIM, dtype=np.int64)
    kc = (c[:, None] * c[None, :]) % D_GROUP_DIM
    angc = 2.0 * np.pi * kc.astype(np.float64) / D_GROUP_DIM
    norm = 1.0 / math.sqrt(SEQ * D_GROUP_DIM)
    return ct, sn, (np.cos(angc) * norm).astype(np.float32), (np.sin(angc) * norm).astype(np.float32)


def _mod_kernel(c_ref, w_ref, b_ref, o_ref):
    c = c_ref[...]
    a = c / (1.0 + jnp.exp(-c))
    a_hi, a_mid, _ = _split3(a)
    y = _dot(jnp.concatenate([a_hi, a_mid], axis=0), w_ref[0].astype(BF16))
    o_ref[0, 0] = y[:MOD_ROWS] + y[MOD_ROWS:] + b_ref[0]


def _modulation(cs, mod_w, mod_b):
    n = 6 * D_MODEL
    return pl.pallas_call(
        _mod_kernel,
        grid=(DEPTH, 6),
        in_specs=[pl.BlockSpec((MOD_ROWS, D_MODEL), lambda l, j: (0, 0)),
                  pl.BlockSpec((1, D_MODEL, D_MODEL), lambda l, j: (l, 0, j)),
                  pl.BlockSpec((1, 1, D_MODEL), lambda l, j: (l, 0, j))],
        out_specs=pl.BlockSpec((1, 1, MOD_ROWS, D_MODEL), lambda l, j: (l, j, 0, 0)),
        out_shape=jax.ShapeDtypeStruct((DEPTH, 6, MOD_ROWS, D_MODEL), F32),
        compiler_params=_cparams(("arbitrary", "arbitrary")),
        name="modulation",
    )(cs, mod_w, mod_b.reshape(DEPTH, 1, n))


def _tok_spec(tm, width):
    return pl.BlockSpec((1, tm, width), lambda g, i: (g, i, 0))


MOD_SPEC = pl.BlockSpec((6, MOD_ROWS, D_MODEL), lambda g, i: (0, 0, 0))


def _mod_vectors(mod_ref, mod_row):
    r = pl.program_id(0) if mod_row is None else mod_row
    return [mod_ref[k, pl.ds(r, 1), :] for k in range(6)]


def _const_spec(shape):
    nd = len(shape)
    return pl.BlockSpec(shape, lambda g, i: (0,) * nd)


AB_QA, AB_QB, AB_KA, AB_VA, AB_KB, AB_VB, AB_W = 0, 512, 1024, 1152, 1280, 1792, 2304


TOKEN_CHAIN_ROWS = 256


def _store_v_ext(o_ref, rows, v):
    ones = jnp.ones((v.shape[0], LANES), o_ref.dtype)
    for j in range(v.shape[1] // LANES):
        o_ref[0, rows, 2 * j * LANES:(2 * j + 1) * LANES] = v[:, j * LANES:(j + 1) * LANES].astype(o_ref.dtype)
        o_ref[0, rows, (2 * j + 1) * LANES:(2 * j + 2) * LANES] = ones


def _rope_chunk(c, cos, sin_a, sin_b):
    return c * cos + pltpu.roll(c, LANES - 16, axis=1) * sin_a + pltpu.roll(c, 16, axis=1) * sin_b


def _inproj_ab_kernel(*refs, rope, mod_row):
    if rope:
        (x_ref, mod_ref, w_ref, gq_ref, gk_ref, g512_ref, g128_ref, cos_ref, sa_ref, sb_ref,
         qa_ref, qb_ref, ka_ref, kb_ref, va_ref, vb_ref) = refs
    else:
        (x_ref, mod_ref, w_ref, gq_ref, gk_ref, g512_ref, g128_ref,
         qa_ref, qb_ref, ka_ref, kb_ref, va_ref, vb_ref) = refs

    def rms(v, g_ref, gm_ref):
        ms = _dot((v * v).astype(BF16), gm_ref[...])
        return v * lax.rsqrt(ms + EPS) * g_ref[...]

    shift, scale = _mod_vectors(mod_ref, mod_row)[0:2]
    lo = lax.broadcasted_iota(jnp.int32, (1, LANES), 1) < HEAD_DIM
    tm = x_ref.shape[1]
    tc = min(TOKEN_CHAIN_ROWS, tm)

    def project(rb):
        rows = slice(rb * tc, (rb + 1) * tc)
        u = (x_ref[0, rows, :] * (1.0 + scale) + shift).astype(BF16)
        return _dot(u, w_ref[...])

    def epilogue(rb, h):
        rows = slice(rb * tc, (rb + 1) * tc)
        if rope:
            cos, sin_a, sin_b = cos_ref[rows, :], sa_ref[rows, :], sb_ref[rows, :]

        def chunks(v, q_scale):
            out = []
            for j in range(v.shape[1] // LANES):
                c = v[:, j * LANES:(j + 1) * LANES]
                if rope:
                    c = _rope_chunk(c, cos, sin_a, sin_b)
                out.append(c * q_scale if q_scale != 1.0 else c)
            return out

        def store(o_ref, cs):
            for j, c in enumerate(cs):
                o_ref[0, rows, j * LANES:(j + 1) * LANES] = c.astype(o_ref.dtype)

        qa = chunks(rms(h[:, AB_QA:AB_QB], gq_ref, g512_ref), Q_SCALE)
        qa_perm = []
        for p in range(A_HEADS // 2):
            a, b = qa[p // 2], qa[A_HEADS // 4 + p // 2]
            if p % 2 == 0:
                qa_perm.append(jnp.where(lo, a, pltpu.roll(b, HEAD_DIM, axis=1)))
            else:
                qa_perm.append(jnp.where(lo, pltpu.roll(a, HEAD_DIM, axis=1), b))
        store(qa_ref, qa_perm)
        store(qb_ref, chunks(h[:, AB_QB:AB_KA], Q_SCALE))
        store(ka_ref, chunks(rms(h[:, AB_KA:AB_VA], gk_ref, g128_ref), 1.0))
        store(kb_ref, chunks(h[:, AB_KB:AB_VB], 1.0))
        _store_v_ext(va_ref, rows, h[:, AB_VA:AB_KB])
        _store_v_ext(vb_ref, rows, h[:, AB_VB:AB_W])

    for rb in range(tm // tc):
        epilogue(rb, project(rb))


def _inproj_ab(tok, mod, mod_row, w, gq, gk, rope_tabs, tm, name):
    g, r, _ = tok.shape
    rope = rope_tabs is not None
    consts = [w, gq, gk, jnp.asarray(_group_mean_matrix(512), BF16), jnp.asarray(_group_mean_matrix(128), BF16)]
    in_specs = [_tok_spec(tm, D_MODEL), MOD_SPEC] + [_const_spec(c.shape) for c in consts]
    args = [tok, mod] + consts
    if rope:
        in_specs += [pl.BlockSpec((tm, LANES), lambda gg, i: (i, 0))] * 3
        args += list(rope_tabs)
    widths = (512, 512, 128, 512, 2 * 128, 2 * 512)
    return pl.pallas_call(
        functools.partial(_inproj_ab_kernel, rope=rope, mod_row=mod_row),
        grid=(g, r // tm),
        in_specs=in_specs,
        out_specs=[_tok_spec(tm, wd) for wd in widths],
        out_shape=[jax.ShapeDtypeStruct((g, r, wd), BF16) for wd in widths],
        compiler_params=_cparams(("parallel", "parallel")),
        name=name,
    )(*args)


def _scores(q2, kv_pairs, bias=None):
    scores = [_dot_nt(q2, k) for k, _ in kv_pairs]
    if bias is not None:
        scores[0] = scores[0] + bias
    return scores


def _softmax_pv(q2, kv_pairs, bias=None, scores=None):
    if scores is None:
        scores = _scores(q2, kv_pairs, bias)
    m = scores[0].max(axis=-1, keepdims=True)
    for s in scores[1:]:
        m = jnp.maximum(m, s.max(axis=-1, keepdims=True))
    acc = None
    for s, (_, v) in zip(scores, kv_pairs):
        a = _dot(jnp.exp2(s - m).astype(BF16), v)
        acc = a if acc is None else acc + a
    return acc[:, :LANES] / acc[:, LANES:]


def _run_chains(chains, lookahead):
    pending = []
    for start, finish in chains:
        pending.append((finish, start()))
        if len(pending) > lookahead:
            fin, scores = pending.pop(0)
            fin(scores)
    for fin, scores in pending:
        fin(scores)


ATTN_LOOKAHEAD = 2


def _split_pair(qp, lo):
    zero = jnp.zeros_like(qp)
    return jnp.concatenate([jnp.where(lo, qp, zero), jnp.where(lo, zero, qp)], axis=0)


ATTN_CHAIN_ROWS = 256


def _chain_schedule(rows, n_pairs, chain_rows):
    blocks = [(r0, min(chain_rows, rows - r0)) for r0 in range(0, rows, chain_rows)]
    sched = [(p, r0, n) for r0, n in blocks for p in range(n_pairs)]
    if len(blocks) < 2:
        return sched
    (p, r0, n), rest = sched[0], sched[1:]
    sched = [(p, r0, n // 2), (p, r0 + n // 2, n - n // 2)] + rest
    (p, r0, n), rest = sched[-1], sched[:-1]
    return rest + [(p, r0, n - n // 2), (p, r0 + n - n // 2, n // 2)]


def _attn_ab_kernel(*refs, with_x, n_casts):
    if n_casts:
        cast_in = refs[len(refs) - 2 * n_casts - 1:len(refs) - n_casts - 1]
        cast_out = refs[len(refs) - n_casts:]
        refs = refs[:len(refs) - 2 * n_casts - 1] + (refs[len(refs) - n_casts - 1],)
        for src, dst in zip(cast_in, cast_out):
            dst[...] = src[...].astype(dst.dtype)
    if with_x:
        (qa_ref, qb_ref, kax_ref, vax_ref, kbx_ref, vbx_ref, kac_ref, vac_ref, kbc_ref, vbc_ref,
         lam_ref, subln_ref, o_ref) = refs
    else:
        (qa_ref, qb_ref, kac_ref, vac_ref, kbc_ref, vbc_ref, lam_ref, subln_ref, o_ref) = refs
    lo = lax.broadcasted_iota(jnp.int32, (1, LANES), 1) < HEAD_DIM

    lp = lam_ref[...]
    lam = (jnp.exp(jnp.sum(lp[0:1] * lp[1:2], axis=-1, keepdims=True))
           - jnp.exp(jnp.sum(lp[2:3] * lp[3:4], axis=-1, keepdims=True)) + LAM_INIT0)

    def gqa_chain(p, r0, n):
        rows = slice(r0, r0 + n)
        sl = slice(p * LANES, (p + 1) * LANES)
        kv = [(kac_ref[0], vac_ref[0])]
        if with_x:
            kv.append((kax_ref[0], vax_ref[0]))

        def finish(scores):
            o2 = _softmax_pv(None, kv, scores=scores)
            o_ref[0, rows, sl] = jnp.where(lo, o2[:n], o2[n:]).astype(o_ref.dtype)

        return (lambda: _scores(_split_pair(qa_ref[0, rows, sl], lo), kv)), finish

    def diff_chain(i, r0, n):
        rows = slice(r0, r0 + n)
        sl = slice(i * LANES, (i + 1) * LANES)
        sl2 = slice(2 * i * LANES, (2 * i + 2) * LANES)
        kv = [(kbc_ref[0, :, sl], vbc_ref[0, :, sl2])]
        if with_x:
            kv.append((kbx_ref[0, :, sl], vbx_ref[0, :, sl2]))

        def finish(scores):
            o2 = _softmax_pv(None, kv, scores=scores)
            o = o2[:n] - lam * o2[n:]
            ms = jnp.mean(o * o, axis=-1, keepdims=True)
            o = o * lax.rsqrt(ms + EPS) * subln_ref[...] * (1.0 - LAM_INIT0)
            o_ref[0, rows, 512 + i * LANES:512 + (i + 1) * LANES] = o.astype(o_ref.dtype)

        return (lambda: _scores(_split_pair(qb_ref[0, rows, sl], lo), kv)), finish

    n_gqa = A_HEADS // 2
    _run_chains([gqa_chain(pair, r0, n) if pair < n_gqa else diff_chain(pair - n_gqa, r0, n)
                 for pair, r0, n in _chain_schedule(qa_ref.shape[1], n_gqa + B_HEADS, ATTN_CHAIN_ROWS)],
                ATTN_LOOKAHEAD)


def _attn_ab(q_parts, x_kv, c_kv, lam_p, subln, tq, name, casts=()):
    qa, qb = q_parts
    b, nq, _ = qa.shape
    with_x = x_kv is not None
    steps = nq // tq

    def full(a):
        return pl.BlockSpec((1,) + a.shape[1:], lambda bb, i: (bb, 0, 0))

    args = [qa, qb]
    in_specs = [pl.BlockSpec((1, tq, 512), lambda bb, i: (bb, i, 0))] * 2
    if with_x:
        args += list(x_kv)
        in_specs += [full(a) for a in x_kv]
    args += list(c_kv) + [lam_p, subln]
    in_specs += [full(a) for a in c_kv] + [pl.BlockSpec(lam_p.shape, lambda bb, i: (0, 0)),
                                           pl.BlockSpec(subln.shape, lambda bb, i: (0, 0))]
    cast_specs = [pl.BlockSpec((a.shape[0] // (b * steps), a.shape[1]), lambda bb, i: (bb * steps + i, 0))
                  for a in casts]
    outs = pl.pallas_call(
        functools.partial(_attn_ab_kernel, with_x=with_x, n_casts=len(casts)),
        grid=(b, steps),
        in_specs=in_specs + cast_specs,
        out_specs=[pl.BlockSpec((1, tq, D_MODEL), lambda bb, i: (bb, i, 0))] + cast_specs,
        out_shape=[jax.ShapeDtypeStruct((b, nq, D_MODEL), BF16)]
                  + [jax.ShapeDtypeStruct(a.shape, BF16) for a in casts],
        compiler_params=_cparams(("parallel", "parallel")),
        name=name,
    )(*args, *casts)
    return outs[0] if not casts else outs


FFN_CHUNK = 1024
POST_EDGE_PIECES = 2


def _post_kernel(*refs, n_parts, mod_row):
    o_refs = refs[:n_parts]
    wo_refs = refs[n_parts:2 * n_parts]
    x_ref, mod_ref, g_ref, b_ref, w1_ref, w2_ref, y_ref = refs[2 * n_parts:]
    _, _, gate1, shift2, scale2, gate2 = _mod_vectors(mod_ref, mod_row)
    tm = x_ref.shape[1]
    n_chunks = FFN_DIM // FFN_CHUNK
    pieces = [slice(r, r + tm // POST_EDGE_PIECES) for r in range(0, tm, tm // POST_EDGE_PIECES)]
    sq_relu = lambda t: jnp.square(jnp.maximum(t, 0.0)).astype(BF16)

    ys = []
    for rows in pieces:
        y = _dot(o_refs[0][0, rows, :], wo_refs[0][...])
        for o_ref, w_ref in zip(o_refs[1:], wo_refs[1:]):
            y = y + _dot(o_ref[0, rows, :], w_ref[...])
        ys.append(y)
    x1s, us, h0s = [], [], []
    for rows, y in zip(pieces, ys):
        x1s.append(_layer_norm(ALPHA * x_ref[0, rows, :] + gate1 * y, g_ref[0:1, :], b_ref[0:1, :]))
        us.append((x1s[-1] * (1.0 + scale2) + shift2).astype(BF16))
        h0s.append(sq_relu(_dot(us[-1], w1_ref[0, :, 0:FFN_CHUNK])))
    u = jnp.concatenate(us, axis=0)
    acc = _dot(jnp.concatenate(h0s, axis=0), w2_ref[0, 0:FFN_CHUNK, :])
    for c in range(1, n_chunks - 1):
        sl = slice(c * FFN_CHUNK, (c + 1) * FFN_CHUNK)
        acc = acc + _dot(sq_relu(_dot(u, w1_ref[0, :, sl])), w2_ref[0, sl, :])
    sl = slice((n_chunks - 1) * FFN_CHUNK, n_chunks * FFN_CHUNK)
    h_last = sq_relu(_dot(u, w1_ref[0, :, sl]))
    fs = [acc[rows] + _dot(h_last[rows], w2_ref[0, sl, :]) for rows in pieces]
    for rows, x1, f in zip(pieces, x1s, fs):
        y_ref[0, rows, :] = _layer_norm(ALPHA * x1 + gate2 * f, g_ref[1:2, :], b_ref[1:2, :])


def _post(o_parts, wo_parts, resid, mod, mod_row, ln_g, ln_b, w1, w2, layer, tm, name):
    g, r, _ = resid.shape
    n = len(o_parts)
    single = pl.Buffered(1)
    layer_w = lambda a: pl.BlockSpec((1,) + a.shape[1:], lambda gg, i: (layer, 0, 0), pipeline_mode=single)
    const1 = lambda a: pl.BlockSpec(a.shape, lambda gg, i: (0,) * a.ndim, pipeline_mode=single)
    in_specs = ([_tok_spec(tm, o.shape[2]) for o in o_parts] + [const1(w) for w in wo_parts]
                + [_tok_spec(tm, D_MODEL), MOD_SPEC, _const_spec(ln_g.shape), _const_spec(ln_b.shape),
                   layer_w(w1), layer_w(w2)])
    return pl.pallas_call(
        functools.partial(_post_kernel, n_parts=n, mod_row=mod_row),
        grid=(g, r // tm),
        in_specs=in_specs,
        out_specs=_tok_spec(tm, D_MODEL),
        out_shape=jax.ShapeDtypeStruct((g, r, D_MODEL), F32),
        compiler_params=_cparams(("parallel", "parallel")),
        name=name,
    )(*o_parts, *wo_parts, resid, mod, ln_g, ln_b, w1, w2)


CD_Q, CD_F, CD_K, CD_V, CD_W = 0, 512, 1024, 1536, 2048


def _inproj_cd_kernel(*refs, with_q, mod_row):
    if with_q:
        x_ref, mod_ref, w_ref, wa_ref, wb_ref, q_ref, fa_ref, fb_ref, k_ref, v_ref = refs
    else:
        x_ref, mod_ref, w_ref, k_ref, v_ref = refs
    shift, scale = _mod_vectors(mod_ref, mod_row)[0:2]
    u = (x_ref[0] * (1.0 + scale) + shift).astype(BF16)
    if with_q:
        q_ref[0] = (_dot(u, w_ref[:, CD_Q:CD_F]) * Q_SCALE).astype(q_ref.dtype)
        fa_ref[0] = _dot(u, wa_ref[...]).astype(fa_ref.dtype)
        fb_ref[0] = _dot(u, wb_ref[...]).astype(fb_ref.dtype)
    kv = _dot(u, w_ref[:, CD_K:CD_W])
    k_ref[0] = kv[:, :CD_V - CD_K].astype(k_ref.dtype)
    _store_v_ext(v_ref, slice(None), kv[:, CD_V - CD_K:])


def _inproj_cd(tok, mod, mod_row, w, w_ab, tm, name):
    g, r, _ = tok.shape
    with_q = w_ab is not None
    weights = [w] + (list(w_ab) if with_q else [])
    widths = ((512, 512, 512) if with_q else ()) + (512, 1024)
    return pl.pallas_call(
        functools.partial(_inproj_cd_kernel, with_q=with_q, mod_row=mod_row),
        grid=(g, r // tm),
        in_specs=[_tok_spec(tm, D_MODEL), MOD_SPEC] + [_const_spec(a.shape) for a in weights],
        out_specs=[_tok_spec(tm, wd) for wd in widths],
        out_shape=[jax.ShapeDtypeStruct((g, r, wd), BF16) for wd in widths],
        compiler_params=_cparams(("parallel", "parallel")),
        name=name,
    )(tok, mod, *weights)


def _dft_weight_kernel(w_ref, cc_ref, sc_ref, wa_ref, wb_ref):
    c_hi, c_mid, _ = _split3(cc_ref[...])
    s_hi, s_mid, _ = _split3(sc_ref[...])
    for gi in range(D_GROUPS):
        sl = slice(gi * D_GROUP_DIM, (gi + 1) * D_GROUP_DIM)
        w_hi, w_mid, _ = _split3(w_ref[:, sl])
        wa_ref[:, sl] = (_dot(w_hi, c_hi) + (_dot(w_mid, c_hi) + _dot(w_hi, c_mid))).astype(BF16)
        wb_ref[:, sl] = (_dot(w_hi, s_hi) + (_dot(w_mid, s_hi) + _dot(w_hi, s_mid))).astype(BF16)


def _dft_weights(w_in, cc, sc):
    width = CD_K - CD_F
    out = pl.BlockSpec((D_MODEL, width), lambda i: (0, 0))
    return pl.pallas_call(
        _dft_weight_kernel,
        grid=(1,),
        in_specs=[pl.BlockSpec((D_MODEL, width), lambda i: (0, CD_F // width)),
                  pl.BlockSpec(cc.shape, lambda i: (0, 0)), pl.BlockSpec(sc.shape, lambda i: (0, 0))],
        out_specs=[out, out],
        out_shape=[jax.ShapeDtypeStruct((D_MODEL, width), BF16)] * 2,
        compiler_params=_cparams(("arbitrary",)),
        name="dft_weights",
    )(w_in, cc, sc)


DFT_ROWS = 512


def _dft_time_kernel(a_ref, b_ref, ct_ref, sn_ref, o_ref):
    a = a_ref[0]
    bm = b_ref[0]
    for r in range(SEQ // DFT_ROWS):
        sl = slice(r * DFT_ROWS, (r + 1) * DFT_ROWS)
        y = _dot(ct_ref[sl, :], a) + _dot(sn_ref[sl, :], bm)
        o_ref[0, sl, :] = y.astype(o_ref.dtype)


def _dft_time(a, bm, ct, sn):
    b = a.shape[0]
    single = pl.Buffered(1)
    tok = pl.BlockSpec((1, SEQ, 512), lambda bb: (bb, 0, 0))
    tab = pl.BlockSpec((SEQ, SEQ), lambda bb: (0, 0), pipeline_mode=single)
    return pl.pallas_call(
        _dft_time_kernel,
        grid=(b,),
        in_specs=[tok, tok, tab, tab],
        out_specs=tok,
        out_shape=jax.ShapeDtypeStruct((b, SEQ, 512), BF16),
        compiler_params=_cparams(("parallel",)),
        name="dft_time",
    )(a, bm, ct, sn)


def _bias_table_kernel(r_ref, o_ref):
    n = GRID_W * LANES
    row = lax.broadcasted_iota(jnp.int32, (2 * 32, n), 0)
    lane = lax.broadcasted_iota(jnp.int32, (2 * 32, n), 1)
    e_row, i_row = row >> 5, row & 31
    c = lane >> 7
    e_lane = (lane >> 6) & 1
    kc = lane & (GRID_W - 1)
    sel = jnp.where((e_row == e_lane) & (kc - c + (WIN_W - 1) == i_row), 1.0, 0.0).astype(BF16)
    hi, mid, lo = _split3(r_ref[...])
    t = _dot(hi, sel) + _dot(mid, sel) + _dot(lo, sel)
    c1 = c[0:1]
    kc1 = kc[0:1]
    cstart = jnp.clip(c1 - WIN_W // 2, 0, GRID_W - WIN_W)
    col_ok = (kc1 >= cstart) & (kc1 < cstart + WIN_W)
    o_ref[...] = jnp.where(col_ok, t * LOG2E, NEG)


def _bias_table(rpb):
    n_dr = 2 * WIN_H - 1
    p = jnp.pad(rpb, ((0, 0), (CB_OFFSET, CB_ENTRIES + 1 - n_dr - CB_OFFSET), (0, 1)))
    r2 = jnp.stack([p[:, 0:CB_ENTRIES], p[:, 1:CB_ENTRIES + 1]], axis=2)
    r2 = r2.reshape(C_HEADS * CB_ENTRIES, 2 * 32)
    t = pl.pallas_call(
        _bias_table_kernel,
        out_shape=jax.ShapeDtypeStruct((C_HEADS * CB_ENTRIES, GRID_W * LANES), F32),
        compiler_params=pltpu.CompilerParams(vmem_limit_bytes=VMEM_LIMIT),
        name="natten_bias_table",
    )(r2)
    return t.reshape(C_HEADS, CB_ENTRIES, GRID_W, LANES)


NAT_BLOCKS = 4


def _natten_block(blk, q_ref, k_ref, v_ref, kc_ref, vc_ref, cb_ref, o_ref, lo, e):
    rows = slice(blk * NAT_Q, (blk + 1) * NAT_Q)
    r0 = NAT_ROWS * (NAT_BLOCKS * pl.program_id(1) + blk)
    ws = jnp.clip(r0 - WIN_H // 2, 0, GRID_H - NAT_WIN_ROWS)
    koff = pl.multiple_of(ws * GRID_W, LANES)

    entry = []
    rmask = []
    for qr in range(NAT_ROWS):
        r = r0 + qr
        lo_r = jnp.clip(r - WIN_H // 2, 0, GRID_H - WIN_H) - r + (WIN_H - 1)
        ent_q, mask_q = [], []
        for j in range(NAT_TILES):
            dr0 = ws + 2 * j - r + (WIN_H - 1)
            ent_q.append(dr0 + CB_OFFSET)
            dr = dr0 + e
            ok = (dr >= lo_r) & (dr < lo_r + WIN_H)
            mask_q.append(jnp.where(ok, 0.0, NEG))
        entry.append(ent_q)
        rmask.append(mask_q)

    states = []
    for p in range(C_HEADS // 2):
        sl = slice(p * LANES, (p + 1) * LANES)
        sl2 = slice(2 * p * LANES, (2 * p + 2) * LANES)
        q2 = _split_pair(q_ref[0, rows, sl], lo)
        bias = jnp.concatenate(
            [jnp.concatenate([cb_ref[2 * p + hh, entry[qr][j]] + rmask[qr][j] for j in range(NAT_TILES)], axis=1)
             for hh in range(2) for qr in range(NAT_ROWS)], axis=0)
        kv = [(k_ref[0, pl.ds(koff, NAT_WIN), sl], v_ref[0, pl.ds(koff, NAT_WIN), sl2]),
              (kc_ref[0, :, sl], vc_ref[0, :, sl2])]
        states.append((sl, kv, _scores(q2, kv, bias)))

    def finish():
        for sl, kv, scores in states:
            o2 = _softmax_pv(None, kv, scores=scores)
            o_ref[0, rows, sl] = jnp.where(lo, o2[:NAT_Q], o2[NAT_Q:]).astype(o_ref.dtype)

    return finish


def _natten_kernel(q_ref, k_ref, v_ref, kc_ref, vc_ref, cb_ref, o_ref):
    lane = lax.broadcasted_iota(jnp.int32, (1, LANES), 1)
    lo = lane < HEAD_DIM
    e = jnp.where(lo, 0, 1)
    pending = None
    for blk in range(NAT_BLOCKS):
        finish = _natten_block(blk, q_ref, k_ref, v_ref, kc_ref, vc_ref, cb_ref, o_ref, lo, e)
        if pending is not None:
            pending()
        pending = finish
    pending()


def _natten(q, k, v, kc, vc, cb):
    b = q.shape[0]
    tq = NAT_BLOCKS * NAT_Q
    qspec = pl.BlockSpec((1, tq, 512), lambda bb, i: (bb, i, 0))
    full = lambda a: pl.BlockSpec((1,) + a.shape[1:], lambda bb, i: (bb, 0, 0))
    return pl.pallas_call(
        _natten_kernel,
        grid=(b, SEQ // tq),
        in_specs=[qspec, full(k), full(v), full(kc), full(vc),
                  pl.BlockSpec(cb.shape, lambda bb, i: (0, 0, 0, 0))],
        out_specs=qspec,
        out_shape=jax.ShapeDtypeStruct((b, SEQ, 512), BF16),
        compiler_params=_cparams(("parallel", "parallel")),
        name="natten",
    )(q, k, v, kc, vc, cb)


def kernel(x, c, ctx, c_ctx, mod_w, mod_b, ln_g, ln_b, ffn_w1, ffn_w2, ab_w_in, ab_w_out, a_q_norm, a_k_norm,
           b_lambda, b_subln, cd_w_in, cd_w_out, c_rpb):
    b = x.shape[0]
    nctx = b * CTX_LEN
    tm = 512
    tmc = min(tm, nctx)

    cs = jnp.concatenate([c, c_ctx[None, :], jnp.zeros((MOD_ROWS - b - 1, D_MODEL), F32)], axis=0)
    mods = _modulation(cs, mod_w, mod_b)
    ctx_row = b

    cflat = ctx.reshape(1, nctx, D_MODEL)

    perm = np.asarray(QA_PERM)
    w_ab = ab_w_in.reshape(D_MODEL, AB_W).astype(BF16)
    w_out = ab_w_out.reshape(D_MODEL, D_MODEL)
    w_out_ab = jnp.concatenate([w_out[:512].reshape(A_HEADS, HEAD_DIM, D_MODEL)[perm].reshape(512, D_MODEL),
                                w_out[512:]], axis=0).astype(BF16)
    gq = jnp.tile(a_q_norm[0], A_HEADS)[None, :]
    gk = jnp.tile(a_k_norm[0], A_KV_HEADS)[None, :]
    rope_tabs = tuple(jnp.asarray(t) for t in _rope_tables())
    subln = b_subln[0][None, :]
    mod0 = mods[0]

    xq = _inproj_ab(x, mod0, None, w_ab, gq, gk, rope_tabs, 2 * tm, "inproj_ab_x")
    cq = _inproj_ab(cflat, mod0, ctx_row, w_ab, gq, gk, None, tmc, "inproj_ab_ctx")
    cq = [a.reshape(b, CTX_LEN, a.shape[2]) for a in cq]
    qa_x, qb_x, ka_x, kb_x, va_x, vb_x = xq
    qa_c, qb_c, ka_c, kb_c, va_c, vb_c = cq
    c_kv = (ka_c, va_c, kb_c, vb_c)
    o_x, w1, w2, w_cd, w_out_cd = _attn_ab(
        (qa_x, qb_x), (ka_x, va_x, kb_x, vb_x), c_kv, b_lambda[0], subln, 512, "attn_ab_x",
        casts=(ffn_w1.reshape(DEPTH * D_MODEL, FFN_DIM), ffn_w2.reshape(DEPTH * FFN_DIM, D_MODEL),
               cd_w_in.reshape(D_MODEL, CD_W), cd_w_out.reshape(D_MODEL, D_MODEL)))
    w1 = w1.reshape(DEPTH, D_MODEL, FFN_DIM)
    w2 = w2.reshape(DEPTH, FFN_DIM, D_MODEL)
    o_c = _attn_ab((qa_c, qb_c), None, c_kv, b_lambda[0], subln, CTX_LEN, "attn_ab_ctx")

    x2 = _post([o_x], [w_out_ab], x, mod0, None, ln_g[0], ln_b[0], w1, w2, 0, tm, "post0_x")
    c2 = _post([o_c.reshape(1, nctx, D_MODEL)], [w_out_ab], cflat, mod0, ctx_row, ln_g[0], ln_b[0], w1, w2, 0,
               tmc, "post0_ctx")

    mod1 = mods[1]
    w_in = cd_w_in.reshape(D_MODEL, CD_W)
    ct, sn, cc, sc = _dft_tables()
    w_fab = _dft_weights(w_in, jnp.asarray(cc), jnp.asarray(sc))
    q_n, f_a, f_b, k_n, v_n = _inproj_cd(x2, mod1, None, w_cd, w_fab, 2 * tm, "inproj_cd_x")
    kc_n, vc_n = _inproj_cd(c2, mod1, ctx_row, w_cd, None, tmc, "inproj_cd_ctx")
    kc_n = kc_n.reshape(b, CTX_LEN, 512)
    vc_n = vc_n.reshape(b, CTX_LEN, 1024)

    o_d = _dft_time(f_a, f_b, jnp.asarray(ct).astype(BF16), jnp.asarray(sn).astype(BF16))
    cb = _bias_table(c_rpb[0])
    o_n = _natten(q_n, k_n, v_n, kc_n, vc_n, cb)

    return _post([o_n, o_d], [w_out_cd[:512], w_out_cd[512:]], x2, mod1, None, ln_g[1], ln_b[1], w1, w2, 1, tm, "post1_x")
```

```python
import functools
import math

import numpy as np
import jax
import jax.numpy as jnp
from jax import lax
from jax.experimental import pallas as pl
from jax.experimental.pallas import tpu as pltpu

F32 = jnp.float32
BF16 = jnp.bfloat16

D_MODEL = 1024
SEQ = 2048
CTX_LEN = 256
DEPTH = 2
GRID_W = 64
GRID_H = SEQ // GRID_W
HEAD_DIM = 64
A_HEADS = 8
A_KV_HEADS = 2
B_HEADS = 4
C_HEADS = 8
WIN_H = 8
WIN_W = 16
D_GROUPS = 4
D_GROUP_DIM = 128
FFN_DIM = 4 * D_MODEL
ROPE_THETA = 10000.0
ALPHA = (2.0 * DEPTH) ** 0.25
EPS = 1e-6
ATTN_SCALE = HEAD_DIM ** -0.5
LOG2E = math.log2(math.e)
Q_SCALE = ATTN_SCALE * LOG2E
LAM_INIT0 = 0.8 - 0.6 * math.exp(-0.3 * 0)
NEG = -1e30

LANES = 128
MOD_ROWS = 16
VMEM_LIMIT = 56 * 1024 * 1024

QA_PERM = (0, 4, 1, 5, 2, 6, 3, 7)

NAT_ROWS = 4
NAT_Q = NAT_ROWS * GRID_W
NAT_WIN_ROWS = NAT_ROWS + WIN_H
NAT_WIN = NAT_WIN_ROWS * GRID_W
NAT_TILES = NAT_WIN_ROWS // 2
_NAT_DR0 = [min(max(r0 - WIN_H // 2, 0), GRID_H - NAT_WIN_ROWS) + 2 * j - (r0 + qr) + WIN_H - 1
            for r0 in range(0, GRID_H, NAT_ROWS) for qr in range(NAT_ROWS) for j in range(NAT_TILES)]
CB_OFFSET = -min(_NAT_DR0)
CB_ENTRIES = max(_NAT_DR0) + CB_OFFSET + 1


def _cparams(sem, flags=None):
    return pltpu.CompilerParams(dimension_semantics=sem, vmem_limit_bytes=VMEM_LIMIT, flags=flags)


def _dot(a, b):
    return jnp.dot(a, b, preferred_element_type=F32)


def _dot_nt(a, b):
    return lax.dot_general(a, b, (((1,), (1,)), ((), ())), preferred_element_type=F32)


def _split3(a):
    hi = a.astype(BF16)
    r1 = a - hi.astype(F32)
    mid = r1.astype(BF16)
    lo = (r1 - mid.astype(F32)).astype(BF16)
    return hi, mid, lo


def _layer_norm(z, g, b):
    mu = jnp.mean(z, axis=-1, keepdims=True)
    zc = z - mu
    var = jnp.mean(zc * zc, axis=-1, keepdims=True)
    return zc * lax.rsqrt(var + EPS) * g + b


@functools.lru_cache(maxsize=None)
def _rope_tables():
    pos = np.arange(SEQ)
    row = (pos // GRID_W).astype(np.float64)
    col = (pos % GRID_W).astype(np.float64)
    half = HEAD_DIM // 2
    freqs = np.power(ROPE_THETA, -np.arange(0, half, 2, dtype=np.float64) / half)
    def axis_angles(p):
        a = p[:, None] * freqs[None, :]
        return np.concatenate([a, a], axis=-1)
    ang = np.concatenate([axis_angles(row), axis_angles(col)], axis=-1)
    cos, sin = np.cos(ang), np.sin(ang)
    first = (np.arange(HEAD_DIM) % half) < (half // 2)
    sin_a = np.where(first[None, :], -sin, 0.0)
    sin_b = np.where(first[None, :], 0.0, sin)
    tile = lambda t: np.tile(t, (1, LANES // HEAD_DIM)).astype(np.float32)
    return tile(cos), tile(sin_a), tile(sin_b)


@functools.lru_cache(maxsize=None)
def _group_mean_matrix(width):
    g = np.arange(width) // HEAD_DIM
    return (g[:, None] == g[None, :]).astype(np.float32) / HEAD_DIM


@functools.lru_cache(maxsize=None)
def _dft_tables():
    t = np.arange(SEQ, dtype=np.int64)
    k = (t[:, None] * t[None, :]) % SEQ
    ang = 2.0 * np.pi * k.astype(np.float64) / SEQ
    ct = np.cos(ang).astype(np.float32)
    sn = (-np.sin(ang)).astype(np.float32)
    c = np.arange(D_GROUP_DIM, dtype=np.int64)
    kc = (c[:, None] * c[None, :]) % D_GROUP_DIM
    angc = 2.0 * np.pi * kc.astype(np.float64) / D_GROUP_DIM
    norm = 1.0 / math.sqrt(SEQ * D_GROUP_DIM)
    return ct, sn, (np.cos(angc) * norm).astype(np.float32), (np.sin(angc) * norm).astype(np.float32)


def _mod_kernel(c_ref, w_ref, b_ref, o_ref):
    c = c_ref[...]
    a = c / (1.0 + jnp.exp(-c))
    a_hi, a_mid, _ = _split3(a)
    y = _dot(jnp.concatenate([a_hi, a_mid], axis=0), w_ref[0].astype(BF16))
    o_ref[0, 0] = y[:MOD_ROWS] + y[MOD_ROWS:] + b_ref[0]


def _modulation(cs, mod_w, mod_b):
    n = 6 * D_MODEL
    return pl.pallas_call(
        _mod_kernel,
        grid=(DEPTH, 6),
        in_specs=[pl.BlockSpec((MOD_ROWS, D_MODEL), lambda l, j: (0, 0)),
                  pl.BlockSpec((1, D_MODEL, D_MODEL), lambda l, j: (l, 0, j)),
                  pl.BlockSpec((1, 1, D_MODEL), lambda l, j: (l, 0, j))],
        out_specs=pl.BlockSpec((1, 1, MOD_ROWS, D_MODEL), lambda l, j: (l, j, 0, 0)),
        out_shape=jax.ShapeDtypeStruct((DEPTH, 6, MOD_ROWS, D_MODEL), F32),
        compiler_params=_cparams(("arbitrary", "arbitrary")),
        name="modulation",
    )(cs, mod_w, mod_b.reshape(DEPTH, 1, n))


def _tok_spec(tm, width):
    return pl.BlockSpec((1, tm, width), lambda g, i: (g, i, 0))


MOD_SPEC = pl.BlockSpec((6, MOD_ROWS, D_MODEL), lambda g, i: (0, 0, 0))


def _mod_vectors(mod_ref, mod_row):
    r = pl.program_id(0) if mod_row is None else mod_row
    return [mod_ref[k, pl.ds(r, 1), :] for k in range(6)]


def _const_spec(shape):
    nd = len(shape)
    return pl.BlockSpec(shape, lambda g, i: (0,) * nd)


AB_QA, AB_QB, AB_KA, AB_VA, AB_KB, AB_VB, AB_W = 0, 512, 1024, 1152, 1280, 1792, 2304


TOKEN_CHAIN_ROWS = 256


def _store_v_ext(o_ref, rows, v):
    ones = jnp.ones((v.shape[0], LANES), o_ref.dtype)
    for j in range(v.shape[1] // LANES):
        o_ref[0, rows, 2 * j * LANES:(2 * j + 1) * LANES] = v[:, j * LANES:(j + 1) * LANES].astype(o_ref.dtype)
        o_ref[0, rows, (2 * j + 1) * LANES:(2 * j + 2) * LANES] = ones


def _rope_chunk(c, cos, sin_a, sin_b):
    return c * cos + pltpu.roll(c, LANES - 16, axis=1) * sin_a + pltpu.roll(c, 16, axis=1) * sin_b


def _inproj_ab_kernel(*refs, rope, mod_row):
    if rope:
        (x_ref, mod_ref, w_ref, gq_ref, gk_ref, g512_ref, g128_ref, cos_ref, sa_ref, sb_ref,
         qa_ref, qb_ref, ka_ref, kb_ref, va_ref, vb_ref) = refs
    else:
        (x_ref, mod_ref, w_ref, gq_ref, gk_ref, g512_ref, g128_ref,
         qa_ref, qb_ref, ka_ref, kb_ref, va_ref, vb_ref) = refs

    def rms(v, g_ref, gm_ref):
        ms = _dot((v * v).astype(BF16), gm_ref[...])
        return v * lax.rsqrt(ms + EPS) * g_ref[...]

    shift, scale = _mod_vectors(mod_ref, mod_row)[0:2]
    lo = lax.broadcasted_iota(jnp.int32, (1, LANES), 1) < HEAD_DIM
    tm = x_ref.shape[1]
    tc = min(TOKEN_CHAIN_ROWS, tm)

    def project(rb):
        rows = slice(rb * tc, (rb + 1) * tc)
        u = (x_ref[0, rows, :] * (1.0 + scale) + shift).astype(BF16)
        return _dot(u, w_ref[...])

    def epilogue(rb, h):
        rows = slice(rb * tc, (rb + 1) * tc)
        if rope:
            cos, sin_a, sin_b = cos_ref[rows, :], sa_ref[rows, :], sb_ref[rows, :]

        def chunks(v, q_scale):
            out = []
            for j in range(v.shape[1] // LANES):
                c = v[:, j * LANES:(j + 1) * LANES]
                if rope:
                    c = _rope_chunk(c, cos, sin_a, sin_b)
                out.append(c * q_scale if q_scale != 1.0 else c)
            return out

        def store(o_ref, cs):
            for j, c in enumerate(cs):
                o_ref[0, rows, j * LANES:(j + 1) * LANES] = c.astype(o_ref.dtype)

        qa = chunks(rms(h[:, AB_QA:AB_QB], gq_ref, g512_ref), Q_SCALE)
        qa_perm = []
        for p in range(A_HEADS // 2):
            a, b = qa[p // 2], qa[A_HEADS // 4 + p // 2]
            if p % 2 == 0:
                qa_perm.append(jnp.where(lo, a, pltpu.roll(b, HEAD_DIM, axis=1)))
            else:
                qa_perm.append(jnp.where(lo, pltpu.roll(a, HEAD_DIM, axis=1), b))
        store(qa_ref, qa_perm)
        store(qb_ref, chunks(h[:, AB_QB:AB_KA], Q_SCALE))
        store(ka_ref, chunks(rms(h[:, AB_KA:AB_VA], gk_ref, g128_ref), 1.0))
        store(kb_ref, chunks(h[:, AB_KB:AB_VB], 1.0))
        _store_v_ext(va_ref, rows, h[:, AB_VA:AB_KB])
        _store_v_ext(vb_ref, rows, h[:, AB_VB:AB_W])

    for rb in range(tm // tc):
        epilogue(rb, project(rb))


def _inproj_ab(tok, mod, mod_row, w, gq, gk, rope_tabs, tm, name):
    g, r, _ = tok.shape
    rope = rope_tabs is not None
    consts = [w, gq, gk, jnp.asarray(_group_mean_matrix(512), BF16), jnp.asarray(_group_mean_matrix(128), BF16)]
    in_specs = [_tok_spec(tm, D_MODEL), MOD_SPEC] + [_const_spec(c.shape) for c in consts]
    args = [tok, mod] + consts
    if rope:
        in_specs += [pl.BlockSpec((tm, LANES), lambda gg, i: (i, 0))] * 3
        args += list(rope_tabs)
    widths = (512, 512, 128, 512, 2 * 128, 2 * 512)
    return pl.pallas_call(
        functools.partial(_inproj_ab_kernel, rope=rope, mod_row=mod_row),
        grid=(g, r // tm),
        in_specs=in_specs,
        out_specs=[_tok_spec(tm, wd) for wd in widths],
        out_shape=[jax.ShapeDtypeStruct((g, r, wd), BF16) for wd in widths],
        compiler_params=_cparams(("parallel", "parallel")),
        name=name,
    )(*args)


def _scores(q2, kv_pairs, bias=None):
    scores = [_dot_nt(q2, k) for k, _ in kv_pairs]
    if bias is not None:
        scores[0] = scores[0] + bias
    return scores


def _softmax_pv(q2, kv_pairs, bias=None, scores=None):
    if scores is None:
        scores = _scores(q2, kv_pairs, bias)
    m = scores[0].max(axis=-1, keepdims=True)
    for s in scores[1:]:
        m = jnp.maximum(m, s.max(axis=-1, keepdims=True))
    acc = None
    for s, (_, v) in zip(scores, kv_pairs):
        a = _dot(jnp.exp2(s - m).astype(BF16), v)
        acc = a if acc is None else acc + a
    return acc[:, :LANES] / acc[:, LANES:]


def _run_chains(chains, lookahead):
    pending = []
    for start, finish in chains:
        pending.append((finish, start()))
        if len(pending) > lookahead:
            fin, scores = pending.pop(0)
            fin(scores)
    for fin, scores in pending:
        fin(scores)


ATTN_LOOKAHEAD = 3


def _split_pair(qp, lo):
    zero = jnp.zeros_like(qp)
    return jnp.concatenate([jnp.where(lo, qp, zero), jnp.where(lo, zero, qp)], axis=0)


ATTN_CHAIN_ROWS = 256


def _chain_schedule(rows, n_pairs, chain_rows):
    blocks = [(r0, min(chain_rows, rows - r0)) for r0 in range(0, rows, chain_rows)]
    sched = [(p, r0, n) for r0, n in blocks for p in range(n_pairs)]
    if len(blocks) < 2:
        return sched
    (p, r0, n), rest = sched[0], sched[1:]
    sched = [(p, r0, n // 2), (p, r0 + n // 2, n - n // 2)] + rest
    (p, r0, n), rest = sched[-1], sched[:-1]
    return rest + [(p, r0, n - n // 2), (p, r0 + n - n // 2, n // 2)]


def _attn_ab_kernel(*refs, with_x, n_casts):
    if n_casts:
        cast_in = refs[len(refs) - 2 * n_casts - 1:len(refs) - n_casts - 1]
        cast_out = refs[len(refs) - n_casts:]
        refs = refs[:len(refs) - 2 * n_casts - 1] + (refs[len(refs) - n_casts - 1],)
        for src, dst in zip(cast_in, cast_out):
            dst[...] = src[...].astype(dst.dtype)
    if with_x:
        (qa_ref, qb_ref, kax_ref, vax_ref, kbx_ref, vbx_ref, kac_ref, vac_ref, kbc_ref, vbc_ref,
         lam_ref, subln_ref, o_ref) = refs
    else:
        (qa_ref, qb_ref, kac_ref, vac_ref, kbc_ref, vbc_ref, lam_ref, subln_ref, o_ref) = refs
    lo = lax.broadcasted_iota(jnp.int32, (1, LANES), 1) < HEAD_DIM

    lp = lam_ref[...]
    lam = (jnp.exp(jnp.sum(lp[0:1] * lp[1:2], axis=-1, keepdims=True))
           - jnp.exp(jnp.sum(lp[2:3] * lp[3:4], axis=-1, keepdims=True)) + LAM_INIT0)

    def gqa_chain(p, r0, n):
        rows = slice(r0, r0 + n)
        sl = slice(p * LANES, (p + 1) * LANES)
        kv = [(kac_ref[0], vac_ref[0])]
        if with_x:
            kv.append((kax_ref[0], vax_ref[0]))

        def finish(scores):
            o2 = _softmax_pv(None, kv, scores=scores)
            o_ref[0, rows, sl] = jnp.where(lo, o2[:n], o2[n:]).astype(o_ref.dtype)

        return (lambda: _scores(_split_pair(qa_ref[0, rows, sl], lo), kv)), finish

    def diff_chain(i, r0, n):
        rows = slice(r0, r0 + n)
        sl = slice(i * LANES, (i + 1) * LANES)
        sl2 = slice(2 * i * LANES, (2 * i + 2) * LANES)
        kv = [(kbc_ref[0, :, sl], vbc_ref[0, :, sl2])]
        if with_x:
            kv.append((kbx_ref[0, :, sl], vbx_ref[0, :, sl2]))

        def finish(scores):
            o2 = _softmax_pv(None, kv, scores=scores)
            o = o2[:n] - lam * o2[n:]
            ms = jnp.mean(o * o, axis=-1, keepdims=True)
            o = o * lax.rsqrt(ms + EPS) * subln_ref[...] * (1.0 - LAM_INIT0)
            o_ref[0, rows, 512 + i * LANES:512 + (i + 1) * LANES] = o.astype(o_ref.dtype)

        return (lambda: _scores(_split_pair(qb_ref[0, rows, sl], lo), kv)), finish

    n_gqa = A_HEADS // 2
    _run_chains([gqa_chain(pair, r0, n) if pair < n_gqa else diff_chain(pair - n_gqa, r0, n)
                 for pair, r0, n in _chain_schedule(qa_ref.shape[1], n_gqa + B_HEADS, ATTN_CHAIN_ROWS)],
                ATTN_LOOKAHEAD)


def _attn_ab(q_parts, x_kv, c_kv, lam_p, subln, tq, name, casts=()):
    qa, qb = q_parts
    b, nq, _ = qa.shape
    with_x = x_kv is not None
    steps = nq // tq

    def full(a):
        return pl.BlockSpec((1,) + a.shape[1:], lambda bb, i: (bb, 0, 0))

    args = [qa, qb]
    in_specs = [pl.BlockSpec((1, tq, 512), lambda bb, i: (bb, i, 0))] * 2
    if with_x:
        args += list(x_kv)
        in_specs += [full(a) for a in x_kv]
    args += list(c_kv) + [lam_p, subln]
    in_specs += [full(a) for a in c_kv] + [pl.BlockSpec(lam_p.shape, lambda bb, i: (0, 0)),
                                           pl.BlockSpec(subln.shape, lambda bb, i: (0, 0))]
    cast_specs = [pl.BlockSpec((a.shape[0] // (b * steps), a.shape[1]), lambda bb, i: (bb * steps + i, 0))
                  for a in casts]
    outs = pl.pallas_call(
        functools.partial(_attn_ab_kernel, with_x=with_x, n_casts=len(casts)),
        grid=(b, steps),
        in_specs=in_specs + cast_specs,
        out_specs=[pl.BlockSpec((1, tq, D_MODEL), lambda bb, i: (bb, i, 0))] + cast_specs,
        out_shape=[jax.ShapeDtypeStruct((b, nq, D_MODEL), BF16)]
                  + [jax.ShapeDtypeStruct(a.shape, BF16) for a in casts],
        compiler_params=_cparams(("parallel", "parallel")),
        name=name,
    )(*args, *casts)
    return outs[0] if not casts else outs


FFN_CHUNK = 1024
POST_EDGE_PIECES = 2


def _post_kernel(*refs, n_parts, mod_row):
    o_refs = refs[:n_parts]
    wo_refs = refs[n_parts:2 * n_parts]
    x_ref, mod_ref, g_ref, b_ref, w1_ref, w2_ref, y_ref = refs[2 * n_parts:]
    _, _, gate1, shift2, scale2, gate2 = _mod_vectors(mod_ref, mod_row)
    tm = x_ref.shape[1]
    n_chunks = FFN_DIM // FFN_CHUNK
    pieces = [slice(r, r + tm // POST_EDGE_PIECES) for r in range(0, tm, tm // POST_EDGE_PIECES)]
    sq_relu = lambda t: jnp.square(jnp.maximum(t, 0.0)).astype(BF16)

    ys = []
    for rows in pieces:
        y = _dot(o_refs[0][0, rows, :], wo_refs[0][...])
        for o_ref, w_ref in zip(o_refs[1:], wo_refs[1:]):
            y = y + _dot(o_ref[0, rows, :], w_ref[...])
        ys.append(y)
    x1s, us, h0s = [], [], []
    for rows, y in zip(pieces, ys):
        x1s.append(_layer_norm(ALPHA * x_ref[0, rows, :] + gate1 * y, g_ref[0:1, :], b_ref[0:1, :]))
        us.append((x1s[-1] * (1.0 + scale2) + shift2).astype(BF16))
        h0s.append(sq_relu(_dot(us[-1], w1_ref[0, :, 0:FFN_CHUNK])))
    u = jnp.concatenate(us, axis=0)
    acc = _dot(jnp.concatenate(h0s, axis=0), w2_ref[0, 0:FFN_CHUNK, :])
    for c in range(1, n_chunks - 1):
        sl = slice(c * FFN_CHUNK, (c + 1) * FFN_CHUNK)
        acc = acc + _dot(sq_relu(_dot(u, w1_ref[0, :, sl])), w2_ref[0, sl, :])
    sl = slice((n_chunks - 1) * FFN_CHUNK, n_chunks * FFN_CHUNK)
    h_last = sq_relu(_dot(u, w1_ref[0, :, sl]))
    fs = [acc[rows] + _dot(h_last[rows], w2_ref[0, sl, :]) for rows in pieces]
    for rows, x1, f in zip(pieces, x1s, fs):
        y_ref[0, rows, :] = _layer_norm(ALPHA * x1 + gate2 * f, g_ref[1:2, :], b_ref[1:2, :])


def _post(o_parts, wo_parts, resid, mod, mod_row, ln_g, ln_b, w1, w2, layer, tm, name):
    g, r, _ = resid.shape
    n = len(o_parts)
    single = pl.Buffered(1)
    layer_w = lambda a: pl.BlockSpec((1,) + a.shape[1:], lambda gg, i: (layer, 0, 0), pipeline_mode=single)
    const1 = lambda a: pl.BlockSpec(a.shape, lambda gg, i: (0,) * a.ndim, pipeline_mode=single)
    in_specs = ([_tok_spec(tm, o.shape[2]) for o in o_parts] + [const1(w) for w in wo_parts]
                + [_tok_spec(tm, D_MODEL), MOD_SPEC, _const_spec(ln_g.shape), _const_spec(ln_b.shape),
                   layer_w(w1), layer_w(w2)])
    return pl.pallas_call(
        functools.partial(_post_kernel, n_parts=n, mod_row=mod_row),
        grid=(g, r // tm),
        in_specs=in_specs,
        out_specs=_tok_spec(tm, D_MODEL),
        out_shape=jax.ShapeDtypeStruct((g, r, D_MODEL), F32),
        compiler_params=_cparams(("parallel", "parallel")),
        name=name,
    )(*o_parts, *wo_parts, resid, mod, ln_g, ln_b, w1, w2)


CD_Q, CD_F, CD_K, CD_V, CD_W = 0, 512, 1024, 1536, 2048


def _inproj_cd_kernel(*refs, with_q, mod_row):
    if with_q:
        x_ref, mod_ref, w_ref, wa_ref, wb_ref, q_ref, fa_ref, fb_ref, k_ref, v_ref = refs
    else:
        x_ref, mod_ref, w_ref, k_ref, v_ref = refs
    shift, scale = _mod_vectors(mod_ref, mod_row)[0:2]
    u = (x_ref[0] * (1.0 + scale) + shift).astype(BF16)
    if with_q:
        q_ref[0] = (_dot(u, w_ref[:, CD_Q:CD_F]) * Q_SCALE).astype(q_ref.dtype)
        fa_ref[0] = _dot(u, wa_ref[...]).astype(fa_ref.dtype)
        fb_ref[0] = _dot(u, wb_ref[...]).astype(fb_ref.dtype)
    kv = _dot(u, w_ref[:, CD_K:CD_W])
    k_ref[0] = kv[:, :CD_V - CD_K].astype(k_ref.dtype)
    _store_v_ext(v_ref, slice(None), kv[:, CD_V - CD_K:])


def _inproj_cd(tok, mod, mod_row, w, w_ab, tm, name):
    g, r, _ = tok.shape
    with_q = w_ab is not None
    weights = [w] + (list(w_ab) if with_q else [])
    widths = ((512, 512, 512) if with_q else ()) + (512, 1024)
    return pl.pallas_call(
        functools.partial(_inproj_cd_kernel, with_q=with_q, mod_row=mod_row),
        grid=(g, r // tm),
        in_specs=[_tok_spec(tm, D_MODEL), MOD_SPEC] + [_const_spec(a.shape) for a in weights],
        out_specs=[_tok_spec(tm, wd) for wd in widths],
        out_shape=[jax.ShapeDtypeStruct((g, r, wd), BF16) for wd in widths],
        compiler_params=_cparams(("parallel", "parallel")),
        name=name,
    )(tok, mod, *weights)


def _dft_weight_kernel(w_ref, cc_ref, sc_ref, wa_ref, wb_ref):
    c_hi, c_mid, _ = _split3(cc_ref[...])
    s_hi, s_mid, _ = _split3(sc_ref[...])
    for gi in range(D_GROUPS):
        sl = slice(gi * D_GROUP_DIM, (gi + 1) * D_GROUP_DIM)
        w_hi, w_mid, _ = _split3(w_ref[:, sl])
        wa_ref[:, sl] = (_dot(w_hi, c_hi) + (_dot(w_mid, c_hi) + _dot(w_hi, c_mid))).astype(BF16)
        wb_ref[:, sl] = (_dot(w_hi, s_hi) + (_dot(w_mid, s_hi) + _dot(w_hi, s_mid))).astype(BF16)


def _dft_weights(w_in, cc, sc):
    width = CD_K - CD_F
    out = pl.BlockSpec((D_MODEL, width), lambda i: (0, 0))
    return pl.pallas_call(
        _dft_weight_kernel,
        grid=(1,),
        in_specs=[pl.BlockSpec((D_MODEL, width), lambda i: (0, CD_F // width)),
                  pl.BlockSpec(cc.shape, lambda i: (0, 0)), pl.BlockSpec(sc.shape, lambda i: (0, 0))],
        out_specs=[out, out],
        out_shape=[jax.ShapeDtypeStruct((D_MODEL, width), BF16)] * 2,
        compiler_params=_cparams(("arbitrary",)),
        name="dft_weights",
    )(w_in, cc, sc)


DFT_ROWS = 512


def _dft_time_kernel(a_ref, b_ref, ct_ref, sn_ref, o_ref):
    a = a_ref[0]
    bm = b_ref[0]
    for r in range(SEQ // DFT_ROWS):
        sl = slice(r * DFT_ROWS, (r + 1) * DFT_ROWS)
        y = _dot(ct_ref[sl, :], a) + _dot(sn_ref[sl, :], bm)
        o_ref[0, sl, :] = y.astype(o_ref.dtype)


def _dft_time(a, bm, ct, sn):
    b = a.shape[0]
    single = pl.Buffered(1)
    tok = pl.BlockSpec((1, SEQ, 512), lambda bb: (bb, 0, 0))
    tab = pl.BlockSpec((SEQ, SEQ), lambda bb: (0, 0), pipeline_mode=single)
    return pl.pallas_call(
        _dft_time_kernel,
        grid=(b,),
        in_specs=[tok, tok, tab, tab],
        out_specs=tok,
        out_shape=jax.ShapeDtypeStruct((b, SEQ, 512), BF16),
        compiler_params=_cparams(("parallel",)),
        name="dft_time",
    )(a, bm, ct, sn)


def _bias_table_kernel(r_ref, o_ref):
    n = GRID_W * LANES
    row = lax.broadcasted_iota(jnp.int32, (2 * 32, n), 0)
    lane = lax.broadcasted_iota(jnp.int32, (2 * 32, n), 1)
    e_row, i_row = row >> 5, row & 31
    c = lane >> 7
    e_lane = (lane >> 6) & 1
    kc = lane & (GRID_W - 1)
    sel = jnp.where((e_row == e_lane) & (kc - c + (WIN_W - 1) == i_row), 1.0, 0.0).astype(BF16)
    hi, mid, lo = _split3(r_ref[...])
    t = _dot(hi, sel) + _dot(mid, sel) + _dot(lo, sel)
    c1 = c[0:1]
    kc1 = kc[0:1]
    cstart = jnp.clip(c1 - WIN_W // 2, 0, GRID_W - WIN_W)
    col_ok = (kc1 >= cstart) & (kc1 < cstart + WIN_W)
    o_ref[...] = jnp.where(col_ok, t * LOG2E, NEG)


def _bias_table(rpb):
    n_dr = 2 * WIN_H - 1
    p = jnp.pad(rpb, ((0, 0), (CB_OFFSET, CB_ENTRIES + 1 - n_dr - CB_OFFSET), (0, 1)))
    r2 = jnp.stack([p[:, 0:CB_ENTRIES], p[:, 1:CB_ENTRIES + 1]], axis=2)
    r2 = r2.reshape(C_HEADS * CB_ENTRIES, 2 * 32)
    t = pl.pallas_call(
        _bias_table_kernel,
        out_shape=jax.ShapeDtypeStruct((C_HEADS * CB_ENTRIES, GRID_W * LANES), F32),
        compiler_params=pltpu.CompilerParams(vmem_limit_bytes=VMEM_LIMIT),
        name="natten_bias_table",
    )(r2)
    return t.reshape(C_HEADS, CB_ENTRIES, GRID_W, LANES)


NAT_BLOCKS = 4
NAT_LOOKAHEAD = 2


def _natten_block(blk, q_ref, k_ref, v_ref, kc_ref, vc_ref, cb_ref, o_ref, lo, e):
    rows = slice(blk * NAT_Q, (blk + 1) * NAT_Q)
    r0 = NAT_ROWS * (NAT_BLOCKS * pl.program_id(1) + blk)
    ws = jnp.clip(r0 - WIN_H // 2, 0, GRID_H - NAT_WIN_ROWS)
    koff = pl.multiple_of(ws * GRID_W, LANES)

    entry = []
    rmask = []
    for qr in range(NAT_ROWS):
        r = r0 + qr
        lo_r = jnp.clip(r - WIN_H // 2, 0, GRID_H - WIN_H) - r + (WIN_H - 1)
        ent_q, mask_q = [], []
        for j in range(NAT_TILES):
            dr0 = ws + 2 * j - r + (WIN_H - 1)
            ent_q.append(dr0 + CB_OFFSET)
            dr = dr0 + e
            ok = (dr >= lo_r) & (dr < lo_r + WIN_H)
            mask_q.append(jnp.where(ok, 0.0, NEG))
        entry.append(ent_q)
        rmask.append(mask_q)

    states = []
    for p in range(C_HEADS // 2):
        sl = slice(p * LANES, (p + 1) * LANES)
        sl2 = slice(2 * p * LANES, (2 * p + 2) * LANES)
        q2 = _split_pair(q_ref[0, rows, sl], lo)
        bias = jnp.concatenate(
            [jnp.concatenate([cb_ref[2 * p + hh, entry[qr][j]] + rmask[qr][j] for j in range(NAT_TILES)], axis=1)
             for hh in range(2) for qr in range(NAT_ROWS)], axis=0)
        kv = [(k_ref[0, pl.ds(koff, NAT_WIN), sl], v_ref[0, pl.ds(koff, NAT_WIN), sl2]),
              (kc_ref[0, :, sl], vc_ref[0, :, sl2])]
        states.append((sl, kv, _scores(q2, kv, bias)))

    def finish():
        for sl, kv, scores in states:
            o2 = _softmax_pv(None, kv, scores=scores)
            o_ref[0, rows, sl] = jnp.where(lo, o2[:NAT_Q], o2[NAT_Q:]).astype(o_ref.dtype)

    return finish


def _natten_kernel(q_ref, k_ref, v_ref, kc_ref, vc_ref, cb_ref, o_ref):
    lane = lax.broadcasted_iota(jnp.int32, (1, LANES), 1)
    lo = lane < HEAD_DIM
    e = jnp.where(lo, 0, 1)
    pending = []
    for blk in range(NAT_BLOCKS):
        pending.append(_natten_block(blk, q_ref, k_ref, v_ref, kc_ref, vc_ref, cb_ref, o_ref, lo, e))
        if len(pending) > NAT_LOOKAHEAD:
            pending.pop(0)()
    for finish in pending:
        finish()


def _natten(q, k, v, kc, vc, cb):
    b = q.shape[0]
    tq = NAT_BLOCKS * NAT_Q
    qspec = pl.BlockSpec((1, tq, 512), lambda bb, i: (bb, i, 0))
    full = lambda a: pl.BlockSpec((1,) + a.shape[1:], lambda bb, i: (bb, 0, 0))
    return pl.pallas_call(
        _natten_kernel,
        grid=(b, SEQ // tq),
        in_specs=[qspec, full(k), full(v), full(kc), full(vc),
                  pl.BlockSpec(cb.shape, lambda bb, i: (0, 0, 0, 0))],
        out_specs=qspec,
        out_shape=jax.ShapeDtypeStruct((b, SEQ, 512), BF16),
        compiler_params=_cparams(("parallel", "parallel")),
        name="natten",
    )(q, k, v, kc, vc, cb)


def kernel(x, c, ctx, c_ctx, mod_w, mod_b, ln_g, ln_b, ffn_w1, ffn_w2, ab_w_in, ab_w_out, a_q_norm, a_k_norm,
           b_lambda, b_subln, cd_w_in, cd_w_out, c_rpb):
    b = x.shape[0]
    nctx = b * CTX_LEN
    tm = 512
    tmc = min(tm, nctx)

    cs = jnp.concatenate([c, c_ctx[None, :], jnp.zeros((MOD_ROWS - b - 1, D_MODEL), F32)], axis=0)
    mods = _modulation(cs, mod_w, mod_b)
    ctx_row = b

    cflat = ctx.reshape(1, nctx, D_MODEL)

    perm = np.asarray(QA_PERM)
    w_ab = ab_w_in.reshape(D_MODEL, AB_W).astype(BF16)
    w_out = ab_w_out.reshape(D_MODEL, D_MODEL)
    w_out_ab = jnp.concatenate([w_out[:512].reshape(A_HEADS, HEAD_DIM, D_MODEL)[perm].reshape(512, D_MODEL),
                                w_out[512:]], axis=0).astype(BF16)
    gq = jnp.tile(a_q_norm[0], A_HEADS)[None, :]
    gk = jnp.tile(a_k_norm[0], A_KV_HEADS)[None, :]
    rope_tabs = tuple(jnp.asarray(t) for t in _rope_tables())
    subln = b_subln[0][None, :]
    mod0 = mods[0]

    xq = _inproj_ab(x, mod0, None, w_ab, gq, gk, rope_tabs, 2 * tm, "inproj_ab_x")
    cq = _inproj_ab(cflat, mod0, ctx_row, w_ab, gq, gk, None, tmc, "inproj_ab_ctx")
    cq = [a.reshape(b, CTX_LEN, a.shape[2]) for a in cq]
    qa_x, qb_x, ka_x, kb_x, va_x, vb_x = xq
    qa_c, qb_c, ka_c, kb_c, va_c, vb_c = cq
    c_kv = (ka_c, va_c, kb_c, vb_c)
    o_x, w1, w2, w_cd, w_out_cd = _attn_ab(
        (qa_x, qb_x), (ka_x, va_x, kb_x, vb_x), c_kv, b_lambda[0], subln, 512, "attn_ab_x",
        casts=(ffn_w1.reshape(DEPTH * D_MODEL, FFN_DIM), ffn_w2.reshape(DEPTH * FFN_DIM, D_MODEL),
               cd_w_in.reshape(D_MODEL, CD_W), cd_w_out.reshape(D_MODEL, D_MODEL)))
    w1 = w1.reshape(DEPTH, D_MODEL, FFN_DIM)
    w2 = w2.reshape(DEPTH, FFN_DIM, D_MODEL)
    o_c = _attn_ab((qa_c, qb_c), None, c_kv, b_lambda[0], subln, CTX_LEN, "attn_ab_ctx")

    x2 = _post([o_x], [w_out_ab], x, mod0, None, ln_g[0], ln_b[0], w1, w2, 0, tm, "post0_x")
    c2 = _post([o_c.reshape(1, nctx, D_MODEL)], [w_out_ab], cflat, mod0, ctx_row, ln_g[0], ln_b[0], w1, w2, 0,
               tmc, "post0_ctx")

    mod1 = mods[1]
    w_in = cd_w_in.reshape(D_MODEL, CD_W)
    ct, sn, cc, sc = _dft_tables()
    w_fab = _dft_weights(w_in, jnp.asarray(cc), jnp.asarray(sc))
    q_n, f_a, f_b, k_n, v_n = _inproj_cd(x2, mod1, None, w_cd, w_fab, 2 * tm, "inproj_cd_x")
    kc_n, vc_n = _inproj_cd(c2, mod1, ctx_row, w_cd, None, tmc, "inproj_cd_ctx")
    kc_n = kc_n.reshape(b, CTX_LEN, 512)
    vc_n = vc_n.reshape(b, CTX_LEN, 1024)

    o_d = _dft_time(f_a, f_b, jnp.asarray(ct).astype(BF16), jnp.asarray(sn).astype(BF16))
    cb = _bias_table(c_rpb[0])
    o_n = _natten(q_n, k_n, v_n, kc_n, vc_n, cb)

    return _post([o_n, o_d], [w_out_cd[:512], w_out_cd[512:]], x2, mod1, None, ln_g[1], ln_b[1], w1, w2, 1, tm, "post1_x")
```

```python
import functools
import math

import numpy as np
import jax
import jax.numpy as jnp
from jax import lax
from jax.experimental import pallas as pl
from jax.experimental.pallas import tpu as pltpu

F32 = jnp.float32
BF16 = jnp.bfloat16

D_MODEL = 1024
SEQ = 2048
CTX_LEN = 256
DEPTH = 2
GRID_W = 64
GRID_H = SEQ // GRID_W
HEAD_DIM = 64
A_HEADS = 8
A_KV_HEADS = 2
B_HEADS = 4
C_HEADS = 8
WIN_H = 8
WIN_W = 16
D_GROUPS = 4
D_GROUP_DIM = 128
FFN_DIM = 4 * D_MODEL
ROPE_THETA = 10000.0
ALPHA = (2.0 * DEPTH) ** 0.25
EPS = 1e-6
ATTN_SCALE = HEAD_DIM ** -0.5
LOG2E = math.log2(math.e)
Q_SCALE = ATTN_SCALE * LOG2E
LAM_INIT0 = 0.8 - 0.6 * math.exp(-0.3 * 0)
NEG = -1e30

LANES = 128
MOD_ROWS = 16
VMEM_LIMIT = 56 * 1024 * 1024

QA_PERM = (0, 4, 1, 5, 2, 6, 3, 7)

NAT_ROWS = 4
NAT_Q = NAT_ROWS * GRID_W
NAT_WIN_ROWS = NAT_ROWS + WIN_H
NAT_WIN = NAT_WIN_ROWS * GRID_W
NAT_TILES = NAT_WIN_ROWS // 2
_NAT_DR0 = [min(max(r0 - WIN_H // 2, 0), GRID_H - NAT_WIN_ROWS) + 2 * j - (r0 + qr) + WIN_H - 1
            for r0 in range(0, GRID_H, NAT_ROWS) for qr in range(NAT_ROWS) for j in range(NAT_TILES)]
CB_OFFSET = -min(_NAT_DR0)
CB_ENTRIES = max(_NAT_DR0) + CB_OFFSET + 1


def _cparams(sem, flags=None):
    return pltpu.CompilerParams(dimension_semantics=sem, vmem_limit_bytes=VMEM_LIMIT, flags=flags)


def _dot(a, b):
    return jnp.dot(a, b, preferred_element_type=F32)


def _dot_nt(a, b):
    return lax.dot_general(a, b, (((1,), (1,)), ((), ())), preferred_element_type=F32)


def _split3(a):
    hi = a.astype(BF16)
    r1 = a - hi.astype(F32)
    mid = r1.astype(BF16)
    lo = (r1 - mid.astype(F32)).astype(BF16)
    return hi, mid, lo


def _layer_norm(z, g, b):
    mu = jnp.mean(z, axis=-1, keepdims=True)
    zc = z - mu
    var = jnp.mean(zc * zc, axis=-1, keepdims=True)
    return zc * lax.rsqrt(var + EPS) * g + b


@functools.lru_cache(maxsize=None)
def _rope_tables():
    pos = np.arange(SEQ)
    row = (pos // GRID_W).astype(np.float64)
    col = (pos % GRID_W).astype(np.float64)
    half = HEAD_DIM // 2
    freqs = np.power(ROPE_THETA, -np.arange(0, half, 2, dtype=np.float64) / half)
    def axis_angles(p):
        a = p[:, None] * freqs[None, :]
        return np.concatenate([a, a], axis=-1)
    ang = np.concatenate([axis_angles(row), axis_angles(col)], axis=-1)
    cos, sin = np.cos(ang), np.sin(ang)
    first = (np.arange(HEAD_DIM) % half) < (half // 2)
    sin_a = np.where(first[None, :], -sin, 0.0)
    sin_b = np.where(first[None, :], 0.0, sin)
    tile = lambda t: np.tile(t, (1, LANES // HEAD_DIM)).astype(np.float32)
    return tile(cos), tile(sin_a), tile(sin_b)


@functools.lru_cache(maxsize=None)
def _group_mean_matrix(width):
    g = np.arange(width) // HEAD_DIM
    return (g[:, None] == g[None, :]).astype(np.float32) / HEAD_DIM


@functools.lru_cache(maxsize=None)
def _dft_tables():
    t = np.arange(SEQ, dtype=np.int64)
    k = (t[:, None] * t[None, :]) % SEQ
    ang = 2.0 * np.pi * k.astype(np.float64) / SEQ
    ct = np.cos(ang).astype(np.float32)
    sn = (-np.sin(ang)).astype(np.float32)
    c = np.arange(D_GROUP_DIM, dtype=np.int64)
    kc = (c[:, None] * c[None, :]) % D_GROUP_DIM
    angc = 2.0 * np.pi * kc.astype(np.float64) / D_GROUP_DIM
    norm = 1.0 / math.sqrt(SEQ * D_GROUP_DIM)
    return ct, sn, (np.cos(angc) * norm).astype(np.float32), (np.sin(angc) * norm).astype(np.float32)


def _mod_kernel(c_ref, w_ref, b_ref, o_ref):
    c = c_ref[...]
    a = c / (1.0 + jnp.exp(-c))
    a_hi, a_mid, _ = _split3(a)
    y = _dot(jnp.concatenate([a_hi, a_mid], axis=0), w_ref[0].astype(BF16))
    o_ref[0, 0] = y[:MOD_ROWS] + y[MOD_ROWS:] + b_ref[0]


def _modulation(cs, mod_w, mod_b):
    n = 6 * D_MODEL
    return pl.pallas_call(
        _mod_kernel,
        grid=(DEPTH, 6),
        in_specs=[pl.BlockSpec((MOD_ROWS, D_MODEL), lambda l, j: (0, 0)),
                  pl.BlockSpec((1, D_MODEL, D_MODEL), lambda l, j: (l, 0, j)),
                  pl.BlockSpec((1, 1, D_MODEL), lambda l, j: (l, 0, j))],
        out_specs=pl.BlockSpec((1, 1, MOD_ROWS, D_MODEL), lambda l, j: (l, j, 0, 0)),
        out_shape=jax.ShapeDtypeStruct((DEPTH, 6, MOD_ROWS, D_MODEL), F32),
        compiler_params=_cparams(("arbitrary", "arbitrary")),
        name="modulation",
    )(cs, mod_w, mod_b.reshape(DEPTH, 1, n))


def _tok_spec(tm, width):
    return pl.BlockSpec((1, tm, width), lambda g, i: (g, i, 0))


MOD_SPEC = pl.BlockSpec((6, MOD_ROWS, D_MODEL), lambda g, i: (0, 0, 0))


def _mod_vectors(mod_ref, mod_row):
    r = pl.program_id(0) if mod_row is None else mod_row
    return [mod_ref[k, pl.ds(r, 1), :] for k in range(6)]


def _const_spec(shape):
    nd = len(shape)
    return pl.BlockSpec(shape, lambda g, i: (0,) * nd)


AB_QA, AB_QB, AB_KA, AB_VA, AB_KB, AB_VB, AB_W = 0, 512, 1024, 1152, 1280, 1792, 2304


TOKEN_CHAIN_ROWS = 256


def _store_v_ext(o_ref, rows, v):
    ones = jnp.ones((v.shape[0], LANES), o_ref.dtype)
    for j in range(v.shape[1] // LANES):
        o_ref[0, rows, 2 * j * LANES:(2 * j + 1) * LANES] = v[:, j * LANES:(j + 1) * LANES].astype(o_ref.dtype)
        o_ref[0, rows, (2 * j + 1) * LANES:(2 * j + 2) * LANES] = ones


def _rope_chunk(c, cos, sin_a, sin_b):
    return c * cos + pltpu.roll(c, LANES - 16, axis=1) * sin_a + pltpu.roll(c, 16, axis=1) * sin_b


def _inproj_ab_kernel(*refs, rope, mod_row):
    if rope:
        (x_ref, mod_ref, w_ref, gq_ref, gk_ref, g512_ref, g128_ref, cos_ref, sa_ref, sb_ref,
         qa_ref, qb_ref, ka_ref, kb_ref, va_ref, vb_ref) = refs
    else:
        (x_ref, mod_ref, w_ref, gq_ref, gk_ref, g512_ref, g128_ref,
         qa_ref, qb_ref, ka_ref, kb_ref, va_ref, vb_ref) = refs

    def rms(v, g_ref, gm_ref):
        ms = _dot((v * v).astype(BF16), gm_ref[...])
        return v * lax.rsqrt(ms + EPS) * g_ref[...]

    shift, scale = _mod_vectors(mod_ref, mod_row)[0:2]
    lo = lax.broadcasted_iota(jnp.int32, (1, LANES), 1) < HEAD_DIM
    tm = x_ref.shape[1]
    tc = min(TOKEN_CHAIN_ROWS, tm)

    def project(rb):
        rows = slice(rb * tc, (rb + 1) * tc)
        u = (x_ref[0, rows, :] * (1.0 + scale) + shift).astype(BF16)
        return _dot(u, w_ref[...])

    def epilogue(rb, h):
        rows = slice(rb * tc, (rb + 1) * tc)
        if rope:
            cos, sin_a, sin_b = cos_ref[rows, :], sa_ref[rows, :], sb_ref[rows, :]

        def chunks(v, q_scale):
            out = []
            for j in range(v.shape[1] // LANES):
                c = v[:, j * LANES:(j + 1) * LANES]
                if rope:
                    c = _rope_chunk(c, cos, sin_a, sin_b)
                out.append(c * q_scale if q_scale != 1.0 else c)
            return out

        def store(o_ref, cs):
            for j, c in enumerate(cs):
                o_ref[0, rows, j * LANES:(j + 1) * LANES] = c.astype(o_ref.dtype)

        qa = chunks(rms(h[:, AB_QA:AB_QB], gq_ref, g512_ref), Q_SCALE)
        qa_perm = []
        for p in range(A_HEADS // 2):
            a, b = qa[p // 2], qa[A_HEADS // 4 + p // 2]
            if p % 2 == 0:
                qa_perm.append(jnp.where(lo, a, pltpu.roll(b, HEAD_DIM, axis=1)))
            else:
                qa_perm.append(jnp.where(lo, pltpu.roll(a, HEAD_DIM, axis=1), b))
        store(qa_ref, qa_perm)
        store(qb_ref, chunks(h[:, AB_QB:AB_KA], Q_SCALE))
        store(ka_ref, chunks(rms(h[:, AB_KA:AB_VA], gk_ref, g128_ref), 1.0))
        store(kb_ref, chunks(h[:, AB_KB:AB_VB], 1.0))
        _store_v_ext(va_ref, rows, h[:, AB_VA:AB_KB])
        _store_v_ext(vb_ref, rows, h[:, AB_VB:AB_W])

    for rb in range(tm // tc):
        epilogue(rb, project(rb))


def _inproj_ab(tok, mod, mod_row, w, gq, gk, rope_tabs, tm, name):
    g, r, _ = tok.shape
    rope = rope_tabs is not None
    consts = [w, gq, gk, jnp.asarray(_group_mean_matrix(512), BF16), jnp.asarray(_group_mean_matrix(128), BF16)]
    in_specs = [_tok_spec(tm, D_MODEL), MOD_SPEC] + [_const_spec(c.shape) for c in consts]
    args = [tok, mod] + consts
    if rope:
        in_specs += [pl.BlockSpec((tm, LANES), lambda gg, i: (i, 0))] * 3
        args += list(rope_tabs)
    widths = (512, 512, 128, 512, 2 * 128, 2 * 512)
    return pl.pallas_call(
        functools.partial(_inproj_ab_kernel, rope=rope, mod_row=mod_row),
        grid=(g, r // tm),
        in_specs=in_specs,
        out_specs=[_tok_spec(tm, wd) for wd in widths],
        out_shape=[jax.ShapeDtypeStruct((g, r, wd), BF16) for wd in widths],
        compiler_params=_cparams(("parallel", "parallel")),
        name=name,
    )(*args)


def _scores(q2, kv_pairs, bias=None):
    scores = [_dot_nt(q2, k) for k, _ in kv_pairs]
    if bias is not None:
        scores[0] = scores[0] + bias
    return scores


def _softmax_pv(q2, kv_pairs, bias=None, scores=None):
    if scores is None:
        scores = _scores(q2, kv_pairs, bias)
    m = scores[0].max(axis=-1, keepdims=True)
    for s in scores[1:]:
        m = jnp.maximum(m, s.max(axis=-1, keepdims=True))
    acc = None
    for s, (_, v) in zip(scores, kv_pairs):
        a = _dot(jnp.exp2(s - m).astype(BF16), v)
        acc = a if acc is None else acc + a
    return acc[:, :LANES] / acc[:, LANES:]


def _run_chains(chains, lookahead):
    pending = []
    for start, finish in chains:
        pending.append((finish, start()))
        if len(pending) > lookahead:
            fin, scores = pending.pop(0)
            fin(scores)
    for fin, scores in pending:
        fin(scores)


ATTN_LOOKAHEAD = 3


def _split_pair(qp, lo):
    zero = jnp.zeros_like(qp)
    return jnp.concatenate([jnp.where(lo, qp, zero), jnp.where(lo, zero, qp)], axis=0)


ATTN_CHAIN_ROWS = 256


def _chain_schedule(rows, n_pairs, chain_rows):
    blocks = [(r0, min(chain_rows, rows - r0)) for r0 in range(0, rows, chain_rows)]
    sched = [(p, r0, n) for r0, n in blocks for p in range(n_pairs)]
    if len(blocks) < 2:
        return sched
    (p, r0, n), rest = sched[0], sched[1:]
    sched = [(p, r0, n // 2), (p, r0 + n // 2, n - n // 2)] + rest
    (p, r0, n), rest = sched[-1], sched[:-1]
    return rest + [(p, r0, n - n // 2), (p, r0 + n - n // 2, n // 2)]


def _attn_ab_kernel(*refs, with_x, n_casts):
    if n_casts:
        cast_in = refs[len(refs) - 2 * n_casts - 1:len(refs) - n_casts - 1]
        cast_out = refs[len(refs) - n_casts:]
        refs = refs[:len(refs) - 2 * n_casts - 1] + (refs[len(refs) - n_casts - 1],)
        for src, dst in zip(cast_in, cast_out):
            dst[...] = src[...].astype(dst.dtype)
    if with_x:
        (qa_ref, qb_ref, kax_ref, vax_ref, kbx_ref, vbx_ref, kac_ref, vac_ref, kbc_ref, vbc_ref,
         lam_ref, subln_ref, o_ref) = refs
    else:
        (qa_ref, qb_ref, kac_ref, vac_ref, kbc_ref, vbc_ref, lam_ref, subln_ref, o_ref) = refs
    lo = lax.broadcasted_iota(jnp.int32, (1, LANES), 1) < HEAD_DIM

    lp = lam_ref[...]
    lam = (jnp.exp(jnp.sum(lp[0:1] * lp[1:2], axis=-1, keepdims=True))
           - jnp.exp(jnp.sum(lp[2:3] * lp[3:4], axis=-1, keepdims=True)) + LAM_INIT0)

    def gqa_chain(p, r0, n):
        rows = slice(r0, r0 + n)
        sl = slice(p * LANES, (p + 1) * LANES)
        kv = [(kac_ref[0], vac_ref[0])]
        if with_x:
            kv.append((kax_ref[0], vax_ref[0]))

        def finish(scores):
            o2 = _softmax_pv(None, kv, scores=scores)
            o_ref[0, rows, sl] = jnp.where(lo, o2[:n], o2[n:]).astype(o_ref.dtype)

        return (lambda: _scores(_split_pair(qa_ref[0, rows, sl], lo), kv)), finish

    def diff_chain(i, r0, n):
        rows = slice(r0, r0 + n)
        sl = slice(i * LANES, (i + 1) * LANES)
        sl2 = slice(2 * i * LANES, (2 * i + 2) * LANES)
        kv = [(kbc_ref[0, :, sl], vbc_ref[0, :, sl2])]
        if with_x:
            kv.append((kbx_ref[0, :, sl], vbx_ref[0, :, sl2]))

        def finish(scores):
            o2 = _softmax_pv(None, kv, scores=scores)
            o = o2[:n] - lam * o2[n:]
            ms = jnp.mean(o * o, axis=-1, keepdims=True)
            o = o * lax.rsqrt(ms + EPS) * subln_ref[...] * (1.0 - LAM_INIT0)
            o_ref[0, rows, 512 + i * LANES:512 + (i + 1) * LANES] = o.astype(o_ref.dtype)

        return (lambda: _scores(_split_pair(qb_ref[0, rows, sl], lo), kv)), finish

    n_gqa = A_HEADS // 2
    _run_chains([gqa_chain(pair, r0, n) if pair < n_gqa else diff_chain(pair - n_gqa, r0, n)
                 for pair, r0, n in _chain_schedule(qa_ref.shape[1], n_gqa + B_HEADS, ATTN_CHAIN_ROWS)],
                ATTN_LOOKAHEAD)


def _attn_ab(q_parts, x_kv, c_kv, lam_p, subln, tq, name, casts=()):
    qa, qb = q_parts
    b, nq, _ = qa.shape
    with_x = x_kv is not None
    steps = nq // tq

    def full(a):
        return pl.BlockSpec((1,) + a.shape[1:], lambda bb, i: (bb, 0, 0))

    args = [qa, qb]
    in_specs = [pl.BlockSpec((1, tq, 512), lambda bb, i: (bb, i, 0))] * 2
    if with_x:
        args += list(x_kv)
        in_specs += [full(a) for a in x_kv]
    args += list(c_kv) + [lam_p, subln]
    in_specs += [full(a) for a in c_kv] + [pl.BlockSpec(lam_p.shape, lambda bb, i: (0, 0)),
                                           pl.BlockSpec(subln.shape, lambda bb, i: (0, 0))]
    cast_specs = [pl.BlockSpec((a.shape[0] // (b * steps), a.shape[1]), lambda bb, i: (bb * steps + i, 0))
                  for a in casts]
    outs = pl.pallas_call(
        functools.partial(_attn_ab_kernel, with_x=with_x, n_casts=len(casts)),
        grid=(b, steps),
        in_specs=in_specs + cast_specs,
        out_specs=[pl.BlockSpec((1, tq, D_MODEL), lambda bb, i: (bb, i, 0))] + cast_specs,
        out_shape=[jax.ShapeDtypeStruct((b, nq, D_MODEL), BF16)]
                  + [jax.ShapeDtypeStruct(a.shape, BF16) for a in casts],
        compiler_params=_cparams(("parallel", "parallel")),
        name=name,
    )(*args, *casts)
    return outs[0] if not casts else outs


FFN_CHUNK = 1024
POST_EDGE_PIECES = 2


def _post_kernel(*refs, n_parts, mod_row):
    o_refs = refs[:n_parts]
    wo_refs = refs[n_parts:2 * n_parts]
    x_ref, mod_ref, g_ref, b_ref, w1_ref, w2_ref, y_ref = refs[2 * n_parts:]
    _, _, gate1, shift2, scale2, gate2 = _mod_vectors(mod_ref, mod_row)
    tm = x_ref.shape[1]
    n_chunks = FFN_DIM // FFN_CHUNK
    pieces = [slice(r, r + tm // POST_EDGE_PIECES) for r in range(0, tm, tm // POST_EDGE_PIECES)]
    sq_relu = lambda t: jnp.square(jnp.maximum(t, 0.0)).astype(BF16)

    ys = []
    for rows in pieces:
        y = _dot(o_refs[0][0, rows, :], wo_refs[0][...])
        for o_ref, w_ref in zip(o_refs[1:], wo_refs[1:]):
            y = y + _dot(o_ref[0, rows, :], w_ref[...])
        ys.append(y)
    x1s, us, h0s = [], [], []
    for rows, y in zip(pieces, ys):
        x1s.append(_layer_norm(ALPHA * x_ref[0, rows, :] + gate1 * y, g_ref[0:1, :], b_ref[0:1, :]))
        us.append((x1s[-1] * (1.0 + scale2) + shift2).astype(BF16))
        h0s.append(sq_relu(_dot(us[-1], w1_ref[0, :, 0:FFN_CHUNK])))
    u = jnp.concatenate(us, axis=0)
    acc = _dot(jnp.concatenate(h0s, axis=0), w2_ref[0, 0:FFN_CHUNK, :])
    for c in range(1, n_chunks - 1):
        sl = slice(c * FFN_CHUNK, (c + 1) * FFN_CHUNK)
        acc = acc + _dot(sq_relu(_dot(u, w1_ref[0, :, sl])), w2_ref[0, sl, :])
    sl = slice((n_chunks - 1) * FFN_CHUNK, n_chunks * FFN_CHUNK)
    h_last = sq_relu(_dot(u, w1_ref[0, :, sl]))
    fs = [acc[rows] + _dot(h_last[rows], w2_ref[0, sl, :]) for rows in pieces]
    for rows, x1, f in zip(pieces, x1s, fs):
        y_ref[0, rows, :] = _layer_norm(ALPHA * x1 + gate2 * f, g_ref[1:2, :], b_ref[1:2, :])


def _post(o_parts, wo_parts, resid, mod, mod_row, ln_g, ln_b, w1, w2, layer, tm, name):
    g, r, _ = resid.shape
    n = len(o_parts)
    single = pl.Buffered(1)
    layer_w = lambda a: pl.BlockSpec((1,) + a.shape[1:], lambda gg, i: (layer, 0, 0), pipeline_mode=single)
    const1 = lambda a: pl.BlockSpec(a.shape, lambda gg, i: (0,) * a.ndim, pipeline_mode=single)
    in_specs = ([_tok_spec(tm, o.shape[2]) for o in o_parts] + [const1(w) for w in wo_parts]
                + [_tok_spec(tm, D_MODEL), MOD_SPEC, _const_spec(ln_g.shape), _const_spec(ln_b.shape),
                   layer_w(w1), layer_w(w2)])
    return pl.pallas_call(
        functools.partial(_post_kernel, n_parts=n, mod_row=mod_row),
        grid=(g, r // tm),
        in_specs=in_specs,
        out_specs=_tok_spec(tm, D_MODEL),
        out_shape=jax.ShapeDtypeStruct((g, r, D_MODEL), F32),
        compiler_params=_cparams(("parallel", "parallel")),
        name=name,
    )(*o_parts, *wo_parts, resid, mod, ln_g, ln_b, w1, w2)


CD_Q, CD_F, CD_K, CD_V, CD_W = 0, 512, 1024, 1536, 2048


def _inproj_cd_kernel(*refs, with_q, mod_row):
    if with_q:
        x_ref, mod_ref, w_ref, wa_ref, wb_ref, q_ref, fa_ref, fb_ref, k_ref, v_ref = refs
    else:
        x_ref, mod_ref, w_ref, k_ref, v_ref = refs
    shift, scale = _mod_vectors(mod_ref, mod_row)[0:2]
    u = (x_ref[0] * (1.0 + scale) + shift).astype(BF16)
    if with_q:
        q_ref[0] = (_dot(u, w_ref[:, CD_Q:CD_F]) * Q_SCALE).astype(q_ref.dtype)
        fa_ref[0] = _dot(u, wa_ref[...]).astype(fa_ref.dtype)
        fb_ref[0] = _dot(u, wb_ref[...]).astype(fb_ref.dtype)
    kv = _dot(u, w_ref[:, CD_K:CD_W])
    k_ref[0] = kv[:, :CD_V - CD_K].astype(k_ref.dtype)
    _store_v_ext(v_ref, slice(None), kv[:, CD_V - CD_K:])


def _inproj_cd(tok, mod, mod_row, w, w_ab, tm, name):
    g, r, _ = tok.shape
    with_q = w_ab is not None
    weights = [w] + (list(w_ab) if with_q else [])
    widths = ((512, 512, 512) if with_q else ()) + (512, 1024)
    return pl.pallas_call(
        functools.partial(_inproj_cd_kernel, with_q=with_q, mod_row=mod_row),
        grid=(g, r // tm),
        in_specs=[_tok_spec(tm, D_MODEL), MOD_SPEC] + [_const_spec(a.shape) for a in weights],
        out_specs=[_tok_spec(tm, wd) for wd in widths],
        out_shape=[jax.ShapeDtypeStruct((g, r, wd), BF16) for wd in widths],
        compiler_params=_cparams(("parallel", "parallel")),
        name=name,
    )(tok, mod, *weights)


def _dft_weight_kernel(w_ref, cc_ref, sc_ref, wa_ref, wb_ref):
    c_hi, c_mid, _ = _split3(cc_ref[...])
    s_hi, s_mid, _ = _split3(sc_ref[...])
    for gi in range(D_GROUPS):
        sl = slice(gi * D_GROUP_DIM, (gi + 1) * D_GROUP_DIM)
        w_hi, w_mid, _ = _split3(w_ref[:, sl])
        wa_ref[:, sl] = (_dot(w_hi, c_hi) + (_dot(w_mid, c_hi) + _dot(w_hi, c_mid))).astype(BF16)
        wb_ref[:, sl] = (_dot(w_hi, s_hi) + (_dot(w_mid, s_hi) + _dot(w_hi, s_mid))).astype(BF16)


def _dft_weights(w_in, cc, sc):
    width = CD_K - CD_F
    out = pl.BlockSpec((D_MODEL, width), lambda i: (0, 0))
    return pl.pallas_call(
        _dft_weight_kernel,
        grid=(1,),
        in_specs=[pl.BlockSpec((D_MODEL, width), lambda i: (0, CD_F // width)),
                  pl.BlockSpec(cc.shape, lambda i: (0, 0)), pl.BlockSpec(sc.shape, lambda i: (0, 0))],
        out_specs=[out, out],
        out_shape=[jax.ShapeDtypeStruct((D_MODEL, width), BF16)] * 2,
        compiler_params=_cparams(("arbitrary",)),
        name="dft_weights",
    )(w_in, cc, sc)


DFT_ROWS = 512


def _dft_time_kernel(a_ref, b_ref, ct_ref, sn_ref, o_ref):
    a = a_ref[0]
    bm = b_ref[0]
    for r in range(SEQ // DFT_ROWS):
        sl = slice(r * DFT_ROWS, (r + 1) * DFT_ROWS)
        y = _dot(ct_ref[sl, :], a) + _dot(sn_ref[sl, :], bm)
        o_ref[0, sl, :] = y.astype(o_ref.dtype)


def _dft_time(a, bm, ct, sn):
    b = a.shape[0]
    single = pl.Buffered(1)
    tok = pl.BlockSpec((1, SEQ, 512), lambda bb: (bb, 0, 0))
    tab = pl.BlockSpec((SEQ, SEQ), lambda bb: (0, 0), pipeline_mode=single)
    return pl.pallas_call(
        _dft_time_kernel,
        grid=(b,),
        in_specs=[tok, tok, tab, tab],
        out_specs=tok,
        out_shape=jax.ShapeDtypeStruct((b, SEQ, 512), BF16),
        compiler_params=_cparams(("parallel",)),
        name="dft_time",
    )(a, bm, ct, sn)


def _bias_table_kernel(r_ref, o_ref):
    n = GRID_W * LANES
    row = lax.broadcasted_iota(jnp.int32, (2 * 32, n), 0)
    lane = lax.broadcasted_iota(jnp.int32, (2 * 32, n), 1)
    e_row, i_row = row >> 5, row & 31
    c = lane >> 7
    e_lane = (lane >> 6) & 1
    kc = lane & (GRID_W - 1)
    sel = jnp.where((e_row == e_lane) & (kc - c + (WIN_W - 1) == i_row), 1.0, 0.0).astype(BF16)
    hi, mid, lo = _split3(r_ref[...])
    t = _dot(hi, sel) + _dot(mid, sel) + _dot(lo, sel)
    c1 = c[0:1]
    kc1 = kc[0:1]
    cstart = jnp.clip(c1 - WIN_W // 2, 0, GRID_W - WIN_W)
    col_ok = (kc1 >= cstart) & (kc1 < cstart + WIN_W)
    t = jnp.where(col_ok, t * LOG2E, NEG)
    for qc in range(GRID_W):
        o_ref[:, qc, :] = t[:, qc * LANES:(qc + 1) * LANES]


def _bias_table(rpb):
    n_dr = 2 * WIN_H - 1
    p = jnp.pad(rpb, ((0, 0), (CB_OFFSET, CB_ENTRIES + 1 - n_dr - CB_OFFSET), (0, 1)))
    r2 = jnp.stack([p[:, 0:CB_ENTRIES], p[:, 1:CB_ENTRIES + 1]], axis=2)
    r2 = r2.reshape(C_HEADS * CB_ENTRIES, 2 * 32)
    t = pl.pallas_call(
        _bias_table_kernel,
        out_shape=jax.ShapeDtypeStruct((C_HEADS * CB_ENTRIES, GRID_W, LANES), F32),
        compiler_params=pltpu.CompilerParams(vmem_limit_bytes=VMEM_LIMIT),
        name="natten_bias_table",
    )(r2)
    return t.reshape(C_HEADS, CB_ENTRIES, GRID_W, LANES)


NAT_BLOCKS = 4
NAT_LOOKAHEAD = 2


def _natten_block(blk, q_ref, k_ref, v_ref, kc_ref, vc_ref, cb_ref, o_ref, lo, e):
    rows = slice(blk * NAT_Q, (blk + 1) * NAT_Q)
    r0 = NAT_ROWS * (NAT_BLOCKS * pl.program_id(1) + blk)
    ws = jnp.clip(r0 - WIN_H // 2, 0, GRID_H - NAT_WIN_ROWS)
    koff = pl.multiple_of(ws * GRID_W, LANES)

    entry = []
    rmask = []
    for qr in range(NAT_ROWS):
        r = r0 + qr
        lo_r = jnp.clip(r - WIN_H // 2, 0, GRID_H - WIN_H) - r + (WIN_H - 1)
        ent_q, mask_q = [], []
        for j in range(NAT_TILES):
            dr0 = ws + 2 * j - r + (WIN_H - 1)
            ent_q.append(dr0 + CB_OFFSET)
            dr = dr0 + e
            ok = (dr >= lo_r) & (dr < lo_r + WIN_H)
            mask_q.append(jnp.where(ok, 0.0, NEG))
        entry.append(ent_q)
        rmask.append(mask_q)

    states = []
    for p in range(C_HEADS // 2):
        sl = slice(p * LANES, (p + 1) * LANES)
        sl2 = slice(2 * p * LANES, (2 * p + 2) * LANES)
        q2 = _split_pair(q_ref[0, rows, sl], lo)
        bias = jnp.concatenate(
            [jnp.concatenate([cb_ref[2 * p + hh, entry[qr][j]] + rmask[qr][j] for j in range(NAT_TILES)], axis=1)
             for hh in range(2) for qr in range(NAT_ROWS)], axis=0)
        kv = [(k_ref[0, pl.ds(koff, NAT_WIN), sl], v_ref[0, pl.ds(koff, NAT_WIN), sl2]),
              (kc_ref[0, :, sl], vc_ref[0, :, sl2])]
        states.append((sl, kv, _scores(q2, kv, bias)))

    def finish():
        for sl, kv, scores in states:
            o2 = _softmax_pv(None, kv, scores=scores)
            o_ref[0, rows, sl] = jnp.where(lo, o2[:NAT_Q], o2[NAT_Q:]).astype(o_ref.dtype)

    return finish


def _natten_kernel(q_ref, k_ref, v_ref, kc_ref, vc_ref, cb_ref, o_ref):
    lane = lax.broadcasted_iota(jnp.int32, (1, LANES), 1)
    lo = lane < HEAD_DIM
    e = jnp.where(lo, 0, 1)
    pending = []
    for blk in range(NAT_BLOCKS):
        pending.append(_natten_block(blk, q_ref, k_ref, v_ref, kc_ref, vc_ref, cb_ref, o_ref, lo, e))
        if len(pending) > NAT_LOOKAHEAD:
            pending.pop(0)()
    for finish in pending:
        finish()


def _natten(q, k, v, kc, vc, cb):
    b = q.shape[0]
    tq = NAT_BLOCKS * NAT_Q
    qspec = pl.BlockSpec((1, tq, 512), lambda bb, i: (bb, i, 0))
    full = lambda a: pl.BlockSpec((1,) + a.shape[1:], lambda bb, i: (bb, 0, 0))
    return pl.pallas_call(
        _natten_kernel,
        grid=(b, SEQ // tq),
        in_specs=[qspec, full(k), full(v), full(kc), full(vc),
                  pl.BlockSpec(cb.shape, lambda bb, i: (0, 0, 0, 0))],
        out_specs=qspec,
        out_shape=jax.ShapeDtypeStruct((b, SEQ, 512), BF16),
        compiler_params=_cparams(("parallel", "parallel")),
        name="natten",
    )(q, k, v, kc, vc, cb)


def kernel(x, c, ctx, c_ctx, mod_w, mod_b, ln_g, ln_b, ffn_w1, ffn_w2, ab_w_in, ab_w_out, a_q_norm, a_k_norm,
           b_lambda, b_subln, cd_w_in, cd_w_out, c_rpb):
    b = x.shape[0]
    nctx = b * CTX_LEN
    tm = 512
    tmc = min(tm, nctx)

    cs = jnp.concatenate([c, c_ctx[None, :], jnp.zeros((MOD_ROWS - b - 1, D_MODEL), F32)], axis=0)
    mods = _modulation(cs, mod_w, mod_b)
    ctx_row = b

    cflat = ctx.reshape(1, nctx, D_MODEL)

    perm = np.asarray(QA_PERM)
    w_ab = ab_w_in.reshape(D_MODEL, AB_W).astype(BF16)
    w_out = ab_w_out.reshape(D_MODEL, D_MODEL)
    w_out_ab = jnp.concatenate([w_out[:512].reshape(A_HEADS, HEAD_DIM, D_MODEL)[perm].reshape(512, D_MODEL),
                                w_out[512:]], axis=0).astype(BF16)
    gq = jnp.tile(a_q_norm[0], A_HEADS)[None, :]
    gk = jnp.tile(a_k_norm[0], A_KV_HEADS)[None, :]
    rope_tabs = tuple(jnp.asarray(t) for t in _rope_tables())
    subln = b_subln[0][None, :]
    mod0 = mods[0]

    xq = _inproj_ab(x, mod0, None, w_ab, gq, gk, rope_tabs, 2 * tm, "inproj_ab_x")
    cq = _inproj_ab(cflat, mod0, ctx_row, w_ab, gq, gk, None, tmc, "inproj_ab_ctx")
    cq = [a.reshape(b, CTX_LEN, a.shape[2]) for a in cq]
    qa_x, qb_x, ka_x, kb_x, va_x, vb_x = xq
    qa_c, qb_c, ka_c, kb_c, va_c, vb_c = cq
    c_kv = (ka_c, va_c, kb_c, vb_c)
    o_x, w1, w2, w_cd, w_out_cd = _attn_ab(
        (qa_x, qb_x), (ka_x, va_x, kb_x, vb_x), c_kv, b_lambda[0], subln, 512, "attn_ab_x",
        casts=(ffn_w1.reshape(DEPTH * D_MODEL, FFN_DIM), ffn_w2.reshape(DEPTH * FFN_DIM, D_MODEL),
               cd_w_in.reshape(D_MODEL, CD_W), cd_w_out.reshape(D_MODEL, D_MODEL)))
    w1 = w1.reshape(DEPTH, D_MODEL, FFN_DIM)
    w2 = w2.reshape(DEPTH, FFN_DIM, D_MODEL)
    o_c = _attn_ab((qa_c, qb_c), None, c_kv, b_lambda[0], subln, CTX_LEN, "attn_ab_ctx")

    x2 = _post([o_x], [w_out_ab], x, mod0, None, ln_g[0], ln_b[0], w1, w2, 0, tm, "post0_x")
    c2 = _post([o_c.reshape(1, nctx, D_MODEL)], [w_out_ab], cflat, mod0, ctx_row, ln_g[0], ln_b[0], w1, w2, 0,
               tmc, "post0_ctx")

    mod1 = mods[1]
    w_in = cd_w_in.reshape(D_MODEL, CD_W)
    ct, sn, cc, sc = _dft_tables()
    w_fab = _dft_weights(w_in, jnp.asarray(cc), jnp.asarray(sc))
    q_n, f_a, f_b, k_n, v_n = _inproj_cd(x2, mod1, None, w_cd, w_fab, 2 * tm, "inproj_cd_x")
    kc_n, vc_n = _inproj_cd(c2, mod1, ctx_row, w_cd, None, tmc, "inproj_cd_ctx")
    kc_n = kc_n.reshape(b, CTX_LEN, 512)
    vc_n = vc_n.reshape(b, CTX_LEN, 1024)

    o_d = _dft_time(f_a, f_b, jnp.asarray(ct).astype(BF16), jnp.asarray(sn).astype(BF16))
    cb = _bias_table(c_rpb[0])
    o_n = _natten(q_n, k_n, v_n, kc_n, vc_n, cb)

    return _post([o_n, o_d], [w_out_cd[:512], w_out_cd[512:]], x2, mod1, None, ln_g[1], ln_b[1], w1, w2, 1, tm, "post1_x")
```

```python
import functools
import math

import numpy as np
import jax
import jax.numpy as jnp
from jax import lax
from jax.experimental import pallas as pl
from jax.experimental.pallas import tpu as pltpu

F32 = jnp.float32
BF16 = jnp.bfloat16

D_MODEL = 1024
SEQ = 2048
CTX_LEN = 256
DEPTH = 2
GRID_W = 64
GRID_H = SEQ // GRID_W
HEAD_DIM = 64
A_HEADS = 8
A_KV_HEADS = 2
B_HEADS = 4
C_HEADS = 8
WIN_H = 8
WIN_W = 16
D_GROUPS = 4
D_GROUP_DIM = 128
FFN_DIM = 4 * D_MODEL
ROPE_THETA = 10000.0
ALPHA = (2.0 * DEPTH) ** 0.25
EPS = 1e-6
ATTN_SCALE = HEAD_DIM ** -0.5
LOG2E = math.log2(math.e)
Q_SCALE = ATTN_SCALE * LOG2E
LAM_INIT0 = 0.8 - 0.6 * math.exp(-0.3 * 0)
NEG = -1e30

LANES = 128
MOD_ROWS = 16
VMEM_LIMIT = 56 * 1024 * 1024

QA_PERM = (0, 4, 1, 5, 2, 6, 3, 7)

NAT_ROWS = 4
NAT_Q = NAT_ROWS * GRID_W
NAT_WIN_ROWS = NAT_ROWS + WIN_H
NAT_WIN = NAT_WIN_ROWS * GRID_W
NAT_TILES = NAT_WIN_ROWS // 2
_NAT_DR0 = [min(max(r0 - WIN_H // 2, 0), GRID_H - NAT_WIN_ROWS) + 2 * j - (r0 + qr) + WIN_H - 1
            for r0 in range(0, GRID_H, NAT_ROWS) for qr in range(NAT_ROWS) for j in range(NAT_TILES)]
CB_OFFSET = -min(_NAT_DR0)
CB_ENTRIES = max(_NAT_DR0) + CB_OFFSET + 1


def _cparams(sem, flags=None):
    return pltpu.CompilerParams(dimension_semantics=sem, vmem_limit_bytes=VMEM_LIMIT, flags=flags)


def _dot(a, b):
    return jnp.dot(a, b, preferred_element_type=F32)


def _dot_nt(a, b):
    return lax.dot_general(a, b, (((1,), (1,)), ((), ())), preferred_element_type=F32)


def _split3(a):
    hi = a.astype(BF16)
    r1 = a - hi.astype(F32)
    mid = r1.astype(BF16)
    lo = (r1 - mid.astype(F32)).astype(BF16)
    return hi, mid, lo


def _layer_norm(z, g, b):
    mu = jnp.mean(z, axis=-1, keepdims=True)
    zc = z - mu
    var = jnp.mean(zc * zc, axis=-1, keepdims=True)
    return zc * lax.rsqrt(var + EPS) * g + b


@functools.lru_cache(maxsize=None)
def _rope_tables():
    pos = np.arange(SEQ)
    row = (pos // GRID_W).astype(np.float64)
    col = (pos % GRID_W).astype(np.float64)
    half = HEAD_DIM // 2
    freqs = np.power(ROPE_THETA, -np.arange(0, half, 2, dtype=np.float64) / half)
    def axis_angles(p):
        a = p[:, None] * freqs[None, :]
        return np.concatenate([a, a], axis=-1)
    ang = np.concatenate([axis_angles(row), axis_angles(col)], axis=-1)
    cos, sin = np.cos(ang), np.sin(ang)
    first = (np.arange(HEAD_DIM) % half) < (half // 2)
    sin_a = np.where(first[None, :], -sin, 0.0)
    sin_b = np.where(first[None, :], 0.0, sin)
    tile = lambda t: np.tile(t, (1, LANES // HEAD_DIM)).astype(np.float32)
    return tile(cos), tile(sin_a), tile(sin_b)


@functools.lru_cache(maxsize=None)
def _group_mean_matrix(width):
    g = np.arange(width) // HEAD_DIM
    return (g[:, None] == g[None, :]).astype(np.float32) / HEAD_DIM


@functools.lru_cache(maxsize=None)
def _dft_tables():
    t = np.arange(SEQ, dtype=np.int64)
    k = (t[:, None] * t[None, :]) % SEQ
    ang = 2.0 * np.pi * k.astype(np.float64) / SEQ
    ct = np.cos(ang).astype(np.float32)
    sn = (-np.sin(ang)).astype(np.float32)
    c = np.arange(D_GROUP_DIM, dtype=np.int64)
    kc = (c[:, None] * c[None, :]) % D_GROUP_DIM
    angc = 2.0 * np.pi * kc.astype(np.float64) / D_GROUP_DIM
    norm = 1.0 / math.sqrt(SEQ * D_GROUP_DIM)
    return ct, sn, (np.cos(angc) * norm).astype(np.float32), (np.sin(angc) * norm).astype(np.float32)


def _mod_kernel(c_ref, w_ref, b_ref, o_ref):
    c = c_ref[...]
    a = c / (1.0 + jnp.exp(-c))
    a_hi, a_mid, _ = _split3(a)
    y = _dot(jnp.concatenate([a_hi, a_mid], axis=0), w_ref[0].astype(BF16))
    o_ref[0, 0] = y[:MOD_ROWS] + y[MOD_ROWS:] + b_ref[0]


def _modulation(cs, mod_w, mod_b):
    n = 6 * D_MODEL
    return pl.pallas_call(
        _mod_kernel,
        grid=(DEPTH, 6),
        in_specs=[pl.BlockSpec((MOD_ROWS, D_MODEL), lambda l, j: (0, 0)),
                  pl.BlockSpec((1, D_MODEL, D_MODEL), lambda l, j: (l, 0, j)),
                  pl.BlockSpec((1, 1, D_MODEL), lambda l, j: (l, 0, j))],
        out_specs=pl.BlockSpec((1, 1, MOD_ROWS, D_MODEL), lambda l, j: (l, j, 0, 0)),
        out_shape=jax.ShapeDtypeStruct((DEPTH, 6, MOD_ROWS, D_MODEL), F32),
        compiler_params=_cparams(("arbitrary", "arbitrary")),
        name="modulation",
    )(cs, mod_w, mod_b.reshape(DEPTH, 1, n))


def _tok_spec(tm, width):
    return pl.BlockSpec((1, tm, width), lambda g, i: (g, i, 0))


MOD_SPEC = pl.BlockSpec((6, MOD_ROWS, D_MODEL), lambda g, i: (0, 0, 0))


def _mod_vectors(mod_ref, mod_row):
    r = pl.program_id(0) if mod_row is None else mod_row
    return [mod_ref[k, pl.ds(r, 1), :] for k in range(6)]


def _const_spec(shape):
    nd = len(shape)
    return pl.BlockSpec(shape, lambda g, i: (0,) * nd)


AB_QA, AB_QB, AB_KA, AB_VA, AB_KB, AB_VB, AB_W = 0, 512, 1024, 1152, 1280, 1792, 2304


TOKEN_CHAIN_ROWS = 256


def _store_v_ext(o_ref, rows, v):
    ones = jnp.ones((v.shape[0], LANES), o_ref.dtype)
    for j in range(v.shape[1] // LANES):
        o_ref[0, rows, 2 * j * LANES:(2 * j + 1) * LANES] = v[:, j * LANES:(j + 1) * LANES].astype(o_ref.dtype)
        o_ref[0, rows, (2 * j + 1) * LANES:(2 * j + 2) * LANES] = ones


def _rope_chunk(c, cos, sin_a, sin_b):
    return c * cos + pltpu.roll(c, LANES - 16, axis=1) * sin_a + pltpu.roll(c, 16, axis=1) * sin_b


def _inproj_ab_kernel(*refs, rope, mod_row):
    if rope:
        (x_ref, mod_ref, w_ref, gq_ref, gk_ref, g512_ref, g128_ref, cos_ref, sa_ref, sb_ref,
         qa_ref, qb_ref, ka_ref, kb_ref, va_ref, vb_ref) = refs
    else:
        (x_ref, mod_ref, w_ref, gq_ref, gk_ref, g512_ref, g128_ref,
         qa_ref, qb_ref, ka_ref, kb_ref, va_ref, vb_ref) = refs

    def rms(v, g_ref, gm_ref):
        ms = _dot((v * v).astype(BF16), gm_ref[...])
        return v * lax.rsqrt(ms + EPS) * g_ref[...]

    shift, scale = _mod_vectors(mod_ref, mod_row)[0:2]
    lo = lax.broadcasted_iota(jnp.int32, (1, LANES), 1) < HEAD_DIM
    tm = x_ref.shape[1]
    tc = min(TOKEN_CHAIN_ROWS, tm)

    def project(rb):
        rows = slice(rb * tc, (rb + 1) * tc)
        u = (x_ref[0, rows, :] * (1.0 + scale) + shift).astype(BF16)
        return _dot(u, w_ref[...])

    def epilogue(rb, h):
        rows = slice(rb * tc, (rb + 1) * tc)
        if rope:
            cos, sin_a, sin_b = cos_ref[rows, :], sa_ref[rows, :], sb_ref[rows, :]

        def chunks(v, q_scale):
            out = []
            for j in range(v.shape[1] // LANES):
                c = v[:, j * LANES:(j + 1) * LANES]
                if rope:
                    c = _rope_chunk(c, cos, sin_a, sin_b)
                out.append(c * q_scale if q_scale != 1.0 else c)
            return out

        def store(o_ref, cs):
            for j, c in enumerate(cs):
                o_ref[0, rows, j * LANES:(j + 1) * LANES] = c.astype(o_ref.dtype)

        qa = chunks(rms(h[:, AB_QA:AB_QB], gq_ref, g512_ref), Q_SCALE)
        qa_perm = []
        for p in range(A_HEADS // 2):
            a, b = qa[p // 2], qa[A_HEADS // 4 + p // 2]
            if p % 2 == 0:
                qa_perm.append(jnp.where(lo, a, pltpu.roll(b, HEAD_DIM, axis=1)))
            else:
                qa_perm.append(jnp.where(lo, pltpu.roll(a, HEAD_DIM, axis=1), b))
        store(qa_ref, qa_perm)
        store(qb_ref, chunks(h[:, AB_QB:AB_KA], Q_SCALE))
        store(ka_ref, chunks(rms(h[:, AB_KA:AB_VA], gk_ref, g128_ref), 1.0))
        store(kb_ref, chunks(h[:, AB_KB:AB_VB], 1.0))
        _store_v_ext(va_ref, rows, h[:, AB_VA:AB_KB])
        _store_v_ext(vb_ref, rows, h[:, AB_VB:AB_W])

    for rb in range(tm // tc):
        epilogue(rb, project(rb))


def _inproj_ab(tok, mod, mod_row, w, gq, gk, rope_tabs, tm, name):
    g, r, _ = tok.shape
    rope = rope_tabs is not None
    consts = [w, gq, gk, jnp.asarray(_group_mean_matrix(512), BF16), jnp.asarray(_group_mean_matrix(128), BF16)]
    in_specs = [_tok_spec(tm, D_MODEL), MOD_SPEC] + [_const_spec(c.shape) for c in consts]
    args = [tok, mod] + consts
    if rope:
        in_specs += [pl.BlockSpec((tm, LANES), lambda gg, i: (i, 0))] * 3
        args += list(rope_tabs)
    widths = (512, 512, 128, 512, 2 * 128, 2 * 512)
    return pl.pallas_call(
        functools.partial(_inproj_ab_kernel, rope=rope, mod_row=mod_row),
        grid=(g, r // tm),
        in_specs=in_specs,
        out_specs=[_tok_spec(tm, wd) for wd in widths],
        out_shape=[jax.ShapeDtypeStruct((g, r, wd), BF16) for wd in widths],
        compiler_params=_cparams(("parallel", "parallel")),
        name=name,
    )(*args)


def _scores(q2, kv_pairs, bias=None):
    scores = [_dot_nt(q2, k) for k, _ in kv_pairs]
    if bias is not None:
        scores[0] = scores[0] + bias
    return scores


def _softmax_pv(q2, kv_pairs, bias=None, scores=None):
    if scores is None:
        scores = _scores(q2, kv_pairs, bias)
    m = scores[0].max(axis=-1, keepdims=True)
    for s in scores[1:]:
        m = jnp.maximum(m, s.max(axis=-1, keepdims=True))
    acc = None
    for s, (_, v) in zip(scores, kv_pairs):
        a = _dot(jnp.exp2(s - m).astype(BF16), v)
        acc = a if acc is None else acc + a
    return acc[:, :LANES] / acc[:, LANES:]


def _run_chains(chains, lookahead):
    pending = []
    for start, finish in chains:
        pending.append((finish, start()))
        if len(pending) > lookahead:
            fin, scores = pending.pop(0)
            fin(scores)
    for fin, scores in pending:
        fin(scores)


ATTN_LOOKAHEAD = 3


def _split_pair(qp, lo):
    zero = jnp.zeros_like(qp)
    return jnp.concatenate([jnp.where(lo, qp, zero), jnp.where(lo, zero, qp)], axis=0)


ATTN_CHAIN_ROWS = 256


def _chain_schedule(rows, n_pairs, chain_rows):
    blocks = [(r0, min(chain_rows, rows - r0)) for r0 in range(0, rows, chain_rows)]
    sched = [(p, r0, n) for r0, n in blocks for p in range(n_pairs)]
    if len(blocks) < 2:
        return sched
    (p, r0, n), rest = sched[0], sched[1:]
    sched = [(p, r0, n // 2), (p, r0 + n // 2, n - n // 2)] + rest
    (p, r0, n), rest = sched[-1], sched[:-1]
    return rest + [(p, r0, n - n // 2), (p, r0 + n - n // 2, n // 2)]


def _attn_ab_kernel(*refs, with_x, n_casts):
    if n_casts:
        cast_in = refs[len(refs) - 2 * n_casts - 1:len(refs) - n_casts - 1]
        cast_out = refs[len(refs) - n_casts:]
        refs = refs[:len(refs) - 2 * n_casts - 1] + (refs[len(refs) - n_casts - 1],)
        for src, dst in zip(cast_in, cast_out):
            dst[...] = src[...].astype(dst.dtype)
    if with_x:
        (qa_ref, qb_ref, kax_ref, vax_ref, kbx_ref, vbx_ref, kac_ref, vac_ref, kbc_ref, vbc_ref,
         lam_ref, subln_ref, o_ref) = refs
    else:
        (qa_ref, qb_ref, kac_ref, vac_ref, kbc_ref, vbc_ref, lam_ref, subln_ref, o_ref) = refs
    lo = lax.broadcasted_iota(jnp.int32, (1, LANES), 1) < HEAD_DIM

    lp = lam_ref[...]
    lam = (jnp.exp(jnp.sum(lp[0:1] * lp[1:2], axis=-1, keepdims=True))
           - jnp.exp(jnp.sum(lp[2:3] * lp[3:4], axis=-1, keepdims=True)) + LAM_INIT0)

    def gqa_chain(p, r0, n):
        rows = slice(r0, r0 + n)
        sl = slice(p * LANES, (p + 1) * LANES)
        kv = [(kac_ref[0], vac_ref[0])]
        if with_x:
            kv.append((kax_ref[0], vax_ref[0]))

        def finish(scores):
            o2 = _softmax_pv(None, kv, scores=scores)
            o_ref[0, rows, sl] = jnp.where(lo, o2[:n], o2[n:]).astype(o_ref.dtype)

        return (lambda: _scores(_split_pair(qa_ref[0, rows, sl], lo), kv)), finish

    def diff_chain(i, r0, n):
        rows = slice(r0, r0 + n)
        sl = slice(i * LANES, (i + 1) * LANES)
        sl2 = slice(2 * i * LANES, (2 * i + 2) * LANES)
        kv = [(kbc_ref[0, :, sl], vbc_ref[0, :, sl2])]
        if with_x:
            kv.append((kbx_ref[0, :, sl], vbx_ref[0, :, sl2]))

        def finish(scores):
            o2 = _softmax_pv(None, kv, scores=scores)
            o = o2[:n] - lam * o2[n:]
            ms = jnp.mean(o * o, axis=-1, keepdims=True)
            o = o * lax.rsqrt(ms + EPS) * subln_ref[...] * (1.0 - LAM_INIT0)
            o_ref[0, rows, 512 + i * LANES:512 + (i + 1) * LANES] = o.astype(o_ref.dtype)

        return (lambda: _scores(_split_pair(qb_ref[0, rows, sl], lo), kv)), finish

    n_gqa = A_HEADS // 2
    _run_chains([gqa_chain(pair, r0, n) if pair < n_gqa else diff_chain(pair - n_gqa, r0, n)
                 for pair, r0, n in _chain_schedule(qa_ref.shape[1], n_gqa + B_HEADS, ATTN_CHAIN_ROWS)],
                ATTN_LOOKAHEAD)


def _attn_ab(q_parts, x_kv, c_kv, lam_p, subln, tq, name, casts=()):
    qa, qb = q_parts
    b, nq, _ = qa.shape
    with_x = x_kv is not None
    steps = nq // tq

    def full(a):
        return pl.BlockSpec((1,) + a.shape[1:], lambda bb, i: (bb, 0, 0))

    args = [qa, qb]
    in_specs = [pl.BlockSpec((1, tq, 512), lambda bb, i: (bb, i, 0))] * 2
    if with_x:
        args += list(x_kv)
        in_specs += [full(a) for a in x_kv]
    args += list(c_kv) + [lam_p, subln]
    in_specs += [full(a) for a in c_kv] + [pl.BlockSpec(lam_p.shape, lambda bb, i: (0, 0)),
                                           pl.BlockSpec(subln.shape, lambda bb, i: (0, 0))]
    cast_specs = [pl.BlockSpec((a.shape[0] // (b * steps), a.shape[1]), lambda bb, i: (bb * steps + i, 0))
                  for a in casts]
    outs = pl.pallas_call(
        functools.partial(_attn_ab_kernel, with_x=with_x, n_casts=len(casts)),
        grid=(b, steps),
        in_specs=in_specs + cast_specs,
        out_specs=[pl.BlockSpec((1, tq, D_MODEL), lambda bb, i: (bb, i, 0))] + cast_specs,
        out_shape=[jax.ShapeDtypeStruct((b, nq, D_MODEL), BF16)]
                  + [jax.ShapeDtypeStruct(a.shape, BF16) for a in casts],
        compiler_params=_cparams(("parallel", "parallel")),
        name=name,
    )(*args, *casts)
    return outs[0] if not casts else outs


FFN_CHUNK = 1024
POST_EDGE_PIECES = 2


def _post_kernel(*refs, n_parts, mod_row):
    o_refs = refs[:n_parts]
    wo_refs = refs[n_parts:2 * n_parts]
    x_ref, mod_ref, g_ref, b_ref, w1_ref, w2_ref, y_ref = refs[2 * n_parts:]
    _, _, gate1, shift2, scale2, gate2 = _mod_vectors(mod_ref, mod_row)
    tm = x_ref.shape[1]
    n_chunks = FFN_DIM // FFN_CHUNK
    pieces = [slice(r, r + tm // POST_EDGE_PIECES) for r in range(0, tm, tm // POST_EDGE_PIECES)]
    sq_relu = lambda t: jnp.square(jnp.maximum(t, 0.0)).astype(BF16)

    ys = []
    for rows in pieces:
        y = _dot(o_refs[0][0, rows, :], wo_refs[0][...])
        for o_ref, w_ref in zip(o_refs[1:], wo_refs[1:]):
            y = y + _dot(o_ref[0, rows, :], w_ref[...])
        ys.append(y)
    x1s, us, h0s = [], [], []
    for rows, y in zip(pieces, ys):
        x1s.append(_layer_norm(ALPHA * x_ref[0, rows, :] + gate1 * y, g_ref[0:1, :], b_ref[0:1, :]))
        us.append((x1s[-1] * (1.0 + scale2) + shift2).astype(BF16))
        h0s.append(sq_relu(_dot(us[-1], w1_ref[0, :, 0:FFN_CHUNK])))
    u = jnp.concatenate(us, axis=0)
    acc = _dot(jnp.concatenate(h0s, axis=0), w2_ref[0, 0:FFN_CHUNK, :])
    for c in range(1, n_chunks - 1):
        sl = slice(c * FFN_CHUNK, (c + 1) * FFN_CHUNK)
        acc = acc + _dot(sq_relu(_dot(u, w1_ref[0, :, sl])), w2_ref[0, sl, :])
    sl = slice((n_chunks - 1) * FFN_CHUNK, n_chunks * FFN_CHUNK)
    h_last = sq_relu(_dot(u, w1_ref[0, :, sl]))
    fs = [acc[rows] + _dot(h_last[rows], w2_ref[0, sl, :]) for rows in pieces]
    for rows, x1, f in zip(pieces, x1s, fs):
        y_ref[0, rows, :] = _layer_norm(ALPHA * x1 + gate2 * f, g_ref[1:2, :], b_ref[1:2, :])


def _post(o_parts, wo_parts, resid, mod, mod_row, ln_g, ln_b, w1, w2, layer, tm, name):
    g, r, _ = resid.shape
    n = len(o_parts)
    single = pl.Buffered(1)
    layer_w = lambda a: pl.BlockSpec((1,) + a.shape[1:], lambda gg, i: (layer, 0, 0), pipeline_mode=single)
    const1 = lambda a: pl.BlockSpec(a.shape, lambda gg, i: (0,) * a.ndim, pipeline_mode=single)
    in_specs = ([_tok_spec(tm, o.shape[2]) for o in o_parts] + [const1(w) for w in wo_parts]
                + [_tok_spec(tm, D_MODEL), MOD_SPEC, _const_spec(ln_g.shape), _const_spec(ln_b.shape),
                   layer_w(w1), layer_w(w2)])
    return pl.pallas_call(
        functools.partial(_post_kernel, n_parts=n, mod_row=mod_row),
        grid=(g, r // tm),
        in_specs=in_specs,
        out_specs=_tok_spec(tm, D_MODEL),
        out_shape=jax.ShapeDtypeStruct((g, r, D_MODEL), F32),
        compiler_params=_cparams(("parallel", "parallel")),
        name=name,
    )(*o_parts, *wo_parts, resid, mod, ln_g, ln_b, w1, w2)


CD_Q, CD_F, CD_K, CD_V, CD_W = 0, 512, 1024, 1536, 2048


def _inproj_cd_kernel(*refs, with_q, mod_row):
    if with_q:
        x_ref, mod_ref, w_ref, wa_ref, wb_ref, q_ref, fa_ref, fb_ref, k_ref, v_ref = refs
    else:
        x_ref, mod_ref, w_ref, k_ref, v_ref = refs
    shift, scale = _mod_vectors(mod_ref, mod_row)[0:2]
    u = (x_ref[0] * (1.0 + scale) + shift).astype(BF16)
    if with_q:
        q_ref[0] = (_dot(u, w_ref[:, CD_Q:CD_F]) * Q_SCALE).astype(q_ref.dtype)
        fa_ref[0] = _dot(u, wa_ref[...]).astype(fa_ref.dtype)
        fb_ref[0] = _dot(u, wb_ref[...]).astype(fb_ref.dtype)
    kv = _dot(u, w_ref[:, CD_K:CD_W])
    k_ref[0] = kv[:, :CD_V - CD_K].astype(k_ref.dtype)
    _store_v_ext(v_ref, slice(None), kv[:, CD_V - CD_K:])


def _inproj_cd(tok, mod, mod_row, w, w_ab, tm, name):
    g, r, _ = tok.shape
    with_q = w_ab is not None
    weights = [w] + (list(w_ab) if with_q else [])
    widths = ((512, 512, 512) if with_q else ()) + (512, 1024)
    return pl.pallas_call(
        functools.partial(_inproj_cd_kernel, with_q=with_q, mod_row=mod_row),
        grid=(g, r // tm),
        in_specs=[_tok_spec(tm, D_MODEL), MOD_SPEC] + [_const_spec(a.shape) for a in weights],
        out_specs=[_tok_spec(tm, wd) for wd in widths],
        out_shape=[jax.ShapeDtypeStruct((g, r, wd), BF16) for wd in widths],
        compiler_params=_cparams(("parallel", "parallel")),
        name=name,
    )(tok, mod, *weights)


def _dft_weight_kernel(w_ref, cc_ref, sc_ref, wa_ref, wb_ref):
    c_hi, c_mid, _ = _split3(cc_ref[...])
    s_hi, s_mid, _ = _split3(sc_ref[...])
    for gi in range(D_GROUPS):
        sl = slice(gi * D_GROUP_DIM, (gi + 1) * D_GROUP_DIM)
        w_hi, w_mid, _ = _split3(w_ref[:, sl])
        wa_ref[:, sl] = (_dot(w_hi, c_hi) + (_dot(w_mid, c_hi) + _dot(w_hi, c_mid))).astype(BF16)
        wb_ref[:, sl] = (_dot(w_hi, s_hi) + (_dot(w_mid, s_hi) + _dot(w_hi, s_mid))).astype(BF16)


def _dft_weights(w_in, cc, sc):
    width = CD_K - CD_F
    out = pl.BlockSpec((D_MODEL, width), lambda i: (0, 0))
    return pl.pallas_call(
        _dft_weight_kernel,
        grid=(1,),
        in_specs=[pl.BlockSpec((D_MODEL, width), lambda i: (0, CD_F // width)),
                  pl.BlockSpec(cc.shape, lambda i: (0, 0)), pl.BlockSpec(sc.shape, lambda i: (0, 0))],
        out_specs=[out, out],
        out_shape=[jax.ShapeDtypeStruct((D_MODEL, width), BF16)] * 2,
        compiler_params=_cparams(("arbitrary",)),
        name="dft_weights",
    )(w_in, cc, sc)


DFT_ROWS = 512


def _dft_time_kernel(a_ref, b_ref, ct_ref, sn_ref, o_ref):
    a = a_ref[0]
    bm = b_ref[0]
    for r in range(SEQ // DFT_ROWS):
        sl = slice(r * DFT_ROWS, (r + 1) * DFT_ROWS)
        y = _dot(ct_ref[sl, :], a) + _dot(sn_ref[sl, :], bm)
        o_ref[0, sl, :] = y.astype(o_ref.dtype)


def _dft_time(a, bm, ct, sn):
    b = a.shape[0]
    single = pl.Buffered(1)
    tok = pl.BlockSpec((1, SEQ, 512), lambda bb: (bb, 0, 0))
    tab = pl.BlockSpec((SEQ, SEQ), lambda bb: (0, 0), pipeline_mode=single)
    return pl.pallas_call(
        _dft_time_kernel,
        grid=(b,),
        in_specs=[tok, tok, tab, tab],
        out_specs=tok,
        out_shape=jax.ShapeDtypeStruct((b, SEQ, 512), BF16),
        compiler_params=_cparams(("parallel",)),
        name="dft_time",
    )(a, bm, ct, sn)


def _bias_table_kernel(r_ref, o_ref):
    n = GRID_W * LANES
    row = lax.broadcasted_iota(jnp.int32, (2 * 32, n), 0)
    lane = lax.broadcasted_iota(jnp.int32, (2 * 32, n), 1)
    e_row, i_row = row >> 5, row & 31
    c = lane >> 7
    e_lane = (lane >> 6) & 1
    kc = lane & (GRID_W - 1)
    sel = jnp.where((e_row == e_lane) & (kc - c + (WIN_W - 1) == i_row), 1.0, 0.0).astype(BF16)
    hi, mid, lo = _split3(r_ref[...])
    t = _dot(hi, sel) + _dot(mid, sel) + _dot(lo, sel)
    c1 = c[0:1]
    kc1 = kc[0:1]
    cstart = jnp.clip(c1 - WIN_W // 2, 0, GRID_W - WIN_W)
    col_ok = (kc1 >= cstart) & (kc1 < cstart + WIN_W)
    t = jnp.where(col_ok, t * LOG2E, NEG)
    for qc in range(GRID_W):
        o_ref[:, qc, :] = t[:, qc * LANES:(qc + 1) * LANES]


def _bias_table(rpb):
    n_dr = 2 * WIN_H - 1
    p = jnp.pad(rpb, ((0, 0), (CB_OFFSET, CB_ENTRIES + 1 - n_dr - CB_OFFSET), (0, 1)))
    r2 = jnp.stack([p[:, 0:CB_ENTRIES], p[:, 1:CB_ENTRIES + 1]], axis=2)
    r2 = r2.reshape(C_HEADS * CB_ENTRIES, 2 * 32)
    t = pl.pallas_call(
        _bias_table_kernel,
        out_shape=jax.ShapeDtypeStruct((C_HEADS * CB_ENTRIES, GRID_W, LANES), F32),
        compiler_params=pltpu.CompilerParams(vmem_limit_bytes=VMEM_LIMIT),
        name="natten_bias_table",
    )(r2)
    return t.reshape(C_HEADS, CB_ENTRIES, GRID_W, LANES)


NAT_BLOCKS = 4
NAT_LOOKAHEAD = 2


def _natten_block(blk, q_ref, k_ref, v_ref, kc_ref, vc_ref, cb_ref, o_ref, lo, e):
    rows = slice(blk * NAT_Q, (blk + 1) * NAT_Q)
    r0 = NAT_ROWS * (NAT_BLOCKS * pl.program_id(1) + blk)
    ws = jnp.clip(r0 - WIN_H // 2, 0, GRID_H - NAT_WIN_ROWS)
    koff = pl.multiple_of(ws * GRID_W, LANES)

    entry = []
    rmask = []
    for qr in range(NAT_ROWS):
        r = r0 + qr
        lo_r = jnp.clip(r - WIN_H // 2, 0, GRID_H - WIN_H) - r + (WIN_H - 1)
        ent_q, mask_q = [], []
        for j in range(NAT_TILES):
            dr0 = ws + 2 * j - r + (WIN_H - 1)
            ent_q.append(dr0 + CB_OFFSET)
            dr = dr0 + e
            ok = (dr >= lo_r) & (dr < lo_r + WIN_H)
            mask_q.append(jnp.where(ok, 0.0, NEG))
        entry.append(ent_q)
        rmask.append(mask_q)

    states = []
    for p in range(C_HEADS // 2):
        sl = slice(p * LANES, (p + 1) * LANES)
        sl2 = slice(2 * p * LANES, (2 * p + 2) * LANES)
        q2 = _split_pair(q_ref[0, rows, sl], lo)
        bias = jnp.concatenate(
            [jnp.concatenate([cb_ref[2 * p + hh, entry[qr][j]] + rmask[qr][j] for j in range(NAT_TILES)], axis=1)
             for hh in range(2) for qr in range(NAT_ROWS)], axis=0)
        kv = [(k_ref[0, pl.ds(koff, NAT_WIN), sl], v_ref[0, pl.ds(koff, NAT_WIN), sl2]),
              (kc_ref[0, :, sl], vc_ref[0, :, sl2])]
        states.append((sl, kv, _scores(q2, kv, bias)))

    def finish():
        for sl, kv, scores in states:
            o2 = _softmax_pv(None, kv, scores=scores)
            o_ref[0, rows, sl] = jnp.where(lo, o2[:NAT_Q], o2[NAT_Q:]).astype(o_ref.dtype)

    return finish


def _natten_kernel(q_ref, k_ref, v_ref, kc_ref, vc_ref, cb_ref, o_ref):
    lane = lax.broadcasted_iota(jnp.int32, (1, LANES), 1)
    lo = lane < HEAD_DIM
    e = jnp.where(lo, 0, 1)
    pending = []
    for blk in range(NAT_BLOCKS):
        pending.append(_natten_block(blk, q_ref, k_ref, v_ref, kc_ref, vc_ref, cb_ref, o_ref, lo, e))
        if len(pending) > NAT_LOOKAHEAD:
            pending.pop(0)()
    for finish in pending:
        finish()


def _natten(q, k, v, kc, vc, cb):
    b = q.shape[0]
    tq = NAT_BLOCKS * NAT_Q
    qspec = pl.BlockSpec((1, tq, 512), lambda bb, i: (bb, i, 0))
    full = lambda a: pl.BlockSpec((1,) + a.shape[1:], lambda bb, i: (bb, 0, 0))
    return pl.pallas_call(
        _natten_kernel,
        grid=(b, SEQ // tq),
        in_specs=[qspec, full(k), full(v), full(kc), full(vc),
                  pl.BlockSpec(cb.shape, lambda bb, i: (0, 0, 0, 0))],
        out_specs=qspec,
        out_shape=jax.ShapeDtypeStruct((b, SEQ, 512), BF16),
        compiler_params=_cparams(("parallel", "parallel")),
        name="natten",
    )(q, k, v, kc, vc, cb)


def kernel(x, c, ctx, c_ctx, mod_w, mod_b, ln_g, ln_b, ffn_w1, ffn_w2, ab_w_in, ab_w_out, a_q_norm, a_k_norm,
           b_lambda, b_subln, cd_w_in, cd_w_out, c_rpb):
    b = x.shape[0]
    nctx = b * CTX_LEN
    tm = 512
    tmc = min(tm, nctx)

    cs = jnp.concatenate([c, c_ctx[None, :], jnp.zeros((MOD_ROWS - b - 1, D_MODEL), F32)], axis=0)
    mods = _modulation(cs, mod_w, mod_b)
    ctx_row = b

    cflat = ctx.reshape(1, nctx, D_MODEL)

    perm = np.asarray(QA_PERM)
    w_ab = ab_w_in.reshape(D_MODEL, AB_W).astype(BF16)
    gq =jnp.tile(a_q_norm[0], A_HEADS)[None, :]
    gk = jnp.tile(a_k_norm[0], A_KV_HEADS)[None, :]
    rope_tabs = tuple(jnp.asarray(t) for t in _rope_tables())
    subln = b_subln[0][None, :]
    mod0 = mods[0]

    xq = _inproj_ab(x, mod0, None, w_ab, gq, gk, rope_tabs, 2 * tm, "inproj_ab_x")
    cq = _inproj_ab(cflat, mod0, ctx_row, w_ab, gq, gk, None, tmc, "inproj_ab_ctx")
    cq = [a.reshape(b, CTX_LEN, a.shape[2]) for a in cq]
    qa_x, qb_x, ka_x, kb_x, va_x, vb_x = xq
    qa_c, qb_c, ka_c, kb_c, va_c, vb_c = cq
    c_kv = (ka_c, va_c, kb_c, vb_c)
    o_x, w1, w2, w_cd, w_out_cd, w_out = _attn_ab(
        (qa_x, qb_x), (ka_x, va_x, kb_x, vb_x), c_kv, b_lambda[0], subln, 512, "attn_ab_x",
        casts=(ffn_w1.reshape(DEPTH * D_MODEL, FFN_DIM), ffn_w2.reshape(DEPTH * FFN_DIM, D_MODEL),
               cd_w_in.reshape(D_MODEL, CD_W), cd_w_out.reshape(D_MODEL, D_MODEL),
               ab_w_out.reshape(D_MODEL, D_MODEL)))
    w_out_ab = jnp.concatenate([w_out[:512].reshape(A_HEADS, HEAD_DIM, D_MODEL)[perm].reshape(512, D_MODEL),
                                w_out[512:]], axis=0)
    w1 = w1.reshape(DEPTH, D_MODEL, FFN_DIM)
    w2 = w2.reshape(DEPTH, FFN_DIM, D_MODEL)
    o_c = _attn_ab((qa_c, qb_c), None, c_kv, b_lambda[0], subln, CTX_LEN, "attn_ab_ctx")

    x2 = _post([o_x], [w_out_ab], x, mod0, None, ln_g[0], ln_b[0], w1, w2, 0, tm, "post0_x")
    c2 = _post([o_c.reshape(1, nctx, D_MODEL)], [w_out_ab], cflat, mod0, ctx_row, ln_g[0], ln_b[0], w1, w2, 0,
               tmc, "post0_ctx")

    mod1 = mods[1]
    w_in = cd_w_in.reshape(D_MODEL, CD_W)
    ct, sn, cc, sc = _dft_tables()
    w_fab = _dft_weights(w_in, jnp.asarray(cc), jnp.asarray(sc))
    q_n, f_a, f_b, k_n, v_n = _inproj_cd(x2, mod1, None, w_cd, w_fab, 2 * tm, "inproj_cd_x")
    kc_n, vc_n = _inproj_cd(c2, mod1, ctx_row, w_cd, None, tmc, "inproj_cd_ctx")
    kc_n = kc_n.reshape(b, CTX_LEN, 512)
    vc_n = vc_n.reshape(b, CTX_LEN, 1024)

    o_d = _dft_time(f_a, f_b, jnp.asarray(ct).astype(BF16), jnp.asarray(sn).astype(BF16))
    cb = _bias_table(c_rpb[0])
    o_n = _natten(q_n, k_n, v_n, kc_n, vc_n, cb)

    return _post([o_n, o_d], [w_out_cd[:512], w_out_cd[512:]], x2, mod1, None, ln_g[1], ln_b[1], w1, w2, 1, tm, "post1_x")
```
